```python
import math
import jax, jax.numpy as jnp
from jax import lax
import numpy as np

D_MODEL = 1024
BATCH = 16
SEQ = 4096
DEPTH = 1

MLA_HEADS = 4
MLA_NOPE = 128
MLA_ROPE = 64
MLA_V = 128
Q_LORA = D_MODEL // 4
KV_LORA = D_MODEL // 8
MLA_QK = MLA_NOPE + MLA_ROPE
MLA_WIDTH = MLA_HEADS * MLA_V

DIFF_HEADS = 4
DIFF_D = 64
DIFF_V = 2 * DIFF_D
DIFF_QK_WIDTH = DIFF_HEADS * 2 * DIFF_D
DIFF_WIDTH = DIFF_HEADS * DIFF_V

MIX_WIDTH = MLA_WIDTH + DIFF_WIDTH
IN_WIDTH = Q_LORA + KV_LORA + MLA_ROPE + 2 * DIFF_QK_WIDTH + DIFF_WIDTH
SPLITS = (Q_LORA, Q_LORA + KV_LORA, Q_LORA + KV_LORA + MLA_ROPE,
          Q_LORA + KV_LORA + MLA_ROPE + DIFF_QK_WIDTH,
          Q_LORA + KV_LORA + MLA_ROPE + 2 * DIFF_QK_WIDTH)

D_FF = 4 * D_MODEL
N_BUCKETS = 32
MAX_DISTANCE = 128
ROPE_THETA = 10000.0
EPS = 1e-6
Q_BLOCK = 128

kernel_name = "hybrid_mla_diffattn_encoder_layer"


def _rmsnorm(x, w):
    x32 = x.astype(jnp.float32)
    y = x32 * lax.rsqrt(jnp.mean(x32 * x32, axis=-1, keepdims=True) + EPS)
    return (y * w.astype(jnp.float32)).astype(x.dtype)


def _rotate_half(x):
    x1, x2 = jnp.split(x, 2, axis=-1)
    return jnp.concatenate([-x2, x1], axis=-1)


def _rope_tables(seq, dtype):
    inv = ROPE_THETA ** (-jnp.arange(0, MLA_ROPE, 2, dtype=jnp.float32) / MLA_ROPE)
    ang = jnp.arange(seq, dtype=jnp.float32)[:, None] * inv[None, :]
    ang = jnp.concatenate([ang, ang], axis=-1)
    return jnp.cos(ang).astype(dtype), jnp.sin(ang).astype(dtype)


def _t5_bucket(rel):
    half = N_BUCKETS // 2
    ret = jnp.where(rel > 0, half, 0)
    n = jnp.abs(rel)
    max_exact = half // 2
    large = max_exact + (jnp.log(jnp.maximum(n, 1).astype(jnp.float32) / max_exact)
                         / math.log(MAX_DISTANCE / max_exact)
                         * (half - max_exact)).astype(jnp.int32)
    large = jnp.minimum(large, half - 1)
    return ret + jnp.where(n < max_exact, n, large)


def _query_blocks(q):
    b, s = q.shape[:2]
    qb = q.reshape((b, s // Q_BLOCK, Q_BLOCK) + q.shape[2:])
    return jnp.moveaxis(qb, 1, 0)


def _merge_blocks(o):
    o = jnp.moveaxis(o, 0, 1)
    return o.reshape((o.shape[0], o.shape[1] * o.shape[2]) + o.shape[3:])


def _mla_attention(q, k, v):
    scale = MLA_QK ** -0.5

    def one(qblk):
        s = jnp.einsum('bqhd,bkhd->bhqk', qblk, k).astype(jnp.float32) * scale
        p = jax.nn.softmax(s, axis=-1).astype(v.dtype)
        return jnp.einsum('bhqk,bkhd->bqhd', p, v)

    return _merge_blocks(lax.map(one, _query_blocks(q)))


def _diff_attention(q, k, v, rel_bias, lam):
    seq = q.shape[1]
    scale = DIFF_D ** -0.5
    kpos = jnp.arange(seq, dtype=jnp.int32)
    starts = jnp.arange(seq // Q_BLOCK, dtype=jnp.int32) * Q_BLOCK
    table = rel_bias.astype(jnp.float32)

    def one(args):
        qblk, start = args
        qpos = start + jnp.arange(Q_BLOCK, dtype=jnp.int32)
        bucket = _t5_bucket(kpos[None, :] - qpos[:, None])
        bias = jnp.moveaxis(table[bucket], -1, 0)
        s = jnp.einsum('bqhmd,bkhmd->bmhqk', qblk, k).astype(jnp.float32) * scale + bias
        p = jax.nn.softmax(s, axis=-1)
        a = p[:, 0] - lam * p[:, 1]
        return jnp.einsum('bhqk,bkhd->bqhd', a.astype(v.dtype), v)

    return _merge_blocks(lax.map(one, (_query_blocks(q), starts)))


def setup_inputs(seed: int = 0) -> dict:
    key = jax.random.key(seed)
    ks = jax.random.split(key, 24)
    f32 = jnp.float32

    def w(k, shape, fan_in):
        return jax.random.normal(k, shape, f32) * fan_in ** -0.5

    def gain(k, shape):
        return 1.0 + 0.02 * jax.random.normal(k, shape, f32)

    L = DEPTH
    return {
        "x": jax.random.normal(ks[0], (BATCH, SEQ, D_MODEL), f32),
        "attn_norm_w": gain(ks[1], (L, D_MODEL)),
        "w_in": w(ks[2], (L, D_MODEL, IN_WIDTH), D_MODEL),
        "q_a_norm_w": gain(ks[3], (L, Q_LORA)),
        "w_uq": w(ks[4], (L, Q_LORA, MLA_HEADS * MLA_QK), Q_LORA),
        "kv_a_norm_w": gain(ks[5], (L, KV_LORA)),
        "w_ukv": w(ks[6], (L, KV_LORA, MLA_HEADS * (MLA_NOPE + MLA_V)), KV_LORA),
        "mla_q_norm_w": gain(ks[7], (L, MLA_QK)),
        "mla_k_norm_w": gain(ks[8], (L, MLA_QK)),
        "diff_q_norm_w": gain(ks[9], (L, DIFF_D)),
        "diff_k_norm_w": gain(ks[10], (L, DIFF_D)),
        "lambda_q1": 0.1 * jax.random.normal(ks[11], (L, DIFF_D), f32),
        "lambda_k1": 0.1 * jax.random.normal(ks[12], (L, DIFF_D), f32),
        "lambda_q2": 0.1 * jax.random.normal(ks[13], (L, DIFF_D), f32),
        "lambda_k2": 0.1 * jax.random.normal(ks[14], (L, DIFF_D), f32),
        "diff_out_norm_w": gain(ks[15], (L, DIFF_V)),
        "w_out": w(ks[16], (L, MIX_WIDTH, D_MODEL), MIX_WIDTH),
        "mlp_norm_w": gain(ks[17], (L, D_MODEL)),
        "w_up": w(ks[18], (L, D_MODEL, D_FF), D_MODEL),
        "w_down": w(ks[19], (L, D_FF, D_MODEL), D_FF),
        "rel_bias": 0.5 * jax.random.normal(ks[20], (N_BUCKETS, DIFF_HEADS), f32),
    }


def reference(x, attn_norm_w, w_in, q_a_norm_w, w_uq, kv_a_norm_w, w_ukv,
              mla_q_norm_w, mla_k_norm_w, diff_q_norm_w, diff_k_norm_w,
              lambda_q1, lambda_k1, lambda_q2, lambda_k2, diff_out_norm_w,
              w_out, mlp_norm_w, w_up, w_down, rel_bias):
    b, s, _ = x.shape
    cos, sin = _rope_tables(s, x.dtype)
    for layer in range(DEPTH):
        lam_init = 0.8 - 0.6 * math.exp(-0.3 * layer)

        h = _rmsnorm(x, attn_norm_w[layer])
        proj = h @ w_in[layer]
        c_q, c_kv, k_rope, dq, dk, dv = jnp.split(proj, SPLITS, axis=-1)

        q = (_rmsnorm(c_q, q_a_norm_w[layer]) @ w_uq[layer]).reshape(b, s, MLA_HEADS, MLA_QK)
        q_nope, q_rope = q[..., :MLA_NOPE], q[..., MLA_NOPE:]
        q_rope = q_rope * cos[:, None, :] + _rotate_half(q_rope) * sin[:, None, :]
        kv = (_rmsnorm(c_kv, kv_a_norm_w[layer]) @ w_ukv[layer]).reshape(b, s, MLA_HEADS, MLA_NOPE + MLA_V)
        k_nope, v_mla = kv[..., :MLA_NOPE], kv[..., MLA_NOPE:]
        k_rope = k_rope * cos + _rotate_half(k_rope) * sin
        k_rope = jnp.broadcast_to(k_rope[:, :, None, :], (b, s, MLA_HEADS, MLA_ROPE))
        q_m = _rmsnorm(jnp.concatenate([q_nope, q_rope], axis=-1), mla_q_norm_w[layer])
        k_m = _rmsnorm(jnp.concatenate([k_nope, k_rope], axis=-1), mla_k_norm_w[layer])
        o_mla = _mla_attention(q_m, k_m, v_mla).reshape(b, s, MLA_WIDTH)

        dq = _rmsnorm(dq.reshape(b, s, DIFF_HEADS, 2, DIFF_D), diff_q_norm_w[layer])
        dk = _rmsnorm(dk.reshape(b, s, DIFF_HEADS, 2, DIFF_D), diff_k_norm_w[layer])
        dv = dv.reshape(b, s, DIFF_HEADS, DIFF_V)
        lam = (jnp.exp(jnp.sum(lambda_q1[layer].astype(jnp.float32) * lambda_k1[layer].astype(jnp.float32)))
               - jnp.exp(jnp.sum(lambda_q2[layer].astype(jnp.float32) * lambda_k2[layer].astype(jnp.float32)))
               + lam_init)
        o_d = _diff_attention(dq, dk, dv, rel_bias, lam)
        o_diff = (_rmsnorm(o_d, diff_out_norm_w[layer]) * (1.0 - lam_init)).reshape(b, s, DIFF_WIDTH)

        x = x + jnp.concatenate([o_mla, o_diff], axis=-1) @ w_out[layer]

        h = _rmsnorm(x, mlp_norm_w[layer])
        x = x + jnp.square(jax.nn.relu(h @ w_up[layer])) @ w_down[layer]
    return x
```

```python
import functools
import math

import jax
import jax.numpy as jnp
from jax import lax
from jax.experimental import pallas as pl
from jax.experimental.pallas import tpu as pltpu

D_MODEL = 1024
DEPTH = 1
MLA_HEADS = 4
MLA_NOPE = 128
MLA_ROPE = 64
MLA_V = 128
Q_LORA = D_MODEL // 4
KV_LORA = D_MODEL // 8
MLA_QK = MLA_NOPE + MLA_ROPE
MLA_QK_PAD = 256
DIFF_HEADS = 4
DIFF_D = 64
DIFF_V = 2 * DIFF_D
DIFF_QK_WIDTH = DIFF_HEADS * 2 * DIFF_D
DIFF_WIDTH = DIFF_HEADS * DIFF_V
MLA_WIDTH = MLA_HEADS * MLA_V
D_FF = 4 * D_MODEL
N_BUCKETS = 32
MAX_DISTANCE = 128
ROPE_THETA = 10000.0
EPS = 1e-6
LOG2E = math.log2(math.e)

V7X_VMEM_LIMIT_BYTES = 56 * 1024 * 1024

PROJ_TOKENS = 512
ATTN_TQ = 256
ATTN_TK = 512
MLP_TOKENS = 1024
MLP_FF = 512

F32 = jnp.float32
BF16 = jnp.bfloat16


def _dot(a, b):
    return jnp.dot(a, b, preferred_element_type=F32)


def _dot_nt(a, b):
    return lax.dot_general(a, b, (((1,), (1,)), ((), ())), preferred_element_type=F32)


def _rotate_half(x):
    x1, x2 = jnp.split(x, 2, axis=-1)
    return jnp.concatenate([-x2, x1], axis=-1)


def _proj_kernel(x_ref, win_ref, wuq_ref, wukv_ref, cos_ref, sin_ref, gk_ref, gdk_ref,
                 qm_ref, km_ref, vm_ref, qd_ref, kd_ref, vd_ref):
    x = x_ref[0]
    h = x * lax.rsqrt(jnp.mean(x * x, axis=-1, keepdims=True) + EPS)
    y = _dot_nt(win_ref[...], h.astype(BF16))
    t = y.shape[1]
    cos = cos_ref[...]
    sin = sin_ref[...]

    o0 = 0
    c_q = y[o0:o0 + Q_LORA]
    o0 += Q_LORA
    c_kv = y[o0:o0 + KV_LORA]
    o0 += KV_LORA
    k_rope = y[o0:o0 + MLA_ROPE]
    o0 += MLA_ROPE
    k_rope_rot = y[o0:o0 + MLA_ROPE]
    o0 += MLA_ROPE
    dq = y[o0:o0 + DIFF_QK_WIDTH]
    o0 += DIFF_QK_WIDTH
    dk = y[o0:o0 + DIFF_QK_WIDTH]
    o0 += DIFF_QK_WIDTH
    dv = y[o0:o0 + DIFF_WIDTH]

    cq_n = c_q * lax.rsqrt(jnp.mean(c_q * c_q, axis=0, keepdims=True) + EPS)
    q_all = _dot(wuq_ref[...], cq_n.astype(BF16))
    q_scale = (MLA_QK ** -0.5) * LOG2E
    zeros_pad = jnp.zeros((MLA_QK_PAD - MLA_QK, t), F32)
    for hd in range(MLA_HEADS):
        nope = q_all[hd * MLA_NOPE:(hd + 1) * MLA_NOPE]
        r0 = MLA_HEADS * MLA_NOPE + hd * MLA_ROPE
        r1 = MLA_HEADS * (MLA_NOPE + MLA_ROPE) + hd * MLA_ROPE
        rope = q_all[r0:r0 + MLA_ROPE] * cos + q_all[r1:r1 + MLA_ROPE] * sin
        ss = jnp.sum(nope * nope, axis=0, keepdims=True) + jnp.sum(rope * rope, axis=0, keepdims=True)
        r = lax.rsqrt(ss * (1.0 / MLA_QK) + EPS) * q_scale
        qm_ref[0, hd, 0:MLA_NOPE, :] = (nope * r).astype(BF16)
        qm_ref[0, hd, MLA_NOPE:MLA_QK, :] = (rope * r).astype(BF16)
        qm_ref[0, hd, MLA_QK:MLA_QK_PAD, :] = zeros_pad.astype(BF16)

    ckv_n = c_kv * lax.rsqrt(jnp.mean(c_kv * c_kv, axis=0, keepdims=True) + EPS)
    kv = _dot(wukv_ref[...], ckv_n.astype(BF16))
    k_r = k_rope * cos + k_rope_rot * sin
    k_r_ss = jnp.sum(k_r * k_r, axis=0, keepdims=True)
    gk = gk_ref[...]
    for hd in range(MLA_HEADS):
        k_nope = kv[hd * MLA_NOPE:(hd + 1) * MLA_NOPE]
        ss = jnp.sum(k_nope * k_nope, axis=0, keepdims=True) + k_r_ss
        r = lax.rsqrt(ss * (1.0 / MLA_QK) + EPS)
        k_t = jnp.concatenate([k_nope * r, k_r * r, zeros_pad], axis=0)
        km_ref[0, hd] = (k_t.T * gk).astype(BF16)
        v0 = MLA_HEADS * MLA_NOPE + hd * MLA_V
        vm_ref[0, hd] = kv[v0:v0 + MLA_V].astype(BF16)

    d_scale = (DIFF_D ** -0.5) * LOG2E
    zeros_half = jnp.zeros((DIFF_D, t), BF16)
    gdk = gdk_ref[...]
    for hd in range(DIFF_HEADS):
        k_parts = []
        for mp in range(2):
            r0 = hd * DIFF_V + mp * DIFF_D
            qv = dq[r0:r0 + DIFF_D]
            qn = qv * (lax.rsqrt(jnp.mean(qv * qv, axis=0, keepdims=True) + EPS) * d_scale)
            qd_ref[0, hd, mp, mp * DIFF_D:(mp + 1) * DIFF_D, :] = qn.astype(BF16)
            qd_ref[0, hd, mp, (1 - mp) * DIFF_D:(2 - mp) * DIFF_D, :] = zeros_half
            kv_ = dk[r0:r0 + DIFF_D]
            k_parts.append(kv_ * lax.rsqrt(jnp.mean(kv_ * kv_, axis=0, keepdims=True) + EPS))
        k_t = jnp.concatenate(k_parts, axis=0)
        kd_ref[0, hd] = (k_t.T * gdk).astype(BF16)
        vd_ref[0, hd] = dv[hd * DIFF_V:(hd + 1) * DIFF_V].astype(BF16)


def _projections(x, win_t, wuq_t, wukv_t, cos_t, sin_t, gk_row, gdk_row):
    b, s, d = x.shape
    t = PROJ_TOKENS
    n_in = win_t.shape[0]
    const = lambda shape: pl.BlockSpec(shape, lambda bi, si: (0,) * len(shape))
    out_shape = (
        jax.ShapeDtypeStruct((b, MLA_HEADS, MLA_QK_PAD, s), BF16),
        jax.ShapeDtypeStruct((b, MLA_HEADS, s, MLA_QK_PAD), BF16),
        jax.ShapeDtypeStruct((b, MLA_HEADS, MLA_V, s), BF16),
        jax.ShapeDtypeStruct((b, DIFF_HEADS, 2, DIFF_V, s), BF16),
        jax.ShapeDtypeStruct((b, DIFF_HEADS, s, DIFF_V), BF16),
        jax.ShapeDtypeStruct((b, DIFF_HEADS, DIFF_V, s), BF16),
    )
    out_specs = (
        pl.BlockSpec((1, MLA_HEADS, MLA_QK_PAD, t), lambda bi, si: (bi, 0, 0, si)),
        pl.BlockSpec((1, MLA_HEADS, t, MLA_QK_PAD), lambda bi, si: (bi, 0, si, 0)),
        pl.BlockSpec((1, MLA_HEADS, MLA_V, t), lambda bi, si: (bi, 0, 0, si)),
        pl.BlockSpec((1, DIFF_HEADS, 2, DIFF_V, t), lambda bi, si: (bi, 0, 0, 0, si)),
        pl.BlockSpec((1, DIFF_HEADS, t, DIFF_V), lambda bi, si: (bi, 0, si, 0)),
        pl.BlockSpec((1, DIFF_HEADS, DIFF_V, t), lambda bi, si: (bi, 0, 0, si)),
    )
    in_specs = [
        pl.BlockSpec((1, t, d), lambda bi, si: (bi, si, 0)),
        const((n_in, d)),
        const(wuq_t.shape),
        const(wukv_t.shape),
        pl.BlockSpec((MLA_ROPE, t), lambda bi, si: (0, si)),
        pl.BlockSpec((MLA_ROPE, t), lambda bi, si: (0, si)),
        const(gk_row.shape),
        const(gdk_row.shape),
    ]
    return pl.pallas_call(
        _proj_kernel,
        out_shape=out_shape,
        grid=(b, s // t),
        in_specs=in_specs,
        out_specs=out_specs,
        compiler_params=pltpu.CompilerParams(
            dimension_semantics=("arbitrary", "arbitrary"),
            vmem_limit_bytes=V7X_VMEM_LIMIT_BYTES),
        name="proj",
    )(x, win_t, wuq_t, wukv_t, cos_t, sin_t, gk_row, gdk_row)


def _online_softmax_step(s, vb, m, l, acc):
    m_new = jnp.maximum(m, jnp.max(s, axis=0, keepdims=True))
    alpha = jnp.exp2(m - m_new)
    p = jnp.exp2(s - m_new)
    l = alpha * l + jnp.sum(p, axis=0, keepdims=True)
    acc = alpha * acc + _dot(vb, p.astype(BF16))
    return m_new, l, acc


def _mla_attn_kernel(q_ref, k_ref, v_ref, o_ref, *, tk, n_chunks):
    q_t = q_ref[0, 0]
    tq = q_t.shape[1]

    def body(c, carry):
        m, l, acc = carry
        off = pl.multiple_of(c * tk, tk)
        kb = k_ref[0, 0, pl.ds(off, tk), :]
        vb = v_ref[0, 0, :, pl.ds(off, tk)]
        s = _dot(kb, q_t)
        return _online_softmax_step(s, vb, m, l, acc)

    init = (jnp.full((1, tq), -jnp.inf, F32), jnp.zeros((1, tq), F32),
            jnp.zeros((MLA_V, tq), F32))
    _, l, acc = lax.fori_loop(0, n_chunks, body, init)
    o = acc * (1.0 / l)
    o_ref[0] = o.T.astype(o_ref.dtype)


def _mla_attention(q_t, k, v_t):
    b, h, _, s = q_t.shape
    tq, tk = ATTN_TQ, ATTN_TK
    kern = functools.partial(_mla_attn_kernel, tk=tk, n_chunks=s // tk)
    return pl.pallas_call(
        kern,
        out_shape=jax.ShapeDtypeStruct((b, s, h * MLA_V), BF16),
        grid=(b, h, s // tq),
        in_specs=[
            pl.BlockSpec((1, 1, MLA_QK_PAD, tq), lambda bi, hi, qi: (bi, hi, 0, qi)),
            pl.BlockSpec((1, 1, s, MLA_QK_PAD), lambda bi, hi, qi: (bi, hi, 0, 0)),
            pl.BlockSpec((1, 1, MLA_V, s), lambda bi, hi, qi: (bi, hi, 0, 0)),
        ],
        out_specs=pl.BlockSpec((1, tq, MLA_V), lambda bi, hi, qi: (bi, qi, hi)),
        compiler_params=pltpu.CompilerParams(
            dimension_semantics=("arbitrary", "arbitrary", "arbitrary"),
            vmem_limit_bytes=V7X_VMEM_LIMIT_BYTES),
        name="mla_attn",
    )(q_t, k, v_t)


def _diff_attn_kernel(q_ref, k_ref, v_ref, bias_ref, lq1_ref, lk1_ref, lq2_ref, lk2_ref,
                      g_ref, o_ref, *, tk, n_chunks, n_bias, lam_init):
    q1 = q_ref[0, 0, 0]
    q2 = q_ref[0, 0, 1]
    tq = q1.shape[1]
    qi = pl.program_id(2)
    near = (n_bias - 2) // 2

    def body(c, carry):
        m1, l1, a1, m2, l2, a2 = carry
        off = pl.multiple_of(c * tk, tk)
        kb = k_ref[0, 0, pl.ds(off, tk), :]
        vb = v_ref[0, 0, :, pl.ds(off, tk)]
        d = c * (tk // tq) - qi
        bias = bias_ref[0, jnp.clip(d, -(near + 1), near + 1) + near + 1]
        s1 = _dot(kb, q1) + bias
        m1, l1, a1 = _online_softmax_step(s1, vb, m1, l1, a1)
        s2 = _dot(kb, q2) + bias
        m2, l2, a2 = _online_softmax_step(s2, vb, m2, l2, a2)
        return m1, l1, a1, m2, l2, a2

    neg = jnp.full((1, tq), -jnp.inf, F32)
    z1 = jnp.zeros((1, tq), F32)
    za = jnp.zeros((DIFF_V, tq), F32)
    _, l1, a1, _, l2, a2 = lax.fori_loop(0, n_chunks, body, (neg, z1, za, neg, z1, za))

    lam = (jnp.exp(jnp.sum(lq1_ref[...] * lk1_ref[...], axis=-1, keepdims=True))
           - jnp.exp(jnp.sum(lq2_ref[...] * lk2_ref[...], axis=-1, keepdims=True))
           + lam_init)
    o = a1 * (1.0 / l1) - lam * (a2 * (1.0 / l2))
    o = o * lax.rsqrt(jnp.mean(o * o, axis=0, keepdims=True) + EPS)
    o_ref[0] = (o.T * (g_ref[...] * (1.0 - lam_init))).astype(o_ref.dtype)


def _diff_attention(q_t, k, v_t, bias_tiles, lq1, lk1, lq2, lk2, g_row, lam_init):
    b, h, _, _, s = q_t.shape
    tq, tk = ATTN_TQ, ATTN_TK
    n_bias = bias_tiles.shape[1]
    kern = functools.partial(_diff_attn_kernel, tk=tk, n_chunks=s // tk, n_bias=n_bias,
                             lam_init=lam_init)
    vec = pl.BlockSpec((1, DIFF_D), lambda bi, hi, qi: (0, 0))
    return pl.pallas_call(
        kern,
        out_shape=jax.ShapeDtypeStruct((b, s, h * DIFF_V), BF16),
        grid=(b, h, s // tq),
        in_specs=[
            pl.BlockSpec((1, 1, 2, DIFF_V, tq), lambda bi, hi, qi: (bi, hi, 0, 0, qi)),
            pl.BlockSpec((1, 1, s, DIFF_V), lambda bi, hi, qi: (bi, hi, 0, 0)),
            pl.BlockSpec((1, 1, DIFF_V, s), lambda bi, hi, qi: (bi, hi, 0, 0)),
            pl.BlockSpec((1, n_bias, tk, tq), lambda bi, hi, qi: (hi, 0, 0, 0)),
            vec, vec, vec, vec,
            pl.BlockSpec((1, DIFF_V), lambda bi, hi, qi: (0, 0)),
        ],
        out_specs=pl.BlockSpec((1, tq, DIFF_V), lambda bi, hi, qi: (bi, qi, hi)),
        compiler_params=pltpu.CompilerParams(
            dimension_semantics=("arbitrary", "arbitrary", "arbitrary"),
            vmem_limit_bytes=V7X_VMEM_LIMIT_BYTES),
        name="diff_attn",
    )(q_t, k, v_t, bias_tiles, lq1, lk1, lq2, lk2, g_row)


def _t5_bucket(rel):
    half = N_BUCKETS // 2
    ret = jnp.where(rel > 0, half, 0)
    n = jnp.abs(rel)
    max_exact = half // 2
    large = max_exact + (jnp.log(jnp.maximum(n, 1).astype(jnp.float32) / max_exact)
                         / math.log(MAX_DISTANCE / max_exact)
                         * (half - max_exact)).astype(jnp.int32)
    large = jnp.minimum(large, half - 1)
    return ret + jnp.where(n < max_exact, n, large)


def _bias_tiles(rel_bias, tq, tk):
    far_left = pl.cdiv(tk - 1 + MAX_DISTANCE, tq)
    far_right = pl.cdiv(tq - 1 + MAX_DISTANCE, tq)
    near = max(far_left, far_right) - 1
    offs = jnp.arange(-(near + 1), near + 2, dtype=jnp.int32) * tq
    kpos = jnp.arange(tk, dtype=jnp.int32)
    qpos = jnp.arange(tq, dtype=jnp.int32)
    rel = offs[:, None, None] + kpos[None, :, None] - qpos[None, None, :]
    table = rel_bias.astype(F32) * LOG2E
    tiles = table[_t5_bucket(rel)]
    return jnp.moveaxis(tiles, -1, 0)


def _mlp_kernel(x_ref, om_ref, od_ref, wo1_ref, wo2_ref, wup_ref, wdn_ref, o_ref, h_ref):
    j = pl.program_id(1)

    @pl.when(j == 0)
    def _():
        x1 = x_ref[...] + _dot(om_ref[...], wo1_ref[...]) + _dot(od_ref[...], wo2_ref[...])
        o_ref[...] = x1
        h_ref[...] = (x1 * lax.rsqrt(jnp.mean(x1 * x1, axis=-1, keepdims=True) + EPS)).astype(BF16)

    up = _dot(h_ref[...], wup_ref[...])
    act = jnp.square(jnp.maximum(up, 0.0)).astype(BF16)
    o_ref[...] += _dot(act, wdn_ref[...])


def _out_proj_mlp(x2d, o_mla, o_diff, wo1, wo2, wup, wdn):
    n, d = x2d.shape
    t, f = MLP_TOKENS, MLP_FF
    return pl.pallas_call(
        _mlp_kernel,
        out_shape=jax.ShapeDtypeStruct((n, d), F32),
        grid=(n // t, D_FF // f),
        in_specs=[
            pl.BlockSpec((t, d), lambda i, j: (i, 0)),
            pl.BlockSpec((t, MLA_WIDTH), lambda i, j: (i, 0)),
            pl.BlockSpec((t, DIFF_WIDTH), lambda i, j: (i, 0)),
            pl.BlockSpec((MLA_WIDTH, d), lambda i, j: (0, 0)),
            pl.BlockSpec((DIFF_WIDTH, d), lambda i, j: (0, 0)),
            pl.BlockSpec((d, f), lambda i, j: (0, j)),
            pl.BlockSpec((f, d), lambda i, j: (j, 0)),
        ],
        out_specs=pl.BlockSpec((t, d), lambda i, j: (i, 0)),
        scratch_shapes=[pltpu.VMEM((t, d), BF16)],
        compiler_params=pltpu.CompilerParams(
            dimension_semantics=("arbitrary", "arbitrary"),
            vmem_limit_bytes=V7X_VMEM_LIMIT_BYTES),
        name="out_mlp",
    )(x2d, o_mla, o_diff, wo1, wo2, wup, wdn)


def _rope_tables_t(seq):
    inv = ROPE_THETA ** (-jnp.arange(0, MLA_ROPE, 2, dtype=jnp.float32) / MLA_ROPE)
    ang = jnp.arange(seq, dtype=jnp.float32)[:, None] * inv[None, :]
    ang = jnp.concatenate([ang, ang], axis=-1)
    return jnp.cos(ang).T, jnp.sin(ang).T


def kernel(x, attn_norm_w, w_in, q_a_norm_w, w_uq, kv_a_norm_w, w_ukv, mla_q_norm_w, mla_k_norm_w, diff_q_norm_w, diff_k_norm_w, lambda_q1, lambda_k1, lambda_q2, lambda_k2, diff_out_norm_w, w_out, mlp_norm_w, w_up, w_down, rel_bias):
    b, s, d = x.shape
    cos_t, sin_t = _rope_tables_t(s)
    bias_tiles = _bias_tiles(rel_bias, ATTN_TQ, ATTN_TK)
    for layer in range(DEPTH):
        lam_init = 0.8 - 0.6 * math.exp(-0.3 * layer)

        wi = w_in[layer].astype(F32) * attn_norm_w[layer].astype(F32)[:, None]
        c0, c1, c2, c3, c4 = (Q_LORA, Q_LORA + KV_LORA, Q_LORA + KV_LORA + MLA_ROPE,
                              Q_LORA + KV_LORA + MLA_ROPE + DIFF_QK_WIDTH,
                              Q_LORA + KV_LORA + MLA_ROPE + 2 * DIFF_QK_WIDTH)
        w_krope = wi[:, c1:c2]
        win_t = jnp.concatenate(
            [wi[:, :c1], w_krope, _rotate_half(w_krope), wi[:, c2:c3], wi[:, c3:c4], wi[:, c4:]],
            axis=1).T.astype(BF16)

        wq = (w_uq[layer].astype(F32) * q_a_norm_w[layer].astype(F32)[:, None]
              ).reshape(Q_LORA, MLA_HEADS, MLA_QK)
        wq_nope = wq[:, :, :MLA_NOPE].reshape(Q_LORA, -1)
        wq_rope = wq[:, :, MLA_NOPE:]
        wuq_t = jnp.concatenate(
            [wq_nope, wq_rope.reshape(Q_LORA, -1), _rotate_half(wq_rope).reshape(Q_LORA, -1)],
            axis=1).T.astype(BF16)

        wkv = (w_ukv[layer].astype(F32) * kv_a_norm_w[layer].astype(F32)[:, None]
               ).reshape(KV_LORA, MLA_HEADS, MLA_NOPE + MLA_V)
        wukv_t = jnp.concatenate(
            [wkv[:, :, :MLA_NOPE].reshape(KV_LORA, -1), wkv[:, :, MLA_NOPE:].reshape(KV_LORA, -1)],
            axis=1).T.astype(BF16)

        gk_row = jnp.concatenate(
            [mla_q_norm_w[layer].astype(F32) * mla_k_norm_w[layer].astype(F32),
             jnp.ones((MLA_QK_PAD - MLA_QK,), F32)])[None, :]
        gd = diff_q_norm_w[layer].astype(F32) * diff_k_norm_w[layer].astype(F32)
        gdk_row = jnp.concatenate([gd, gd])[None, :]

        qm_t, km, vm_t, qd_t, kd, vd_t = _projections(
            x, win_t, wuq_t, wukv_t, cos_t, sin_t, gk_row, gdk_row)

        o_mla = _mla_attention(qm_t, km, vm_t)
        o_diff = _diff_attention(
            qd_t, kd, vd_t, bias_tiles,
            lambda_q1[layer].astype(F32)[None, :], lambda_k1[layer].astype(F32)[None, :],
            lambda_q2[layer].astype(F32)[None, :], lambda_k2[layer].astype(F32)[None, :],
            diff_out_norm_w[layer].astype(F32)[None, :], lam_init)

        wo = w_out[layer].astype(BF16)
        wup = (w_up[layer].astype(F32) * mlp_norm_w[layer].astype(F32)[:, None]).astype(BF16)
        x = _out_proj_mlp(
            x.reshape(b * s, d), o_mla.reshape(b * s, MLA_WIDTH), o_diff.reshape(b * s, DIFF_WIDTH),
            wo[:MLA_WIDTH], wo[MLA_WIDTH:], wup, w_down[layer].astype(BF16)).reshape(b, s, d)
    return x
```

```python
import functools
import math

import jax
import jax.numpy as jnp
from jax import lax
from jax.experimental import pallas as pl
from jax.experimental.pallas import tpu as pltpu

D_MODEL = 1024
DEPTH = 1
MLA_HEADS = 4
MLA_NOPE = 128
MLA_ROPE = 64
MLA_V = 128
Q_LORA = D_MODEL // 4
KV_LORA = D_MODEL // 8
MLA_QK = MLA_NOPE + MLA_ROPE
MLA_QK_PAD = 256
DIFF_HEADS = 4
DIFF_D = 64
DIFF_V = 2 * DIFF_D
DIFF_QK_WIDTH = DIFF_HEADS * 2 * DIFF_D
DIFF_WIDTH = DIFF_HEADS * DIFF_V
MLA_WIDTH = MLA_HEADS * MLA_V
D_FF = 4 * D_MODEL
N_BUCKETS = 32
MAX_DISTANCE = 128
ROPE_THETA = 10000.0
EPS = 1e-6
LOG2E = math.log2(math.e)

V7X_VMEM_LIMIT_BYTES = 56 * 1024 * 1024

PROJ_TOKENS = 512
ATTN_TQ = 512
ATTN_TK = 512
MLP_TOKENS = 1024
MLP_FF = 512

F32 = jnp.float32
BF16 = jnp.bfloat16


def _dot(a, b):
    return jnp.dot(a, b, preferred_element_type=F32)


def _dot_nt(a, b):
    return lax.dot_general(a, b, (((1,), (1,)), ((), ())), preferred_element_type=F32)


def _rotate_half(x):
    x1, x2 = jnp.split(x, 2, axis=-1)
    return jnp.concatenate([-x2, x1], axis=-1)


def _proj_kernel(x_ref, win_ref, wuq_ref, wukv_ref, cos_ref, sin_ref, gk_ref, gdk_ref,
                 qm_ref, km_ref, vm_ref, qd_ref, kd_ref, vd_ref):
    x = x_ref[0]
    h = x * lax.rsqrt(jnp.mean(x * x, axis=-1, keepdims=True) + EPS)
    y = _dot_nt(win_ref[...], h.astype(BF16))
    t = y.shape[1]
    cos = cos_ref[...]
    sin = sin_ref[...]

    o0 = 0
    c_q = y[o0:o0 + Q_LORA]
    o0 += Q_LORA
    c_kv = y[o0:o0 + KV_LORA]
    o0 += KV_LORA
    k_rope = y[o0:o0 + MLA_ROPE]
    o0 += MLA_ROPE
    k_rope_rot = y[o0:o0 + MLA_ROPE]
    o0 += MLA_ROPE
    dq = y[o0:o0 + DIFF_QK_WIDTH]
    o0 += DIFF_QK_WIDTH
    dk = y[o0:o0 + DIFF_QK_WIDTH]
    o0 += DIFF_QK_WIDTH
    dv = y[o0:o0 + DIFF_WIDTH]

    cq_n = c_q * lax.rsqrt(jnp.mean(c_q * c_q, axis=0, keepdims=True) + EPS)
    q_all = _dot(wuq_ref[...], cq_n.astype(BF16))
    q_scale = (MLA_QK ** -0.5) * LOG2E
    zeros_pad = jnp.zeros((MLA_QK_PAD - MLA_QK, t), F32)
    for hd in range(MLA_HEADS):
        nope = q_all[hd * MLA_NOPE:(hd + 1) * MLA_NOPE]
        r0 = MLA_HEADS * MLA_NOPE + hd * MLA_ROPE
        r1 = MLA_HEADS * (MLA_NOPE + MLA_ROPE) + hd * MLA_ROPE
        rope = q_all[r0:r0 + MLA_ROPE] * cos + q_all[r1:r1 + MLA_ROPE] * sin
        ss = jnp.sum(nope * nope, axis=0, keepdims=True) + jnp.sum(rope * rope, axis=0, keepdims=True)
        r = lax.rsqrt(ss * (1.0 / MLA_QK) + EPS) * q_scale
        qm_ref[0, hd, 0:MLA_NOPE, :] = (nope * r).astype(BF16)
        qm_ref[0, hd, MLA_NOPE:MLA_QK, :] = (rope * r).astype(BF16)
        qm_ref[0, hd, MLA_QK:MLA_QK_PAD, :] = zeros_pad.astype(BF16)

    ckv_n = c_kv * lax.rsqrt(jnp.mean(c_kv * c_kv, axis=0, keepdims=True) + EPS)
    kv = _dot(wukv_ref[...], ckv_n.astype(BF16))
    k_r = k_rope * cos + k_rope_rot * sin
    k_r_ss = jnp.sum(k_r * k_r, axis=0, keepdims=True)
    gk = gk_ref[...]
    for hd in range(MLA_HEADS):
        k_nope = kv[hd * MLA_NOPE:(hd + 1) * MLA_NOPE]
        ss = jnp.sum(k_nope * k_nope, axis=0, keepdims=True) + k_r_ss
        r = lax.rsqrt(ss * (1.0 / MLA_QK) + EPS)
        k_t = jnp.concatenate([k_nope * r, k_r * r, zeros_pad], axis=0)
        km_ref[0, hd] = (k_t.T * gk).astype(BF16)
        v0 = MLA_HEADS * MLA_NOPE + hd * MLA_V
        vm_ref[0, hd] = kv[v0:v0 + MLA_V].astype(BF16)

    d_scale = (DIFF_D ** -0.5) * LOG2E
    zeros_half = jnp.zeros((DIFF_D, t), BF16)
    gdk = gdk_ref[...]
    for hd in range(DIFF_HEADS):
        k_parts = []
        for mp in range(2):
            r0 = hd * DIFF_V + mp * DIFF_D
            qv = dq[r0:r0 + DIFF_D]
            qn = qv * (lax.rsqrt(jnp.mean(qv * qv, axis=0, keepdims=True) + EPS) * d_scale)
            qd_ref[0, hd, mp, mp * DIFF_D:(mp + 1) * DIFF_D, :] = qn.astype(BF16)
            qd_ref[0, hd, mp, (1 - mp) * DIFF_D:(2 - mp) * DIFF_D, :] = zeros_half
            kv_ = dk[r0:r0 + DIFF_D]
            k_parts.append(kv_ * lax.rsqrt(jnp.mean(kv_ * kv_, axis=0, keepdims=True) + EPS))
        k_t = jnp.concatenate(k_parts, axis=0)
        kd_ref[0, hd] = (k_t.T * gdk).astype(BF16)
        vd_ref[0, hd] = dv[hd * DIFF_V:(hd + 1) * DIFF_V].astype(BF16)


def _projections(x, win_t, wuq_t, wukv_t, cos_t, sin_t, gk_row, gdk_row):
    b, s, d = x.shape
    t = PROJ_TOKENS
    n_in = win_t.shape[0]
    const = lambda shape: pl.BlockSpec(shape, lambda bi, si: (0,) * len(shape))
    out_shape = (
        jax.ShapeDtypeStruct((b, MLA_HEADS, MLA_QK_PAD, s), BF16),
        jax.ShapeDtypeStruct((b, MLA_HEADS, s, MLA_QK_PAD), BF16),
        jax.ShapeDtypeStruct((b, MLA_HEADS, MLA_V, s), BF16),
        jax.ShapeDtypeStruct((b, DIFF_HEADS, 2, DIFF_V, s), BF16),
        jax.ShapeDtypeStruct((b, DIFF_HEADS, s, DIFF_V), BF16),
        jax.ShapeDtypeStruct((b, DIFF_HEADS, DIFF_V, s), BF16),
    )
    out_specs = (
        pl.BlockSpec((1, MLA_HEADS, MLA_QK_PAD, t), lambda bi, si: (bi, 0, 0, si)),
        pl.BlockSpec((1, MLA_HEADS, t, MLA_QK_PAD), lambda bi, si: (bi, 0, si, 0)),
        pl.BlockSpec((1, MLA_HEADS, MLA_V, t), lambda bi, si: (bi, 0, 0, si)),
        pl.BlockSpec((1, DIFF_HEADS, 2, DIFF_V, t), lambda bi, si: (bi, 0, 0, 0, si)),
        pl.BlockSpec((1, DIFF_HEADS, t, DIFF_V), lambda bi, si: (bi, 0, si, 0)),
        pl.BlockSpec((1, DIFF_HEADS, DIFF_V, t), lambda bi, si: (bi, 0, 0, si)),
    )
    in_specs = [
        pl.BlockSpec((1, t, d), lambda bi, si: (bi, si, 0)),
        const((n_in, d)),
        const(wuq_t.shape),
        const(wukv_t.shape),
        pl.BlockSpec((MLA_ROPE, t), lambda bi, si: (0, si)),
        pl.BlockSpec((MLA_ROPE, t), lambda bi, si: (0, si)),
        const(gk_row.shape),
        const(gdk_row.shape),
    ]
    return pl.pallas_call(
        _proj_kernel,
        out_shape=out_shape,
        grid=(b, s // t),
        in_specs=in_specs,
        out_specs=out_specs,
        compiler_params=pltpu.CompilerParams(
            dimension_semantics=("arbitrary", "arbitrary"),
            vmem_limit_bytes=V7X_VMEM_LIMIT_BYTES),
        name="proj",
    )(x, win_t, wuq_t, wukv_t, cos_t, sin_t, gk_row, gdk_row)


def _softmax_pv(s, v_t):
    m = jnp.max(s, axis=0, keepdims=True)
    p = jnp.exp2(s - m)
    l = jnp.sum(p, axis=0, keepdims=True)
    return _dot(v_t, p.astype(BF16)), l


def _mla_attn_kernel(q_ref, k_ref, v_ref, o_ref):
    s = _dot(k_ref[0, 0], q_ref[0, 0])
    o, l = _softmax_pv(s, v_ref[0, 0])
    o_ref[0] = (o * (1.0 / l)).T.astype(o_ref.dtype)


def _mla_attention(q_t, k, v_t):
    b, h, _, s = q_t.shape
    tq = ATTN_TQ
    return pl.pallas_call(
        _mla_attn_kernel,
        out_shape=jax.ShapeDtypeStruct((b, s, h * MLA_V), BF16),
        grid=(b, h, s // tq),
        in_specs=[
            pl.BlockSpec((1, 1, MLA_QK_PAD, tq), lambda bi, hi, qi: (bi, hi, 0, qi)),
            pl.BlockSpec((1, 1, s, MLA_QK_PAD), lambda bi, hi, qi: (bi, hi, 0, 0)),
            pl.BlockSpec((1, 1, MLA_V, s), lambda bi, hi, qi: (bi, hi, 0, 0)),
        ],
        out_specs=pl.BlockSpec((1, tq, MLA_V), lambda bi, hi, qi: (bi, qi, hi)),
        compiler_params=pltpu.CompilerParams(
            dimension_semantics=("arbitrary", "arbitrary", "arbitrary"),
            vmem_limit_bytes=V7X_VMEM_LIMIT_BYTES),
        name="mla_attn",
    )(q_t, k, v_t)


def _diff_attn_kernel(q_ref, k_ref, v_ref, bias_ref, lq1_ref, lk1_ref, lq2_ref, lk2_ref,
                      g_ref, o_ref, *, tk, n_chunks, n_bias, lam_init):
    tq = q_ref.shape[-1]
    qi = pl.program_id(2)
    far = (n_bias - 1) // 2
    bias = jnp.concatenate(
        [bias_ref[0, jnp.clip(c * (tk // tq) - qi, -far, far) + far] for c in range(n_chunks)],
        axis=0)
    k = k_ref[0, 0]
    v_t = v_ref[0, 0]
    a1, l1 = _softmax_pv(_dot(k, q_ref[0, 0, 0]) + bias, v_t)
    a2, l2 = _softmax_pv(_dot(k, q_ref[0, 0, 1]) + bias, v_t)

    lam = (jnp.exp(jnp.sum(lq1_ref[...] * lk1_ref[...], axis=-1, keepdims=True))
           - jnp.exp(jnp.sum(lq2_ref[...] * lk2_ref[...], axis=-1, keepdims=True))
           + lam_init)
    o = a1 * (1.0 / l1) - lam * (a2 * (1.0 / l2))
    o = o * lax.rsqrt(jnp.mean(o * o, axis=0, keepdims=True) + EPS)
    o_ref[0] = (o.T * (g_ref[...] * (1.0 - lam_init))).astype(o_ref.dtype)


def _diff_attention(q_t, k, v_t, bias_tiles, lq1, lk1, lq2, lk2, g_row, lam_init):
    b, h, _, _, s = q_t.shape
    tq, tk = ATTN_TQ, ATTN_TK
    n_bias = bias_tiles.shape[1]
    kern = functools.partial(_diff_attn_kernel, tk=tk, n_chunks=s // tk, n_bias=n_bias,
                             lam_init=lam_init)
    vec = pl.BlockSpec((1, DIFF_D), lambda bi, hi, qi: (0, 0))
    return pl.pallas_call(
        kern,
        out_shape=jax.ShapeDtypeStruct((b, s, h * DIFF_V), BF16),
        grid=(b, h, s // tq),
        in_specs=[
            pl.BlockSpec((1, 1, 2, DIFF_V, tq), lambda bi, hi, qi: (bi, hi, 0, 0, qi)),
            pl.BlockSpec((1, 1, s, DIFF_V), lambda bi, hi, qi: (bi, hi, 0, 0)),
            pl.BlockSpec((1, 1, DIFF_V, s), lambda bi, hi, qi: (bi, hi, 0, 0)),
            pl.BlockSpec((1, n_bias, tk, tq), lambda bi, hi, qi: (hi, 0, 0, 0)),
            vec, vec, vec, vec,
            pl.BlockSpec((1, DIFF_V), lambda bi, hi, qi: (0, 0)),
        ],
        out_specs=pl.BlockSpec((1, tq, DIFF_V), lambda bi, hi, qi: (bi, qi, hi)),
        compiler_params=pltpu.CompilerParams(
            dimension_semantics=("arbitrary", "arbitrary", "arbitrary"),
            vmem_limit_bytes=V7X_VMEM_LIMIT_BYTES),
        name="diff_attn",
    )(q_t, k, v_t, bias_tiles, lq1, lk1, lq2, lk2, g_row)


def _t5_bucket(rel):
    half = N_BUCKETS // 2
    ret = jnp.where(rel > 0, half, 0)
    n = jnp.abs(rel)
    max_exact = half // 2
    large = max_exact + (jnp.log(jnp.maximum(n, 1).astype(jnp.float32) / max_exact)
                         / math.log(MAX_DISTANCE / max_exact)
                         * (half - max_exact)).astype(jnp.int32)
    large = jnp.minimum(large, half - 1)
    return ret + jnp.where(n < max_exact, n, large)


def _bias_tiles(rel_bias, tq, tk):
    far_left = pl.cdiv(tk - 1 + MAX_DISTANCE, tq)
    far_right = pl.cdiv(tq - 1 + MAX_DISTANCE, tq)
    near = max(far_left, far_right) - 1
    offs = jnp.arange(-(near + 1), near + 2, dtype=jnp.int32) * tq
    kpos = jnp.arange(tk, dtype=jnp.int32)
    qpos = jnp.arange(tq, dtype=jnp.int32)
    rel = offs[:, None, None] + kpos[None, :, None] - qpos[None, None, :]
    table = rel_bias.astype(F32) * LOG2E
    bucket = _t5_bucket(rel)[None]
    tiles = jnp.zeros((table.shape[1],) + rel.shape, F32)
    for bkt in range(N_BUCKETS):
        tiles = jnp.where(bucket == bkt, table[bkt][:, None, None, None], tiles)
    return tiles


def _mlp_kernel(x_ref, om_ref, od_ref, wo1_ref, wo2_ref, wup_ref, wdn_ref, o_ref, h_ref):
    j = pl.program_id(1)

    @pl.when(j == 0)
    def _():
        x1 = x_ref[...] + _dot(om_ref[...], wo1_ref[...]) + _dot(od_ref[...], wo2_ref[...])
        o_ref[...] = x1
        h_ref[...] = (x1 * lax.rsqrt(jnp.mean(x1 * x1, axis=-1, keepdims=True) + EPS)).astype(BF16)

    up = _dot(h_ref[...], wup_ref[...])
    act = jnp.square(jnp.maximum(up, 0.0)).astype(BF16)
    o_ref[...] += _dot(act, wdn_ref[...])


def _out_proj_mlp(x2d, o_mla, o_diff, wo1, wo2, wup, wdn):
    n, d = x2d.shape
    t, f = MLP_TOKENS, MLP_FF
    return pl.pallas_call(
        _mlp_kernel,
        out_shape=jax.ShapeDtypeStruct((n, d), F32),
        grid=(n // t, D_FF // f),
        in_specs=[
            pl.BlockSpec((t, d), lambda i, j: (i, 0)),
            pl.BlockSpec((t, MLA_WIDTH), lambda i, j: (i, 0)),
            pl.BlockSpec((t, DIFF_WIDTH), lambda i, j: (i, 0)),
            pl.BlockSpec((MLA_WIDTH, d), lambda i, j: (0, 0)),
            pl.BlockSpec((DIFF_WIDTH, d), lambda i, j: (0, 0)),
            pl.BlockSpec((d, f), lambda i, j: (0, j)),
            pl.BlockSpec((f, d), lambda i, j: (j, 0)),
        ],
        out_specs=pl.BlockSpec((t, d), lambda i, j: (i, 0)),
        scratch_shapes=[pltpu.VMEM((t, d), BF16)],
        compiler_params=pltpu.CompilerParams(
            dimension_semantics=("arbitrary", "arbitrary"),
            vmem_limit_bytes=V7X_VMEM_LIMIT_BYTES),
        name="out_mlp",
    )(x2d, o_mla, o_diff, wo1, wo2, wup, wdn)


def _rope_tables_t(seq):
    inv = ROPE_THETA ** (-jnp.arange(0, MLA_ROPE, 2, dtype=jnp.float32) / MLA_ROPE)
    ang = jnp.arange(seq, dtype=jnp.float32)[:, None] * inv[None, :]
    ang = jnp.concatenate([ang, ang], axis=-1)
    return jnp.cos(ang).T, jnp.sin(ang).T


def kernel(x, attn_norm_w, w_in, q_a_norm_w, w_uq, kv_a_norm_w, w_ukv, mla_q_norm_w, mla_k_norm_w, diff_q_norm_w, diff_k_norm_w, lambda_q1, lambda_k1, lambda_q2, lambda_k2, diff_out_norm_w, w_out, mlp_norm_w, w_up, w_down, rel_bias):
    b, s, d = x.shape
    cos_t, sin_t = _rope_tables_t(s)
    bias_tiles = _bias_tiles(rel_bias, ATTN_TQ, ATTN_TK)
    for layer in range(DEPTH):
        lam_init = 0.8 - 0.6 * math.exp(-0.3 * layer)

        wi = w_in[layer].astype(F32) * attn_norm_w[layer].astype(F32)[:, None]
        c0, c1, c2, c3, c4 = (Q_LORA, Q_LORA + KV_LORA, Q_LORA + KV_LORA + MLA_ROPE,
                              Q_LORA + KV_LORA + MLA_ROPE + DIFF_QK_WIDTH,
                              Q_LORA + KV_LORA + MLA_ROPE + 2 * DIFF_QK_WIDTH)
        w_krope = wi[:, c1:c2]
        win_t = jnp.concatenate(
            [wi[:, :c1], w_krope, _rotate_half(w_krope), wi[:, c2:c3], wi[:, c3:c4], wi[:, c4:]],
            axis=1).T.astype(BF16)

        wq = (w_uq[layer].astype(F32) * q_a_norm_w[layer].astype(F32)[:, None]
              ).reshape(Q_LORA, MLA_HEADS, MLA_QK)
        wq_nope = wq[:, :, :MLA_NOPE].reshape(Q_LORA, -1)
        wq_rope = wq[:, :, MLA_NOPE:]
        wuq_t = jnp.concatenate(
            [wq_nope, wq_rope.reshape(Q_LORA, -1), _rotate_half(wq_rope).reshape(Q_LORA, -1)],
            axis=1).T.astype(BF16)

        wkv = (w_ukv[layer].astype(F32) * kv_a_norm_w[layer].astype(F32)[:, None]
               ).reshape(KV_LORA, MLA_HEADS, MLA_NOPE + MLA_V)
        wukv_t = jnp.concatenate(
            [wkv[:, :, :MLA_NOPE].reshape(KV_LORA, -1), wkv[:, :, MLA_NOPE:].reshape(KV_LORA, -1)],
            axis=1).T.astype(BF16)

        gk_row = jnp.concatenate(
            [mla_q_norm_w[layer].astype(F32) * mla_k_norm_w[layer].astype(F32),
             jnp.ones((MLA_QK_PAD - MLA_QK,), F32)])[None, :]
        gd = diff_q_norm_w[layer].astype(F32) * diff_k_norm_w[layer].astype(F32)
        gdk_row = jnp.concatenate([gd, gd])[None, :]

        qm_t, km, vm_t, qd_t, kd, vd_t = _projections(
            x, win_t, wuq_t, wukv_t, cos_t, sin_t, gk_row, gdk_row)

        o_mla = _mla_attention(qm_t, km, vm_t)
        o_diff = _diff_attention(
            qd_t, kd, vd_t, bias_tiles,
            lambda_q1[layer].astype(F32)[None, :], lambda_k1[layer].astype(F32)[None, :],
            lambda_q2[layer].astype(F32)[None, :], lambda_k2[layer].astype(F32)[None, :],
            diff_out_norm_w[layer].astype(F32)[None, :], lam_init)

        wo = w_out[layer].astype(BF16)
        wup = (w_up[layer].astype(F32) * mlp_norm_w[layer].astype(F32)[:, None]).astype(BF16)
        x = _out_proj_mlp(
            x.reshape(b * s, d), o_mla.reshape(b * s, MLA_WIDTH), o_diff.reshape(b * s, DIFF_WIDTH),
            wo[:MLA_WIDTH], wo[MLA_WIDTH:], wup, w_down[layer].astype(BF16)).reshape(b, s, d)
    return x
```

```python
import functools
import math

import jax
import jax.numpy as jnp
from jax import lax
from jax.experimental import pallas as pl
from jax.experimental.pallas import tpu as pltpu

D_MODEL = 1024
DEPTH = 1
MLA_HEADS = 4
MLA_NOPE = 128
MLA_ROPE = 64
MLA_V = 128
Q_LORA = D_MODEL // 4
KV_LORA = D_MODEL // 8
MLA_QK = MLA_NOPE + MLA_ROPE
MLA_QK_PAD = 256
DIFF_HEADS = 4
DIFF_D = 64
DIFF_V = 2 * DIFF_D
DIFF_QK_WIDTH = DIFF_HEADS * 2 * DIFF_D
DIFF_WIDTH = DIFF_HEADS * DIFF_V
MLA_WIDTH = MLA_HEADS * MLA_V
D_FF = 4 * D_MODEL
N_BUCKETS = 32
MAX_DISTANCE = 128
ROPE_THETA = 10000.0
EPS = 1e-6
LOG2E = math.log2(math.e)

V7X_VMEM_LIMIT_BYTES = 56 * 1024 * 1024

PROJ_TOKENS = 512
ATTN_TQ = 512
ATTN_TK = 512
MLP_TOKENS = 1024
MLP_FF = 512

F32 = jnp.float32
BF16 = jnp.bfloat16


def _dot(a, b):
    return jnp.dot(a, b, preferred_element_type=F32)


def _dot_nt(a, b):
    return lax.dot_general(a, b, (((1,), (1,)), ((), ())), preferred_element_type=F32)


def _rotate_half(x):
    x1, x2 = jnp.split(x, 2, axis=-1)
    return jnp.concatenate([-x2, x1], axis=-1)


def _proj_kernel(x_ref, win_ref, wuq_ref, wukv_ref, cos_ref, sin_ref, gk_ref, gdk_ref,
                 qm_ref, km_ref, vm_ref, qd_ref, kd_ref, vd_ref):
    x = x_ref[0]
    h = x * lax.rsqrt(jnp.mean(x * x, axis=-1, keepdims=True) + EPS)
    y = _dot_nt(win_ref[...], h.astype(BF16))
    t = y.shape[1]
    cos = cos_ref[...]
    sin = sin_ref[...]

    o0 = 0
    c_q = y[o0:o0 + Q_LORA]
    o0 += Q_LORA
    c_kv = y[o0:o0 + KV_LORA]
    o0 += KV_LORA
    k_rope = y[o0:o0 + MLA_ROPE]
    o0 += MLA_ROPE
    k_rope_rot = y[o0:o0 + MLA_ROPE]
    o0 += MLA_ROPE
    dq = y[o0:o0 + DIFF_QK_WIDTH]
    o0 += DIFF_QK_WIDTH
    dk = y[o0:o0 + DIFF_QK_WIDTH]
    o0 += DIFF_QK_WIDTH
    dv = y[o0:o0 + DIFF_WIDTH]

    cq_n = c_q * lax.rsqrt(jnp.mean(c_q * c_q, axis=0, keepdims=True) + EPS)
    q_all = _dot(wuq_ref[...], cq_n.astype(BF16))
    q_scale = (MLA_QK ** -0.5) * LOG2E
    zeros_pad = jnp.zeros((MLA_QK_PAD - MLA_QK, t), F32)
    for hd in range(MLA_HEADS):
        nope = q_all[hd * MLA_NOPE:(hd + 1) * MLA_NOPE]
        r0 = MLA_HEADS * MLA_NOPE + hd * MLA_ROPE
        r1 = MLA_HEADS * (MLA_NOPE + MLA_ROPE) + hd * MLA_ROPE
        rope = q_all[r0:r0 + MLA_ROPE] * cos + q_all[r1:r1 + MLA_ROPE] * sin
        ss = jnp.sum(nope * nope, axis=0, keepdims=True) + jnp.sum(rope * rope, axis=0, keepdims=True)
        r = lax.rsqrt(ss * (1.0 / MLA_QK) + EPS) * q_scale
        qm_ref[0, hd, 0:MLA_NOPE, :] = (nope * r).astype(BF16)
        qm_ref[0, hd, MLA_NOPE:MLA_QK, :] = (rope * r).astype(BF16)
        qm_ref[0, hd, MLA_QK:MLA_QK_PAD, :] = zeros_pad.astype(BF16)

    ckv_n = c_kv * lax.rsqrt(jnp.mean(c_kv * c_kv, axis=0, keepdims=True) + EPS)
    kv = _dot(wukv_ref[...], ckv_n.astype(BF16))
    k_r = k_rope * cos + k_rope_rot * sin
    k_r_ss = jnp.sum(k_r * k_r, axis=0, keepdims=True)
    gk = gk_ref[...]
    for hd in range(MLA_HEADS):
        k_nope = kv[hd * MLA_NOPE:(hd + 1) * MLA_NOPE]
        ss = jnp.sum(k_nope * k_nope, axis=0, keepdims=True) + k_r_ss
        r = lax.rsqrt(ss * (1.0 / MLA_QK) + EPS)
        k_t = jnp.concatenate([k_nope * r, k_r * r, zeros_pad], axis=0)
        km_ref[0, hd] = (k_t.T * gk).astype(BF16)
        v0 = MLA_HEADS * MLA_NOPE + hd * MLA_V
        vm_ref[0, hd] = kv[v0:v0 + MLA_V].astype(BF16)

    d_scale = (DIFF_D ** -0.5) * LOG2E
    zeros_half = jnp.zeros((DIFF_D, t), BF16)
    gdk = gdk_ref[...]
    for hd in range(DIFF_HEADS):
        k_parts = []
        for mp in range(2):
            r0 = hd * DIFF_V + mp * DIFF_D
            qv = dq[r0:r0 + DIFF_D]
            qn = qv * (lax.rsqrt(jnp.mean(qv * qv, axis=0, keepdims=True) + EPS) * d_scale)
            qd_ref[0, hd, mp, mp * DIFF_D:(mp + 1) * DIFF_D, :] = qn.astype(BF16)
            qd_ref[0, hd, mp, (1 - mp) * DIFF_D:(2 - mp) * DIFF_D, :] = zeros_half
            kv_ = dk[r0:r0 + DIFF_D]
            k_parts.append(kv_ * lax.rsqrt(jnp.mean(kv_ * kv_, axis=0, keepdims=True) + EPS))
        k_t = jnp.concatenate(k_parts, axis=0)
        kd_ref[0, hd] = (k_t.T * gdk).astype(BF16)
        vd_ref[0, hd] = dv[hd * DIFF_V:(hd + 1) * DIFF_V].astype(BF16)


def _projections(x, win_t, wuq_t, wukv_t, cos_t, sin_t, gk_row, gdk_row):
    b, s, d = x.shape
    t = PROJ_TOKENS
    n_in = win_t.shape[0]
    const = lambda shape: pl.BlockSpec(shape, lambda bi, si: (0,) * len(shape))
    out_shape = (
        jax.ShapeDtypeStruct((b, MLA_HEADS, MLA_QK_PAD, s), BF16),
        jax.ShapeDtypeStruct((b, MLA_HEADS, s, MLA_QK_PAD), BF16),
        jax.ShapeDtypeStruct((b, MLA_HEADS, MLA_V, s), BF16),
        jax.ShapeDtypeStruct((b, DIFF_HEADS, 2, DIFF_V, s), BF16),
        jax.ShapeDtypeStruct((b, DIFF_HEADS, s, DIFF_V), BF16),
        jax.ShapeDtypeStruct((b, DIFF_HEADS, DIFF_V, s), BF16),
    )
    out_specs = (
        pl.BlockSpec((1, MLA_HEADS, MLA_QK_PAD, t), lambda bi, si: (bi, 0, 0, si)),
        pl.BlockSpec((1, MLA_HEADS, t, MLA_QK_PAD), lambda bi, si: (bi, 0, si, 0)),
        pl.BlockSpec((1, MLA_HEADS, MLA_V, t), lambda bi, si: (bi, 0, 0, si)),
        pl.BlockSpec((1, DIFF_HEADS, 2, DIFF_V, t), lambda bi, si: (bi, 0, 0, 0, si)),
        pl.BlockSpec((1, DIFF_HEADS, t, DIFF_V), lambda bi, si: (bi, 0, si, 0)),
        pl.BlockSpec((1, DIFF_HEADS, DIFF_V, t), lambda bi, si: (bi, 0, 0, si)),
    )
    in_specs = [
        pl.BlockSpec((1, t, d), lambda bi, si: (bi, si, 0)),
        const((n_in, d)),
        const(wuq_t.shape),
        const(wukv_t.shape),
        pl.BlockSpec((MLA_ROPE, t), lambda bi, si: (0, si)),
        pl.BlockSpec((MLA_ROPE, t), lambda bi, si: (0, si)),
        const(gk_row.shape),
        const(gdk_row.shape),
    ]
    return pl.pallas_call(
        _proj_kernel,
        out_shape=out_shape,
        grid=(b, s // t),
        in_specs=in_specs,
        out_specs=out_specs,
        compiler_params=pltpu.CompilerParams(
            dimension_semantics=("arbitrary", "arbitrary"),
            vmem_limit_bytes=V7X_VMEM_LIMIT_BYTES),
        name="proj",
    )(x, win_t, wuq_t, wukv_t, cos_t, sin_t, gk_row, gdk_row)


def _pipelined_step(t, k_ref, q_t, v_ref, bias_fn, bufs, acc_ref, l_ref, tk):
    s_a, s_b, m_a, m_b = bufs
    n_keys, tq = s_a.shape

    @pl.when(t == 0)
    def _():
        s_b[...] = jnp.zeros_like(s_b)
        m_b[...] = jnp.zeros_like(m_b)

    def run(s_w, m_w, s_r, m_r):
        m_prev = m_r[...]
        m_new = None
        l = jnp.zeros((1, tq), F32)
        acc = jnp.zeros(acc_ref.shape, F32)
        for c in range(n_keys // tk):
            rows = slice(c * tk, (c + 1) * tk)
            s = _dot(k_ref[0, 0, rows, :], q_t)
            if bias_fn is not None:
                s = s + bias_fn(c)
            s_w[rows, :] = s
            m_c = jnp.max(s, axis=0, keepdims=True)
            m_new = m_c if m_new is None else jnp.maximum(m_new, m_c)
            p = jnp.exp2(s_r[rows, :] - m_prev)
            l = l + jnp.sum(p, axis=0, keepdims=True)
            acc = acc + _dot(v_ref[0, 0, :, rows], p.astype(BF16))
        m_w[...] = m_new
        l_ref[...] = l
        acc_ref[...] = acc

    parity = lax.rem(t, 2)

    @pl.when(parity == 0)
    def _():
        run(s_a, m_a, s_b, m_b)

    @pl.when(parity == 1)
    def _():
        run(s_b, m_b, s_a, m_a)


def _attn_scratch(s, tq, v_dim):
    return [pltpu.VMEM((s, tq), F32), pltpu.VMEM((s, tq), F32),
            pltpu.VMEM((1, tq), F32), pltpu.VMEM((1, tq), F32),
            pltpu.VMEM((v_dim, tq), F32), pltpu.VMEM((1, tq), F32)]


def _mla_attn_kernel(q_ref, k_ref, v_ref, o_ref, s_a, s_b, m_a, m_b, acc_ref, l_ref):
    _pipelined_step(pl.program_id(0), k_ref, q_ref[0, 0], v_ref, None,
                    (s_a, s_b, m_a, m_b), acc_ref, l_ref, ATTN_TK)
    o_ref[0] = (acc_ref[...] * (1.0 / l_ref[...])).T.astype(o_ref.dtype)


def _mla_attention(q_t, k, v_t):
    b, h, _, s = q_t.shape
    tq = ATTN_TQ
    n_q = s // tq
    n_tiles = b * h * n_q

    def coords(t):
        return t // (h * n_q), (t // n_q) % h, t % n_q

    def scored(t):
        return coords(jnp.minimum(t, n_tiles - 1))

    def finished(t):
        return coords(jnp.maximum(t - 1, 0))

    return pl.pallas_call(
        _mla_attn_kernel,
        out_shape=jax.ShapeDtypeStruct((b, s, h * MLA_V), BF16),
        grid=(n_tiles + 1,),
        in_specs=[
            pl.BlockSpec((1, 1, MLA_QK_PAD, tq),
                         lambda t: (scored(t)[0], scored(t)[1], 0, scored(t)[2])),
            pl.BlockSpec((1, 1, s, MLA_QK_PAD), lambda t: (scored(t)[0], scored(t)[1], 0, 0)),
            pl.BlockSpec((1, 1, MLA_V, s), lambda t: (finished(t)[0], finished(t)[1], 0, 0)),
        ],
        out_specs=pl.BlockSpec((1, tq, MLA_V),
                               lambda t: (finished(t)[0], finished(t)[2], finished(t)[1])),
        scratch_shapes=_attn_scratch(s, tq, MLA_V),
        compiler_params=pltpu.CompilerParams(
            dimension_semantics=("arbitrary",),
            vmem_limit_bytes=V7X_VMEM_LIMIT_BYTES),
        name="mla_attn",
    )(q_t, k, v_t)


def _diff_attn_kernel(q_ref, k_ref, v_ref, bias_ref, lq1_ref, lk1_ref, lq2_ref, lk2_ref,
                      g_ref, o_ref, s_a, s_b, m_a, m_b, acc_ref, l_ref, o1_ref,
                      *, n_tiles, n_q, tk, n_chunks, n_bias, lam_init):
    tq = q_ref.shape[-1]
    t = pl.program_id(0)
    qi = lax.rem(jnp.minimum(t, n_tiles - 1) // 2, n_q)
    far = (n_bias - 1) // 2

    def bias_tile(c):
        return bias_ref[0, jnp.clip(c * (tk // tq) - qi, -far, far) + far]

    _pipelined_step(t, k_ref, q_ref[0, 0, 0], v_ref, bias_tile,
                    (s_a, s_b, m_a, m_b), acc_ref, l_ref, tk)

    o = acc_ref[...] * (1.0 / l_ref[...])
    finished_map = lax.rem(jnp.maximum(t - 1, 0), 2)

    @pl.when(finished_map == 0)
    def _():
        o1_ref[...] = o

    @pl.when(finished_map == 1)
    def _():
        lam = (jnp.exp(jnp.sum(lq1_ref[...] * lk1_ref[...], axis=-1, keepdims=True))
               - jnp.exp(jnp.sum(lq2_ref[...] * lk2_ref[...], axis=-1, keepdims=True))
               + lam_init)
        od = o1_ref[...] - lam * o
        od = od * lax.rsqrt(jnp.mean(od * od, axis=0, keepdims=True) + EPS)
        o_ref[0] = (od.T * (g_ref[...] * (1.0 - lam_init))).astype(o_ref.dtype)


def _diff_attention(q_t, k, v_t, bias_tiles, lq1, lk1, lq2, lk2, g_row, lam_init):
    b, h, _, _, s = q_t.shape
    tq, tk = ATTN_TQ, ATTN_TK
    n_q = s // tq
    n_tiles = b * h * n_q * 2
    n_bias = bias_tiles.shape[1]
    kern = functools.partial(_diff_attn_kernel, n_tiles=n_tiles, n_q=n_q, tk=tk, n_chunks=s // tk,
                             n_bias=n_bias, lam_init=lam_init)

    def coords(t):
        u = t // 2
        return u // (h * n_q), (u // n_q) % h, u % n_q, t % 2

    def scored(t):
        return coords(jnp.minimum(t, n_tiles - 1))

    def finished(t):
        return coords(jnp.maximum(t - 1, 0))

    vec = pl.BlockSpec((1, DIFF_D), lambda t: (0, 0))
    return pl.pallas_call(
        kern,
        out_shape=jax.ShapeDtypeStruct((b, s, h * DIFF_V), BF16),
        grid=(n_tiles + 1,),
        in_specs=[
            pl.BlockSpec((1, 1, 1, DIFF_V, tq),
                         lambda t: (scored(t)[0], scored(t)[1], scored(t)[3], 0, scored(t)[2])),
            pl.BlockSpec((1, 1, s, DIFF_V), lambda t: (scored(t)[0], scored(t)[1], 0, 0)),
            pl.BlockSpec((1, 1, DIFF_V, s), lambda t: (finished(t)[0], finished(t)[1], 0, 0)),
            pl.BlockSpec((1, n_bias, tk, tq), lambda t: (scored(t)[1], 0, 0, 0)),
            vec, vec, vec, vec,
            pl.BlockSpec((1, DIFF_V), lambda t: (0, 0)),
        ],
        out_specs=pl.BlockSpec((1, tq, DIFF_V),
                               lambda t: (finished(t)[0], finished(t)[2], finished(t)[1])),
        scratch_shapes=_attn_scratch(s, tq, DIFF_V) + [pltpu.VMEM((DIFF_V, tq), F32)],
        compiler_params=pltpu.CompilerParams(
            dimension_semantics=("arbitrary",),
            vmem_limit_bytes=V7X_VMEM_LIMIT_BYTES),
        name="diff_attn",
    )(q_t, k, v_t, bias_tiles, lq1, lk1, lq2, lk2, g_row)


def _t5_bucket(rel):
    half = N_BUCKETS // 2
    ret = jnp.where(rel > 0, half, 0)
    n = jnp.abs(rel)
    max_exact = half // 2
    large = max_exact + (jnp.log(jnp.maximum(n, 1).astype(jnp.float32) / max_exact)
                         / math.log(MAX_DISTANCE / max_exact)
                         * (half - max_exact)).astype(jnp.int32)
    large = jnp.minimum(large, half - 1)
    return ret + jnp.where(n < max_exact, n, large)


def _bias_tiles(rel_bias, tq, tk):
    far_left = pl.cdiv(tk - 1 + MAX_DISTANCE, tq)
    far_right = pl.cdiv(tq - 1 + MAX_DISTANCE, tq)
    near = max(far_left, far_right) - 1
    offs = jnp.arange(-(near + 1), near + 2, dtype=jnp.int32) * tq
    kpos = jnp.arange(tk, dtype=jnp.int32)
    qpos = jnp.arange(tq, dtype=jnp.int32)
    rel = offs[:, None, None] + kpos[None, :, None] - qpos[None, None, :]
    table = rel_bias.astype(F32) * LOG2E
    bucket = _t5_bucket(rel)[None]
    tiles = jnp.zeros((table.shape[1],) + rel.shape, F32)
    for bkt in range(N_BUCKETS):
        tiles = jnp.where(bucket == bkt, table[bkt][:, None, None, None], tiles)
    return tiles


def _mlp_kernel(x_ref, om_ref, od_ref, wo1_ref, wo2_ref, wup_ref, wdn_ref, o_ref, h_ref):
    j = pl.program_id(1)

    @pl.when(j == 0)
    def _():
        x1 = x_ref[...] + _dot(om_ref[...], wo1_ref[...]) + _dot(od_ref[...], wo2_ref[...])
        o_ref[...] = x1
        h_ref[...] = (x1 * lax.rsqrt(jnp.mean(x1 * x1, axis=-1, keepdims=True) + EPS)).astype(BF16)

    up = _dot(h_ref[...], wup_ref[...])
    act = jnp.square(jnp.maximum(up, 0.0)).astype(BF16)
    o_ref[...] += _dot(act, wdn_ref[...])


def _out_proj_mlp(x2d, o_mla, o_diff, wo1, wo2, wup, wdn):
    n, d = x2d.shape
    t, f = MLP_TOKENS, MLP_FF
    return pl.pallas_call(
        _mlp_kernel,
        out_shape=jax.ShapeDtypeStruct((n, d), F32),
        grid=(n // t, D_FF // f),
        in_specs=[
            pl.BlockSpec((t, d), lambda i, j: (i, 0)),
            pl.BlockSpec((t, MLA_WIDTH), lambda i, j: (i, 0)),
            pl.BlockSpec((t, DIFF_WIDTH), lambda i, j: (i, 0)),
            pl.BlockSpec((MLA_WIDTH, d), lambda i, j: (0, 0)),
            pl.BlockSpec((DIFF_WIDTH, d), lambda i, j: (0, 0)),
            pl.BlockSpec((d, f), lambda i, j: (0, j)),
            pl.BlockSpec((f, d), lambda i, j: (j, 0)),
        ],
        out_specs=pl.BlockSpec((t, d), lambda i, j: (i, 0)),
        scratch_shapes=[pltpu.VMEM((t, d), BF16)],
        compiler_params=pltpu.CompilerParams(
            dimension_semantics=("arbitrary", "arbitrary"),
            vmem_limit_bytes=V7X_VMEM_LIMIT_BYTES),
        name="out_mlp",
    )(x2d, o_mla, o_diff, wo1, wo2, wup, wdn)


def _rope_tables_t(seq):
    inv = ROPE_THETA ** (-jnp.arange(0, MLA_ROPE, 2, dtype=jnp.float32) / MLA_ROPE)
    ang = jnp.arange(seq, dtype=jnp.float32)[:, None] * inv[None, :]
    ang = jnp.concatenate([ang, ang], axis=-1)
    return jnp.cos(ang).T, jnp.sin(ang).T


def kernel(x, attn_norm_w, w_in, q_a_norm_w, w_uq, kv_a_norm_w, w_ukv, mla_q_norm_w, mla_k_norm_w, diff_q_norm_w, diff_k_norm_w, lambda_q1, lambda_k1, lambda_q2, lambda_k2, diff_out_norm_w, w_out, mlp_norm_w, w_up, w_down, rel_bias):
    b, s, d = x.shape
    cos_t, sin_t = _rope_tables_t(s)
    bias_tiles = _bias_tiles(rel_bias, ATTN_TQ, ATTN_TK)
    for layer in range(DEPTH):
        lam_init = 0.8 - 0.6 * math.exp(-0.3 * layer)

        wi = w_in[layer].astype(F32) * attn_norm_w[layer].astype(F32)[:, None]
        c0, c1, c2, c3, c4 = (Q_LORA, Q_LORA + KV_LORA, Q_LORA + KV_LORA + MLA_ROPE,
                              Q_LORA + KV_LORA + MLA_ROPE + DIFF_QK_WIDTH,
                              Q_LORA + KV_LORA + MLA_ROPE + 2 * DIFF_QK_WIDTH)
        w_krope = wi[:, c1:c2]
        win_t = jnp.concatenate(
            [wi[:, :c1], w_krope, _rotate_half(w_krope), wi[:, c2:c3], wi[:, c3:c4], wi[:, c4:]],
            axis=1).T.astype(BF16)

        wq = (w_uq[layer].astype(F32) * q_a_norm_w[layer].astype(F32)[:, None]
              ).reshape(Q_LORA, MLA_HEADS, MLA_QK)
        wq_nope = wq[:, :, :MLA_NOPE].reshape(Q_LORA, -1)
        wq_rope = wq[:, :, MLA_NOPE:]
        wuq_t = jnp.concatenate(
            [wq_nope, wq_rope.reshape(Q_LORA, -1), _rotate_half(wq_rope).reshape(Q_LORA, -1)],
            axis=1).T.astype(BF16)

        wkv = (w_ukv[layer].astype(F32) * kv_a_norm_w[layer].astype(F32)[:, None]
               ).reshape(KV_LORA, MLA_HEADS, MLA_NOPE + MLA_V)
        wukv_t = jnp.concatenate(
            [wkv[:, :, :MLA_NOPE].reshape(KV_LORA, -1), wkv[:, :, MLA_NOPE:].reshape(KV_LORA, -1)],
            axis=1).T.astype(BF16)

        gk_row = jnp.concatenate(
            [mla_q_norm_w[layer].astype(F32) * mla_k_norm_w[layer].astype(F32),
             jnp.ones((MLA_QK_PAD - MLA_QK,), F32)])[None, :]
        gd = diff_q_norm_w[layer].astype(F32) * diff_k_norm_w[layer].astype(F32)
        gdk_row = jnp.concatenate([gd, gd])[None, :]

        qm_t, km, vm_t, qd_t, kd, vd_t = _projections(
            x, win_t, wuq_t, wukv_t, cos_t, sin_t, gk_row, gdk_row)

        o_mla = _mla_attention(qm_t, km, vm_t)
        o_diff = _diff_attention(
            qd_t, kd, vd_t, bias_tiles,
            lambda_q1[layer].astype(F32)[None, :], lambda_k1[layer].astype(F32)[None, :],
            lambda_q2[layer].astype(F32)[None, :], lambda_k2[layer].astype(F32)[None, :],
            diff_out_norm_w[layer].astype(F32)[None, :], lam_init)

        wo = w_out[layer].astype(BF16)
        wup = (w_up[layer].astype(F32) * mlp_norm_w[layer].astype(F32)[:, None]).astype(BF16)
        x = _out_proj_mlp(
            x.reshape(b * s, d), o_mla.reshape(b * s, MLA_WIDTH), o_diff.reshape(b * s, DIFF_WIDTH),
            wo[:MLA_WIDTH], wo[MLA_WIDTH:], wup, w_down[layer].astype(BF16)).reshape(b, s, d)
    return x
```

```python
import functools
import math

import jax
import jax.numpy as jnp
from jax import lax
from jax.experimental import pallas as pl
from jax.experimental.pallas import tpu as pltpu

D_MODEL = 1024
DEPTH = 1
MLA_HEADS = 4
MLA_NOPE = 128
MLA_ROPE = 64
MLA_V = 128
Q_LORA = D_MODEL // 4
KV_LORA = D_MODEL // 8
MLA_QK = MLA_NOPE + MLA_ROPE
MLA_QK_PAD = 256
DIFF_HEADS = 4
DIFF_D = 64
DIFF_V = 2 * DIFF_D
DIFF_QK_WIDTH = DIFF_HEADS * 2 * DIFF_D
DIFF_WIDTH = DIFF_HEADS * DIFF_V
MLA_WIDTH = MLA_HEADS * MLA_V
D_FF = 4 * D_MODEL
N_BUCKETS = 32
MAX_DISTANCE = 128
ROPE_THETA = 10000.0
EPS = 1e-6
LOG2E = math.log2(math.e)

V7X_VMEM_LIMIT_BYTES = 56 * 1024 * 1024

PROJ_TOKENS = 512
ATTN_TQ = 512
ATTN_TK = ATTN_TQ
MLP_TOKENS = 1024
MLP_FF = 512

F32 = jnp.float32
BF16 = jnp.bfloat16


def _dot(a, b):
    return jnp.dot(a, b, preferred_element_type=F32)


def _dot_nt(a, b):
    return lax.dot_general(a, b, (((1,), (1,)), ((), ())), preferred_element_type=F32)


def _rotate_half(x):
    x1, x2 = jnp.split(x, 2, axis=-1)
    return jnp.concatenate([-x2, x1], axis=-1)


def _proj_kernel(x_ref, win_ref, wuq_ref, wukv_ref, cos_ref, sin_ref, gk_ref, gdk_ref,
                 qm_ref, km_ref, vm_ref, qd_ref, kd_ref, vd_ref):
    x = x_ref[0]
    h = x * lax.rsqrt(jnp.mean(x * x, axis=-1, keepdims=True) + EPS)
    y = _dot_nt(win_ref[...], h.astype(BF16))
    t = y.shape[1]
    cos = cos_ref[...]
    sin = sin_ref[...]

    o0 = 0
    c_q = y[o0:o0 + Q_LORA]
    o0 += Q_LORA
    c_kv = y[o0:o0 + KV_LORA]
    o0 += KV_LORA
    k_rope = y[o0:o0 + MLA_ROPE]
    o0 += MLA_ROPE
    k_rope_rot = y[o0:o0 + MLA_ROPE]
    o0 += MLA_ROPE
    dq = y[o0:o0 + DIFF_QK_WIDTH]
    o0 += DIFF_QK_WIDTH
    dk = y[o0:o0 + DIFF_QK_WIDTH]
    o0 += DIFF_QK_WIDTH
    dv = y[o0:o0 + DIFF_WIDTH]

    cq_n = c_q * lax.rsqrt(jnp.mean(c_q * c_q, axis=0, keepdims=True) + EPS)
    q_all = _dot(wuq_ref[...], cq_n.astype(BF16))
    q_scale = (MLA_QK ** -0.5) * LOG2E
    zeros_pad = jnp.zeros((MLA_QK_PAD - MLA_QK, t), F32)
    for hd in range(MLA_HEADS):
        nope = q_all[hd * MLA_NOPE:(hd + 1) * MLA_NOPE]
        r0 = MLA_HEADS * MLA_NOPE + hd * MLA_ROPE
        r1 = MLA_HEADS * (MLA_NOPE + MLA_ROPE) + hd * MLA_ROPE
        rope = q_all[r0:r0 + MLA_ROPE] * cos + q_all[r1:r1 + MLA_ROPE] * sin
        ss = jnp.sum(nope * nope, axis=0, keepdims=True) + jnp.sum(rope * rope, axis=0, keepdims=True)
        r = lax.rsqrt(ss * (1.0 / MLA_QK) + EPS) * q_scale
        qm_ref[0, hd, 0:MLA_NOPE, :] = (nope * r).astype(BF16)
        qm_ref[0, hd, MLA_NOPE:MLA_QK, :] = (rope * r).astype(BF16)
        qm_ref[0, hd, MLA_QK:MLA_QK_PAD, :] = zeros_pad.astype(BF16)

    ckv_n = c_kv * lax.rsqrt(jnp.mean(c_kv * c_kv, axis=0, keepdims=True) + EPS)
    kv = _dot(wukv_ref[...], ckv_n.astype(BF16))
    k_r = k_rope * cos + k_rope_rot * sin
    k_r_ss = jnp.sum(k_r * k_r, axis=0, keepdims=True)
    gk = gk_ref[...]
    for hd in range(MLA_HEADS):
        k_nope = kv[hd * MLA_NOPE:(hd + 1) * MLA_NOPE]
        ss = jnp.sum(k_nope * k_nope, axis=0, keepdims=True) + k_r_ss
        r = lax.rsqrt(ss * (1.0 / MLA_QK) + EPS)
        k_t = jnp.concatenate([k_nope * r, k_r * r, zeros_pad], axis=0)
        km_ref[0, hd] = (k_t.T * gk).astype(BF16)
        v0 = MLA_HEADS * MLA_NOPE + hd * MLA_V
        vm_ref[0, hd] = kv[v0:v0 + MLA_V].astype(BF16)

    d_scale = (DIFF_D ** -0.5) * LOG2E
    zeros_half = jnp.zeros((DIFF_D, t), BF16)
    gdk = gdk_ref[...]
    for hd in range(DIFF_HEADS):
        k_parts = []
        for mp in range(2):
            r0 = hd * DIFF_V + mp * DIFF_D
            qv = dq[r0:r0 + DIFF_D]
            qn = qv * (lax.rsqrt(jnp.mean(qv * qv, axis=0, keepdims=True) + EPS) * d_scale)
            qd_ref[0, hd, mp, mp * DIFF_D:(mp + 1) * DIFF_D, :] = qn.astype(BF16)
            qd_ref[0, hd, mp, (1 - mp) * DIFF_D:(2 - mp) * DIFF_D, :] = zeros_half
            kv_ = dk[r0:r0 + DIFF_D]
            k_parts.append(kv_ * lax.rsqrt(jnp.mean(kv_ * kv_, axis=0, keepdims=True) + EPS))
        k_t = jnp.concatenate(k_parts, axis=0)
        kd_ref[0, hd] = (k_t.T * gdk).astype(BF16)
        vd_ref[0, hd] = dv[hd * DIFF_V:(hd + 1) * DIFF_V].astype(BF16)


def _projections(x, win_t, wuq_t, wukv_t, cos_t, sin_t, gk_row, gdk_row):
    b, s, d = x.shape
    t = PROJ_TOKENS
    n_in = win_t.shape[0]
    const = lambda shape: pl.BlockSpec(shape, lambda bi, si: (0,) * len(shape))
    out_shape = (
        jax.ShapeDtypeStruct((b, MLA_HEADS, MLA_QK_PAD, s), BF16),
        jax.ShapeDtypeStruct((b, MLA_HEADS, s, MLA_QK_PAD), BF16),
        jax.ShapeDtypeStruct((b, MLA_HEADS, MLA_V, s), BF16),
        jax.ShapeDtypeStruct((b, DIFF_HEADS, 2, DIFF_V, s), BF16),
        jax.ShapeDtypeStruct((b, DIFF_HEADS, s, DIFF_V), BF16),
        jax.ShapeDtypeStruct((b, DIFF_HEADS, DIFF_V, s), BF16),
    )
    out_specs = (
        pl.BlockSpec((1, MLA_HEADS, MLA_QK_PAD, t), lambda bi, si: (bi, 0, 0, si)),
        pl.BlockSpec((1, MLA_HEADS, t, MLA_QK_PAD), lambda bi, si: (bi, 0, si, 0)),
        pl.BlockSpec((1, MLA_HEADS, MLA_V, t), lambda bi, si: (bi, 0, 0, si)),
        pl.BlockSpec((1, DIFF_HEADS, 2, DIFF_V, t), lambda bi, si: (bi, 0, 0, 0, si)),
        pl.BlockSpec((1, DIFF_HEADS, t, DIFF_V), lambda bi, si: (bi, 0, si, 0)),
        pl.BlockSpec((1, DIFF_HEADS, DIFF_V, t), lambda bi, si: (bi, 0, 0, si)),
    )
    in_specs = [
        pl.BlockSpec((1, t, d), lambda bi, si: (bi, si, 0)),
        const((n_in, d)),
        const(wuq_t.shape),
        const(wukv_t.shape),
        pl.BlockSpec((MLA_ROPE, t), lambda bi, si: (0, si)),
        pl.BlockSpec((MLA_ROPE, t), lambda bi, si: (0, si)),
        const(gk_row.shape),
        const(gdk_row.shape),
    ]
    return pl.pallas_call(
        _proj_kernel,
        out_shape=out_shape,
        grid=(b, s // t),
        in_specs=in_specs,
        out_specs=out_specs,
        compiler_params=pltpu.CompilerParams(
            dimension_semantics=("arbitrary", "arbitrary"),
            vmem_limit_bytes=V7X_VMEM_LIMIT_BYTES),
        name="proj",
    )(x, win_t, wuq_t, wukv_t, cos_t, sin_t, gk_row, gdk_row)


def _attn_sweeps(k_ref, q_t, v_ref, score_bufs, prev_bufs, tk, bias=None):
    s_w, m_w, c_w = score_bufs
    s_r, m_r, c_r = prev_bufs
    n_keys, tq = s_w.shape
    n_chunks = n_keys // tk
    m_prev = m_r[...]
    m_new = None
    l = jnp.zeros((1, tq), F32)
    acc = jnp.zeros((v_ref.shape[2], tq), F32)
    if bias is not None:
        bias_ref, qi = bias
        n_bias = bias_ref.shape[0]
        far = (n_bias - 1) // 2
        n_near = n_bias - 2
        first_near = jnp.clip(qi - n_near // 2, 0, n_chunks - n_near)
        c_left = bias_ref[0, 0:1, :]
        c_right = bias_ref[n_bias - 1, 0:1, :]
    for j in range(n_chunks):
        if bias is None:
            rows_w = slice(j * tk, (j + 1) * tk)
            s = _dot(k_ref[0, 0, rows_w, :], q_t)
            m_c = jnp.max(s, axis=0, keepdims=True)
        else:
            c = first_near + j if j < n_near else lax.rem(first_near + j, n_chunks)
            rows_w = pl.ds(pl.multiple_of(c * tk, tk), tk)
            s = _dot(k_ref[0, 0, rows_w, :], q_t)
            if j < n_near:
                s = s + bias_ref[jnp.clip(c - qi, -far, far) + far]
                m_c = jnp.max(s, axis=0, keepdims=True)
                c_w[c] = jnp.zeros((1, tq), F32)
            else:
                off = jnp.where(c > qi, c_right, c_left)
                m_c = jnp.max(s, axis=0, keepdims=True) + off
                c_w[c] = off
        s_w[rows_w, :] = s
        m_new = m_c if m_new is None else jnp.maximum(m_new, m_c)

        rows_r = slice(j * tk, (j + 1) * tk)
        shift = m_prev if c_r is None else m_prev - c_r[j]
        p = jnp.exp2(s_r[rows_r, :] - shift)
        l = l + jnp.sum(p, axis=0, keepdims=True)
        acc = acc + _dot(v_ref[0, 0, :, rows_r], p.astype(BF16))
    m_w[...] = m_new
    return acc, l


def _mla_attn_kernel(qp_ref, qc_ref, kp_ref, kc_ref, v_ref, o_ref, s_a, s_b, m_a, m_b):
    tq = s_a.shape[1]

    @pl.when(pl.program_id(0) == 0)
    def _():
        s_a[...] = jnp.zeros_like(s_a)
        m_a[...] = jnp.zeros_like(m_a)

    acc, l = _attn_sweeps(kp_ref, qp_ref[0, 0], v_ref, (s_b, m_b, None), (s_a, m_a, None), ATTN_TK)
    o_ref[0, 0:tq, :] = (acc * (1.0 / l)).T.astype(o_ref.dtype)
    acc, l = _attn_sweeps(kc_ref, qc_ref[0, 0], v_ref, (s_a, m_a, None), (s_b, m_b, None), ATTN_TK)
    o_ref[0, tq:2 * tq, :] = (acc * (1.0 / l)).T.astype(o_ref.dtype)


def _mla_attention(q_t, k, v_t):
    b, h, _, s = q_t.shape
    tq = ATTN_TQ
    n_qp = s // (2 * tq)
    n_pairs = b * h * n_qp

    def coords(p):
        return p // (h * n_qp), (p // n_qp) % h, p % n_qp

    def prev(g):
        return coords(jnp.maximum(g - 1, 0))

    def cur(g):
        return coords(jnp.minimum(g, n_pairs - 1))

    return pl.pallas_call(
        _mla_attn_kernel,
        out_shape=jax.ShapeDtypeStruct((b, s, h * MLA_V), BF16),
        grid=(n_pairs + 1,),
        in_specs=[
            pl.BlockSpec((1, 1, MLA_QK_PAD, tq),
                         lambda g: (prev(g)[0], prev(g)[1], 0, 2 * prev(g)[2] + 1)),
            pl.BlockSpec((1, 1, MLA_QK_PAD, tq),
                         lambda g: (cur(g)[0], cur(g)[1], 0, 2 * cur(g)[2])),
            pl.BlockSpec((1, 1, s, MLA_QK_PAD), lambda g: (prev(g)[0], prev(g)[1], 0, 0)),
            pl.BlockSpec((1, 1, s, MLA_QK_PAD), lambda g: (cur(g)[0], cur(g)[1], 0, 0)),
            pl.BlockSpec((1, 1, MLA_V, s), lambda g: (prev(g)[0], prev(g)[1], 0, 0)),
        ],
        out_specs=pl.BlockSpec((1, 2 * tq, MLA_V), lambda g: (prev(g)[0], prev(g)[2], prev(g)[1])),
        scratch_shapes=[pltpu.VMEM((s, tq), F32), pltpu.VMEM((s, tq), F32),
                        pltpu.VMEM((1, tq), F32), pltpu.VMEM((1, tq), F32)],
        compiler_params=pltpu.CompilerParams(
            dimension_semantics=("arbitrary",),
            vmem_limit_bytes=V7X_VMEM_LIMIT_BYTES),
        name="mla_attn",
    )(q_t, q_t, k, k, v_t)


def _diff_attn_kernel(qp_ref, qc_ref, kp_ref, kc_ref, v_ref, brow_ref,
                      lq1_ref, lk1_ref, lq2_ref, lk2_ref, g_ref, o_ref,
                      s_a, s_b, m_a, m_b, c_a, c_b, bias_ref, *, n_pairs, n_q, lam_init):
    g = pl.program_id(1)

    @pl.when(g == 0)
    def _():
        s_a[...] = jnp.zeros_like(s_a)
        m_a[...] = jnp.zeros_like(m_a)
        c_a[...] = jnp.zeros_like(c_a)
        n_bias, tk, tq = bias_ref.shape
        for j in range(n_bias):
            rows = jnp.broadcast_to(brow_ref[0, j], (tk, brow_ref.shape[-1]))
            bias_ref[j] = pltpu.roll(rows, 0, 1, stride=1, stride_axis=0)[:, :tq]

    qi_prev = lax.rem(jnp.maximum(g - 1, 0), n_q)
    qi_cur = lax.rem(jnp.minimum(g, n_pairs - 1), n_q)
    a1, l1 = _attn_sweeps(kp_ref, qp_ref[0, 0, 0], v_ref, (s_b, m_b, c_b), (s_a, m_a, c_a),
                          ATTN_TK, bias=(bias_ref, qi_prev))
    a2, l2 = _attn_sweeps(kc_ref, qc_ref[0, 0, 0], v_ref, (s_a, m_a, c_a), (s_b, m_b, c_b),
                          ATTN_TK, bias=(bias_ref, qi_cur))
    lam = (jnp.exp(jnp.sum(lq1_ref[...] * lk1_ref[...], axis=-1, keepdims=True))
           - jnp.exp(jnp.sum(lq2_ref[...] * lk2_ref[...], axis=-1, keepdims=True))
           + lam_init)
    o = a1 * (1.0 / l1) - lam * (a2 * (1.0 / l2))
    o = o * lax.rsqrt(jnp.mean(o * o, axis=0, keepdims=True) + EPS)
    o_ref[0] = (o.T * (g_ref[...] * (1.0 - lam_init))).astype(o_ref.dtype)


def _diff_attention(q_t, k, v_t, bias_rows, lq1, lk1, lq2, lk2, g_row, lam_init):
    b, h, _, _, s = q_t.shape
    tq = ATTN_TQ
    n_q = s // tq
    n_pairs = b * n_q
    n_bias, _, span = bias_rows.shape[1:]
    assert s // ATTN_TK >= n_bias - 2
    kern = functools.partial(_diff_attn_kernel, n_pairs=n_pairs, n_q=n_q, lam_init=lam_init)

    def prev(g):
        p = jnp.maximum(g - 1, 0)
        return p // n_q, p % n_q

    def cur(g):
        p = jnp.minimum(g, n_pairs - 1)
        return p // n_q, p % n_q

    vec = pl.BlockSpec((1, DIFF_D), lambda hi, g: (0, 0))
    return pl.pallas_call(
        kern,
        out_shape=jax.ShapeDtypeStruct((b, s, h * DIFF_V), BF16),
        grid=(h, n_pairs + 1),
        in_specs=[
            pl.BlockSpec((1, 1, 1, DIFF_V, tq), lambda hi, g: (prev(g)[0], hi, 1, 0, prev(g)[1])),
            pl.BlockSpec((1, 1, 1, DIFF_V, tq), lambda hi, g: (cur(g)[0], hi, 0, 0, cur(g)[1])),
            pl.BlockSpec((1, 1, s, DIFF_V), lambda hi, g: (prev(g)[0], hi, 0, 0)),
            pl.BlockSpec((1, 1, s, DIFF_V), lambda hi, g: (cur(g)[0], hi, 0, 0)),
            pl.BlockSpec((1, 1, DIFF_V, s), lambda hi, g: (prev(g)[0], hi, 0, 0)),
            pl.BlockSpec((1, n_bias, 1, span), lambda hi, g: (hi, 0, 0, 0)),
            vec, vec, vec, vec,
            pl.BlockSpec((1, DIFF_V), lambda hi, g: (0, 0)),
        ],
        out_specs=pl.BlockSpec((1, tq, DIFF_V), lambda hi, g: (prev(g)[0], prev(g)[1], hi)),
        scratch_shapes=[pltpu.VMEM((s, tq), F32), pltpu.VMEM((s, tq), F32),
                        pltpu.VMEM((1, tq), F32), pltpu.VMEM((1, tq), F32),
                        pltpu.VMEM((s // ATTN_TK, 1, tq), F32), pltpu.VMEM((s // ATTN_TK, 1, tq), F32),
                        pltpu.VMEM((n_bias, ATTN_TK, tq), F32)],
        compiler_params=pltpu.CompilerParams(
            dimension_semantics=("arbitrary", "arbitrary"),
            vmem_limit_bytes=V7X_VMEM_LIMIT_BYTES),
        name="diff_attn",
    )(q_t, q_t, k, k, v_t, bias_rows, lq1, lk1, lq2, lk2, g_row)


def _t5_bucket(rel):
    half = N_BUCKETS // 2
    ret = jnp.where(rel > 0, half, 0)
    n = jnp.abs(rel)
    max_exact = half // 2
    large = max_exact + (jnp.log(jnp.maximum(n, 1).astype(jnp.float32) / max_exact)
                         / math.log(MAX_DISTANCE / max_exact)
                         * (half - max_exact)).astype(jnp.int32)
    large = jnp.minimum(large, half - 1)
    return ret + jnp.where(n < max_exact, n, large)


def _bias_rows(rel_bias, t):
    far = pl.cdiv(t - 1 + MAX_DISTANCE, t)
    offs = jnp.arange(-far, far + 1, dtype=jnp.int32) * t
    span = 2 * t
    m = jnp.arange(span, dtype=jnp.int32)
    q_minus_k = jnp.where(m < t, m, m - span)
    rel = offs[:, None] - q_minus_k[None, :]
    table = rel_bias.astype(F32) * LOG2E
    bucket = _t5_bucket(rel)[None]
    rows = jnp.zeros((table.shape[1],) + rel.shape, F32)
    for bkt in range(N_BUCKETS):
        rows = jnp.where(bucket == bkt, table[bkt][:, None, None], rows)
    return rows[:, :, None, :]


def _mlp_kernel(x_ref, om_ref, od_ref, wo1_ref, wo2_ref, wup_ref, wdn_ref, o_ref, h_ref):
    j = pl.program_id(1)

    @pl.when(j == 0)
    def _():
        x1 = x_ref[...] + _dot(om_ref[...], wo1_ref[...]) + _dot(od_ref[...], wo2_ref[...])
        o_ref[...] = x1
        h_ref[...] = (x1 * lax.rsqrt(jnp.mean(x1 * x1, axis=-1, keepdims=True) + EPS)).astype(BF16)

    up = _dot(h_ref[...], wup_ref[...])
    act = jnp.square(jnp.maximum(up, 0.0)).astype(BF16)
    o_ref[...] += _dot(act, wdn_ref[...])


def _out_proj_mlp(x2d, o_mla, o_diff, wo1, wo2, wup, wdn):
    n, d = x2d.shape
    t, f = MLP_TOKENS, MLP_FF
    return pl.pallas_call(
        _mlp_kernel,
        out_shape=jax.ShapeDtypeStruct((n, d), F32),
        grid=(n // t, D_FF // f),
        in_specs=[
            pl.BlockSpec((t, d), lambda i, j: (i, 0)),
            pl.BlockSpec((t, MLA_WIDTH), lambda i, j: (i, 0)),
            pl.BlockSpec((t, DIFF_WIDTH), lambda i, j: (i, 0)),
            pl.BlockSpec((MLA_WIDTH, d), lambda i, j: (0, 0)),
            pl.BlockSpec((DIFF_WIDTH, d), lambda i, j: (0, 0)),
            pl.BlockSpec((d, f), lambda i, j: (0, j)),
            pl.BlockSpec((f, d), lambda i, j: (j, 0)),
        ],
        out_specs=pl.BlockSpec((t, d), lambda i, j: (i, 0)),
        scratch_shapes=[pltpu.VMEM((t, d), BF16)],
        compiler_params=pltpu.CompilerParams(
            dimension_semantics=("arbitrary", "arbitrary"),
            vmem_limit_bytes=V7X_VMEM_LIMIT_BYTES),
        name="out_mlp",
    )(x2d, o_mla, o_diff, wo1, wo2, wup, wdn)


def _rope_tables_t(seq):
    inv = ROPE_THETA ** (-jnp.arange(0, MLA_ROPE, 2, dtype=jnp.float32) / MLA_ROPE)
    ang = jnp.arange(seq, dtype=jnp.float32)[:, None] * inv[None, :]
    ang = jnp.concatenate([ang, ang], axis=-1)
    return jnp.cos(ang).T, jnp.sin(ang).T


def kernel(x, attn_norm_w, w_in, q_a_norm_w, w_uq, kv_a_norm_w, w_ukv, mla_q_norm_w, mla_k_norm_w, diff_q_norm_w, diff_k_norm_w, lambda_q1, lambda_k1, lambda_q2, lambda_k2, diff_out_norm_w, w_out, mlp_norm_w, w_up, w_down, rel_bias):
    b, s, d = x.shape
    cos_t, sin_t = _rope_tables_t(s)
    bias_rows = _bias_rows(rel_bias, ATTN_TQ)
    for layer in range(DEPTH):
        lam_init = 0.8 - 0.6 * math.exp(-0.3 * layer)

        wi = w_in[layer].astype(F32) * attn_norm_w[layer].astype(F32)[:, None]
        c0, c1, c2, c3, c4 = (Q_LORA, Q_LORA + KV_LORA, Q_LORA + KV_LORA + MLA_ROPE,
                              Q_LORA + KV_LORA + MLA_ROPE + DIFF_QK_WIDTH,
                              Q_LORA + KV_LORA + MLA_ROPE + 2 * DIFF_QK_WIDTH)
        w_krope = wi[:, c1:c2]
        win_t = jnp.concatenate(
            [wi[:, :c1], w_krope, _rotate_half(w_krope), wi[:, c2:c3], wi[:, c3:c4], wi[:, c4:]],
            axis=1).T.astype(BF16)

        wq = (w_uq[layer].astype(F32) * q_a_norm_w[layer].astype(F32)[:, None]
              ).reshape(Q_LORA, MLA_HEADS, MLA_QK)
        wq_nope = wq[:, :, :MLA_NOPE].reshape(Q_LORA, -1)
        wq_rope = wq[:, :, MLA_NOPE:]
        wuq_t = jnp.concatenate(
            [wq_nope, wq_rope.reshape(Q_LORA, -1), _rotate_half(wq_rope).reshape(Q_LORA, -1)],
            axis=1).T.astype(BF16)

        wkv = (w_ukv[layer].astype(F32) * kv_a_norm_w[layer].astype(F32)[:, None]
               ).reshape(KV_LORA, MLA_HEADS, MLA_NOPE + MLA_V)
        wukv_t = jnp.concatenate(
            [wkv[:, :, :MLA_NOPE].reshape(KV_LORA, -1), wkv[:, :, MLA_NOPE:].reshape(KV_LORA, -1)],
            axis=1).T.astype(BF16)

        gk_row = jnp.concatenate(
            [mla_q_norm_w[layer].astype(F32) * mla_k_norm_w[layer].astype(F32),
             jnp.ones((MLA_QK_PAD - MLA_QK,), F32)])[None, :]
        gd = diff_q_norm_w[layer].astype(F32) * diff_k_norm_w[layer].astype(F32)
        gdk_row = jnp.concatenate([gd, gd])[None, :]

        qm_t, km, vm_t, qd_t, kd, vd_t = _projections(
            x, win_t, wuq_t, wukv_t, cos_t, sin_t, gk_row, gdk_row)

        o_mla = _mla_attention(qm_t, km, vm_t)
        o_diff = _diff_attention(
            qd_t, kd, vd_t, bias_rows,
            lambda_q1[layer].astype(F32)[None, :], lambda_k1[layer].astype(F32)[None, :],
            lambda_q2[layer].astype(F32)[None, :], lambda_k2[layer].astype(F32)[None, :],
            diff_out_norm_w[layer].astype(F32)[None, :], lam_init)

        wo = w_out[layer].astype(BF16)
        wup = (w_up[layer].astype(F32) * mlp_norm_w[layer].astype(F32)[:, None]).astype(BF16)
        x = _out_proj_mlp(
            x.reshape(b * s, d), o_mla.reshape(b * s, MLA_WIDTH), o_diff.reshape(b * s, DIFF_WIDTH),
            wo[:MLA_WIDTH], wo[MLA_WIDTH:], wup, w_down[layer].astype(BF16)).reshape(b, s, d)
    return x
```

```python
import functools
import math

import jax
import jax.numpy as jnp
from jax import lax
from jax.experimental import pallas as pl
from jax.experimental.pallas import tpu as pltpu

D_MODEL = 1024
DEPTH = 1
MLA_HEADS = 4
MLA_NOPE = 128
MLA_ROPE = 64
MLA_V = 128
Q_LORA = D_MODEL // 4
KV_LORA = D_MODEL // 8
MLA_QK = MLA_NOPE + MLA_ROPE
MLA_QK_PAD = 256
DIFF_HEADS = 4
DIFF_D = 64
DIFF_V = 2 * DIFF_D
DIFF_QK_WIDTH = DIFF_HEADS * 2 * DIFF_D
DIFF_WIDTH = DIFF_HEADS * DIFF_V
MLA_WIDTH = MLA_HEADS * MLA_V
D_FF = 4 * D_MODEL
N_BUCKETS = 32
MAX_DISTANCE = 128
ROPE_THETA = 10000.0
EPS = 1e-6
LOG2E = math.log2(math.e)

V7X_VMEM_LIMIT_BYTES = 56 * 1024 * 1024

PROJ_TOKENS = 1024
PROJ_SUB_TOKENS = 256
ATTN_TQ = 512
ATTN_TK = ATTN_TQ
ATTN_GROUP = 2
MLP_TOKENS = 1024
MLP_FF = 1024

F32 = jnp.float32
BF16 = jnp.bfloat16


def _dot(a, b):
    return jnp.dot(a, b, preferred_element_type=F32)


def _dot_nt(a, b):
    return lax.dot_general(a, b, (((1,), (1,)), ((), ())), preferred_element_type=F32)


def _rotate_half(x):
    x1, x2 = jnp.split(x, 2, axis=-1)
    return jnp.concatenate([-x2, x1], axis=-1)


def _proj_kernel(x_ref, win_ref, wuq_ref, wukv_ref, cos_ref, sin_ref, gk_ref, gdk_ref,
                 qm_ref, km_ref, vm_ref, qd_ref, kd_ref, vd_ref):
    ys = []
    for i in range(x_ref.shape[1] // PROJ_SUB_TOKENS):
        x = x_ref[0, i * PROJ_SUB_TOKENS:(i + 1) * PROJ_SUB_TOKENS]
        h = x * lax.rsqrt(jnp.mean(x * x, axis=-1, keepdims=True) + EPS)
        ys.append(_dot_nt(win_ref[...], h.astype(BF16)))
    for i, y in enumerate(ys):
        _proj_heads(y, slice(i * PROJ_SUB_TOKENS, (i + 1) * PROJ_SUB_TOKENS),
                    wuq_ref, wukv_ref, cos_ref, sin_ref, gk_ref, gdk_ref,
                    qm_ref, km_ref, vm_ref, qd_ref, kd_ref, vd_ref)


def _proj_heads(y, tok, wuq_ref, wukv_ref, cos_ref, sin_ref, gk_ref, gdk_ref,
                qm_ref, km_ref, vm_ref, qd_ref, kd_ref, vd_ref):
    t = y.shape[1]
    cos = cos_ref[:, tok]
    sin = sin_ref[:, tok]

    o0 = 0
    c_q = y[o0:o0 + Q_LORA]
    o0 += Q_LORA
    c_kv = y[o0:o0 + KV_LORA]
    o0 += KV_LORA
    k_rope = y[o0:o0 + MLA_ROPE]
    o0 += MLA_ROPE
    k_rope_rot = y[o0:o0 + MLA_ROPE]
    o0 += MLA_ROPE
    dq = y[o0:o0 + DIFF_QK_WIDTH]
    o0 += DIFF_QK_WIDTH
    dk = y[o0:o0 + DIFF_QK_WIDTH]
    o0 += DIFF_QK_WIDTH
    dv = y[o0:o0 + DIFF_WIDTH]

    cq_n = c_q * lax.rsqrt(jnp.mean(c_q * c_q, axis=0, keepdims=True) + EPS)
    q_all = _dot(wuq_ref[...], cq_n.astype(BF16))
    q_scale = (MLA_QK ** -0.5) * LOG2E
    zeros_pad = jnp.zeros((MLA_QK_PAD - MLA_QK, t), F32)
    for hd in range(MLA_HEADS):
        nope = q_all[hd * MLA_NOPE:(hd + 1) * MLA_NOPE]
        r0 = MLA_HEADS * MLA_NOPE + hd * MLA_ROPE
        r1 = MLA_HEADS * (MLA_NOPE + MLA_ROPE) + hd * MLA_ROPE
        rope = q_all[r0:r0 + MLA_ROPE] * cos + q_all[r1:r1 + MLA_ROPE] * sin
        ss = jnp.sum(nope * nope, axis=0, keepdims=True) + jnp.sum(rope * rope, axis=0, keepdims=True)
        r = lax.rsqrt(ss * (1.0 / MLA_QK) + EPS) * q_scale
        qm_ref[0, hd, 0:MLA_NOPE, tok] = (nope * r).astype(BF16)
        qm_ref[0, hd, MLA_NOPE:MLA_QK, tok] = (rope * r).astype(BF16)
        qm_ref[0, hd, MLA_QK:MLA_QK_PAD, tok] = zeros_pad.astype(BF16)

    ckv_n = c_kv * lax.rsqrt(jnp.mean(c_kv * c_kv, axis=0, keepdims=True) + EPS)
    kv = _dot(wukv_ref[...], ckv_n.astype(BF16))
    k_r = k_rope * cos + k_rope_rot * sin
    k_r_ss = jnp.sum(k_r * k_r, axis=0, keepdims=True)
    gk = gk_ref[...]
    for hd in range(MLA_HEADS):
        k_nope = kv[hd * MLA_NOPE:(hd + 1) * MLA_NOPE]
        ss = jnp.sum(k_nope * k_nope, axis=0, keepdims=True) + k_r_ss
        r = lax.rsqrt(ss * (1.0 / MLA_QK) + EPS)
        k_t = jnp.concatenate([k_nope * r, k_r * r, zeros_pad], axis=0)
        km_ref[0, hd, tok, :] = (k_t.T * gk).astype(BF16)
        v0 = MLA_HEADS * MLA_NOPE + hd * MLA_V
        vm_ref[0, hd, :, tok] = kv[v0:v0 + MLA_V].astype(BF16)

    d_scale = (DIFF_D ** -0.5) * LOG2E
    zeros_half = jnp.zeros((DIFF_D, t), BF16)
    gdk = gdk_ref[...]
    for hd in range(DIFF_HEADS):
        k_parts = []
        for mp in range(2):
            r0 = hd * DIFF_V + mp * DIFF_D
            qv = dq[r0:r0 + DIFF_D]
            qn = qv * (lax.rsqrt(jnp.mean(qv * qv, axis=0, keepdims=True) + EPS) * d_scale)
            qd_ref[0, hd, mp, mp * DIFF_D:(mp + 1) * DIFF_D, tok] = qn.astype(BF16)
            qd_ref[0, hd, mp, (1 - mp) * DIFF_D:(2 - mp) * DIFF_D, tok] = zeros_half
            kv_ = dk[r0:r0 + DIFF_D]
            k_parts.append(kv_ * lax.rsqrt(jnp.mean(kv_ * kv_, axis=0, keepdims=True) + EPS))
        k_t = jnp.concatenate(k_parts, axis=0)
        kd_ref[0, hd, tok, :] = (k_t.T * gdk).astype(BF16)
        vd_ref[0, hd, :, tok] = dv[hd * DIFF_V:(hd + 1) * DIFF_V].astype(BF16)


def _projections(x, win_t, wuq_t, wukv_t, cos_t, sin_t, gk_row, gdk_row):
    b, s, d = x.shape
    t = PROJ_TOKENS
    n_in = win_t.shape[0]
    const = lambda shape: pl.BlockSpec(shape, lambda bi, si: (0,) * len(shape))
    out_shape = (
        jax.ShapeDtypeStruct((b, MLA_HEADS, MLA_QK_PAD, s), BF16),
        jax.ShapeDtypeStruct((b, MLA_HEADS, s, MLA_QK_PAD), BF16),
        jax.ShapeDtypeStruct((b, MLA_HEADS, MLA_V, s), BF16),
        jax.ShapeDtypeStruct((b, DIFF_HEADS, 2, DIFF_V, s), BF16),
        jax.ShapeDtypeStruct((b, DIFF_HEADS, s, DIFF_V), BF16),
        jax.ShapeDtypeStruct((b, DIFF_HEADS, DIFF_V, s), BF16),
    )
    out_specs = (
        pl.BlockSpec((1, MLA_HEADS, MLA_QK_PAD, t), lambda bi, si: (bi, 0, 0, si)),
        pl.BlockSpec((1, MLA_HEADS, t, MLA_QK_PAD), lambda bi, si: (bi, 0, si, 0)),
        pl.BlockSpec((1, MLA_HEADS, MLA_V, t), lambda bi, si: (bi, 0, 0, si)),
        pl.BlockSpec((1, DIFF_HEADS, 2, DIFF_V, t), lambda bi, si: (bi, 0, 0, 0, si)),
        pl.BlockSpec((1, DIFF_HEADS, t, DIFF_V), lambda bi, si: (bi, 0, si, 0)),
        pl.BlockSpec((1, DIFF_HEADS, DIFF_V, t), lambda bi, si: (bi, 0, 0, si)),
    )
    in_specs = [
        pl.BlockSpec((1, t, d), lambda bi, si: (bi, si, 0)),
        const((n_in, d)),
        const(wuq_t.shape),
        const(wukv_t.shape),
        pl.BlockSpec((MLA_ROPE, t), lambda bi, si: (0, si)),
        pl.BlockSpec((MLA_ROPE, t), lambda bi, si: (0, si)),
        const(gk_row.shape),
        const(gdk_row.shape),
    ]
    return pl.pallas_call(
        _proj_kernel,
        out_shape=out_shape,
        grid=(b, s // t),
        in_specs=in_specs,
        out_specs=out_specs,
        compiler_params=pltpu.CompilerParams(
            dimension_semantics=("arbitrary", "arbitrary"),
            vmem_limit_bytes=V7X_VMEM_LIMIT_BYTES),
        name="proj",
    )(x, win_t, wuq_t, wukv_t, cos_t, sin_t, gk_row, gdk_row)


def _attn_sweeps(k_ref, q_t, v_ref, score_bufs, prev_bufs, tk, bias=None):
    s_w, m_w, c_w = score_bufs
    s_r, m_r, c_r = prev_bufs
    n_keys, tq = s_w.shape
    n_chunks = n_keys // tk
    m_prev = m_r[...]
    m_new = None
    l = jnp.zeros((1, tq), F32)
    acc = jnp.zeros((v_ref.shape[2], tq), F32)
    if bias is not None:
        bias_ref, qi = bias
        n_bias = bias_ref.shape[0]
        far = (n_bias - 1) // 2
        n_near = n_bias - 2
        first_near = jnp.clip(qi - n_near // 2, 0, n_chunks - n_near)
        c_left = bias_ref[0, 0:1, :]
        c_right = bias_ref[n_bias - 1, 0:1, :]
    for j in range(n_chunks):
        if bias is None:
            rows_w = slice(j * tk, (j + 1) * tk)
            s = _dot(k_ref[0, 0, rows_w, :], q_t)
            m_c = jnp.max(s, axis=0, keepdims=True)
        else:
            c = first_near + j if j < n_near else lax.rem(first_near + j, n_chunks)
            rows_w = pl.ds(pl.multiple_of(c * tk, tk), tk)
            s = _dot(k_ref[0, 0, rows_w, :], q_t)
            if j < n_near:
                s = s + bias_ref[jnp.clip(c - qi, -far, far) + far]
                m_c = jnp.max(s, axis=0, keepdims=True)
                c_w[c] = jnp.zeros((1, tq), F32)
            else:
                off = jnp.where(c > qi, c_right, c_left)
                m_c = jnp.max(s, axis=0, keepdims=True) + off
                c_w[c] = off
        s_w[rows_w, :] = s
        m_new = m_c if m_new is None else jnp.maximum(m_new, m_c)

        rows_r = slice(j * tk, (j + 1) * tk)
        shift = m_prev if c_r is None else m_prev - c_r[j]
        p = jnp.exp2(s_r[rows_r, :] - shift)
        l = l + jnp.sum(p, axis=0, keepdims=True)
        acc = acc + _dot(v_ref[0, 0, :, rows_r], p.astype(BF16))
    m_w[...] = m_new
    return acc, l


def _mla_attn_kernel(qp_ref, qc_ref, kp_ref, kc_ref, v_ref, o_ref, s_a, s_b, m_a, m_b):
    tq = s_a.shape[1]
    bufs = ((s_a, m_a, None), (s_b, m_b, None))

    @pl.when(pl.program_id(0) == 0)
    def _():
        s_a[...] = jnp.zeros_like(s_a)
        m_a[...] = jnp.zeros_like(m_a)

    for i in range(ATTN_GROUP):
        if i + 1 < ATTN_GROUP:
            k_ref, q_t = kp_ref, qp_ref[0, 0, :, (i + 1) * tq:(i + 2) * tq]
        else:
            k_ref, q_t = kc_ref, qc_ref[0, 0]
        acc, l = _attn_sweeps(k_ref, q_t, v_ref, bufs[(i + 1) % 2], bufs[i % 2], ATTN_TK)
        o_ref[0, i * tq:(i + 1) * tq, :] = (acc * (1.0 / l)).T.astype(o_ref.dtype)


def _mla_attention(q_t, k, v_t):
    b, h, _, s = q_t.shape
    tq = ATTN_TQ
    n_qg = s // (ATTN_GROUP * tq)
    n_groups = b * h * n_qg

    def coords(p):
        return p // (h * n_qg), (p // n_qg) % h, p % n_qg

    def prev(g):
        return coords(jnp.maximum(g - 1, 0))

    def cur(g):
        return coords(jnp.minimum(g, n_groups - 1))

    return pl.pallas_call(
        _mla_attn_kernel,
        out_shape=jax.ShapeDtypeStruct((b, s, h * MLA_V), BF16),
        grid=(n_groups + 1,),
        in_specs=[
            pl.BlockSpec((1, 1, MLA_QK_PAD, ATTN_GROUP * tq),
                         lambda g: (prev(g)[0], prev(g)[1], 0, prev(g)[2])),
            pl.BlockSpec((1, 1, MLA_QK_PAD, tq),
                         lambda g: (cur(g)[0], cur(g)[1], 0, ATTN_GROUP * cur(g)[2])),
            pl.BlockSpec((1, 1, s, MLA_QK_PAD), lambda g: (prev(g)[0], prev(g)[1], 0, 0)),
            pl.BlockSpec((1, 1, s, MLA_QK_PAD), lambda g: (cur(g)[0], cur(g)[1], 0, 0)),
            pl.BlockSpec((1, 1, MLA_V, s), lambda g: (prev(g)[0], prev(g)[1], 0, 0)),
        ],
        out_specs=pl.BlockSpec((1, ATTN_GROUP * tq, MLA_V),
                               lambda g: (prev(g)[0], prev(g)[2], prev(g)[1])),
        scratch_shapes=[pltpu.VMEM((s, tq), F32), pltpu.VMEM((s, tq), F32),
                        pltpu.VMEM((1, tq), F32), pltpu.VMEM((1, tq), F32)],
        compiler_params=pltpu.CompilerParams(
            dimension_semantics=("arbitrary",),
            vmem_limit_bytes=V7X_VMEM_LIMIT_BYTES),
        name="mla_attn",
    )(q_t, q_t, k, k, v_t)


def _diff_attn_kernel(qp_ref, qc_ref, kp_ref, kc_ref, v_ref, brow_ref,
                      lq1_ref, lk1_ref, lq2_ref, lk2_ref, g_ref, o_ref,
                      s_a, s_b, m_a, m_b, c_a, c_b, bias_ref, *, n_groups, n_qg, lam_init):
    g = pl.program_id(1)
    tq = s_a.shape[1]
    n_qt = ATTN_GROUP // 2
    bufs = ((s_a, m_a, c_a), (s_b, m_b, c_b))

    @pl.when(g == 0)
    def _():
        s_a[...] = jnp.zeros_like(s_a)
        m_a[...] = jnp.zeros_like(m_a)
        c_a[...] = jnp.zeros_like(c_a)
        n_bias, tk, tq = bias_ref.shape
        for j in range(n_bias):
            rows = jnp.broadcast_to(brow_ref[0, j], (tk, brow_ref.shape[-1]))
            bias_ref[j] = pltpu.roll(rows, 0, 1, stride=1, stride_axis=0)[:, :tq]

    qi_prev = lax.rem(jnp.maximum(g - 1, 0), n_qg) * n_qt
    qi_cur = lax.rem(jnp.minimum(g, n_groups - 1), n_qg) * n_qt
    lam = (jnp.exp(jnp.sum(lq1_ref[...] * lk1_ref[...], axis=-1, keepdims=True))
           - jnp.exp(jnp.sum(lq2_ref[...] * lk2_ref[...], axis=-1, keepdims=True))
           + lam_init)
    o_map0 = None
    for i in range(ATTN_GROUP):
        if i + 1 < ATTN_GROUP:
            qt, mp = divmod(i + 1, 2)
            k_ref, q_t, qi = kp_ref, qp_ref[0, 0, mp, :, qt * tq:(qt + 1) * tq], qi_prev + qt
        else:
            k_ref, q_t, qi = kc_ref, qc_ref[0, 0, 0], qi_cur
        acc, l = _attn_sweeps(k_ref, q_t, v_ref, bufs[(i + 1) % 2], bufs[i % 2], ATTN_TK,
                              bias=(bias_ref, qi))
        o = acc * (1.0 / l)
        if i % 2 == 0:
            o_map0 = o
        else:
            o = o_map0 - lam * o
            o = o * lax.rsqrt(jnp.mean(o * o, axis=0, keepdims=True) + EPS)
            o_ref[0, (i // 2) * tq:(i // 2 + 1) * tq, :] = (
                o.T * (g_ref[...] * (1.0 - lam_init))).astype(o_ref.dtype)


def _diff_attention(q_t, k, v_t, bias_rows, lq1, lk1, lq2, lk2, g_row, lam_init):
    b, h, _, _, s = q_t.shape
    tq = ATTN_TQ
    n_qt = ATTN_GROUP // 2
    n_qg = s // (n_qt * tq)
    n_groups = b * n_qg
    n_bias, _, span = bias_rows.shape[1:]
    assert s // ATTN_TK >= n_bias - 2
    kern = functools.partial(_diff_attn_kernel, n_groups=n_groups, n_qg=n_qg, lam_init=lam_init)

    def prev(g):
        p = jnp.maximum(g - 1, 0)
        return p // n_qg, p % n_qg

    def cur(g):
        p = jnp.minimum(g, n_groups - 1)
        return p // n_qg, p % n_qg

    vec = pl.BlockSpec((1, DIFF_D), lambda hi, g: (0, 0))
    return pl.pallas_call(
        kern,
        out_shape=jax.ShapeDtypeStruct((b, s, h * DIFF_V), BF16),
        grid=(h, n_groups + 1),
        in_specs=[
            pl.BlockSpec((1, 1, 2, DIFF_V, n_qt * tq), lambda hi, g: (prev(g)[0], hi, 0, 0, prev(g)[1])),
            pl.BlockSpec((1, 1, 1, DIFF_V, tq), lambda hi, g: (cur(g)[0], hi, 0, 0, n_qt * cur(g)[1])),
            pl.BlockSpec((1, 1, s, DIFF_V), lambda hi, g: (prev(g)[0], hi, 0, 0)),
            pl.BlockSpec((1, 1, s, DIFF_V), lambda hi, g: (cur(g)[0], hi, 0, 0)),
            pl.BlockSpec((1, 1, DIFF_V, s), lambda hi, g: (prev(g)[0], hi, 0, 0)),
            pl.BlockSpec((1, n_bias, 1, span), lambda hi, g: (hi, 0, 0, 0)),
            vec, vec, vec, vec,
            pl.BlockSpec((1, DIFF_V), lambda hi, g: (0, 0)),
        ],
        out_specs=pl.BlockSpec((1, n_qt * tq, DIFF_V), lambda hi, g: (prev(g)[0], prev(g)[1], hi)),
        scratch_shapes=[pltpu.VMEM((s, tq), F32), pltpu.VMEM((s, tq), F32),
                        pltpu.VMEM((1, tq), F32), pltpu.VMEM((1, tq), F32),
                        pltpu.VMEM((s // ATTN_TK, 1, tq), F32), pltpu.VMEM((s // ATTN_TK, 1, tq), F32),
                        pltpu.VMEM((n_bias, ATTN_TK, tq), F32)],
        compiler_params=pltpu.CompilerParams(
            dimension_semantics=("arbitrary", "arbitrary"),
            vmem_limit_bytes=V7X_VMEM_LIMIT_BYTES),
        name="diff_attn",
    )(q_t, q_t, k, k, v_t, bias_rows, lq1, lk1, lq2, lk2, g_row)


def _t5_bucket(rel):
    half = N_BUCKETS // 2
    ret = jnp.where(rel > 0, half, 0)
    n = jnp.abs(rel)
    max_exact = half // 2
    large = max_exact + (jnp.log(jnp.maximum(n, 1).astype(jnp.float32) / max_exact)
                         / math.log(MAX_DISTANCE / max_exact)
                         * (half - max_exact)).astype(jnp.int32)
    large = jnp.minimum(large, half - 1)
    return ret + jnp.where(n < max_exact, n, large)


def _bias_rows(rel_bias, t):
    far = pl.cdiv(t - 1 + MAX_DISTANCE, t)
    offs = jnp.arange(-far, far + 1, dtype=jnp.int32) * t
    span = 2 * t
    m = jnp.arange(span, dtype=jnp.int32)
    q_minus_k = jnp.where(m < t, m, m - span)
    rel = offs[:, None] - q_minus_k[None, :]
    table = rel_bias.astype(F32) * LOG2E
    bucket = _t5_bucket(rel)[None]
    rows = jnp.zeros((table.shape[1],) + rel.shape, F32)
    for bkt in range(N_BUCKETS):
        rows = jnp.where(bucket == bkt, table[bkt][:, None, None], rows)
    return rows[:, :, None, :]


def _mlp_kernel(x_ref, om_ref, od_ref, wo1_ref, wo2_ref, wup_ref, wdn_ref, o_ref, h_ref):
    j = pl.program_id(1)

    @pl.when(j == 0)
    def _():
        x1 = x_ref[...] + _dot(om_ref[...], wo1_ref[...]) + _dot(od_ref[...], wo2_ref[...])
        o_ref[...] = x1
        h_ref[...] = (x1 * lax.rsqrt(jnp.mean(x1 * x1, axis=-1, keepdims=True) + EPS)).astype(BF16)

    up = _dot(h_ref[...], wup_ref[...])
    act = jnp.square(jnp.maximum(up, 0.0)).astype(BF16)
    o_ref[...] += _dot(act, wdn_ref[...])


def _out_proj_mlp(x2d, o_mla, o_diff, wo1, wo2, wup, wdn):
    n, d = x2d.shape
    t, f = MLP_TOKENS, MLP_FF
    return pl.pallas_call(
        _mlp_kernel,
        out_shape=jax.ShapeDtypeStruct((n, d), F32),
        grid=(n // t, D_FF // f),
        in_specs=[
            pl.BlockSpec((t, d), lambda i, j: (i, 0)),
            pl.BlockSpec((t, MLA_WIDTH), lambda i, j: (i, 0)),
            pl.BlockSpec((t, DIFF_WIDTH), lambda i, j: (i, 0)),
            pl.BlockSpec((MLA_WIDTH, d), lambda i, j: (0, 0)),
            pl.BlockSpec((DIFF_WIDTH, d), lambda i, j: (0, 0)),
            pl.BlockSpec((d, f), lambda i, j: (0, j)),
            pl.BlockSpec((f, d), lambda i, j: (j, 0)),
        ],
        out_specs=pl.BlockSpec((t, d), lambda i, j: (i, 0)),
        scratch_shapes=[pltpu.VMEM((t, d), BF16)],
        compiler_params=pltpu.CompilerParams(
            dimension_semantics=("arbitrary", "arbitrary"),
            vmem_limit_bytes=V7X_VMEM_LIMIT_BYTES),
        name="out_mlp",
    )(x2d, o_mla, o_diff, wo1, wo2, wup, wdn)


def _rope_tables_t(seq):
    inv = ROPE_THETA ** (-jnp.arange(0, MLA_ROPE, 2, dtype=jnp.float32) / MLA_ROPE)
    ang = jnp.arange(seq, dtype=jnp.float32)[:, None] * inv[None, :]
    ang = jnp.concatenate([ang, ang], axis=-1)
    return jnp.cos(ang).T, jnp.sin(ang).T


def kernel(x, attn_norm_w, w_in, q_a_norm_w, w_uq, kv_a_norm_w, w_ukv, mla_q_norm_w, mla_k_norm_w, diff_q_norm_w, diff_k_norm_w, lambda_q1, lambda_k1, lambda_q2, lambda_k2, diff_out_norm_w, w_out, mlp_norm_w, w_up, w_down, rel_bias):
    b, s, d = x.shape
    cos_t, sin_t = _rope_tables_t(s)
    bias_rows = _bias_rows(rel_bias, ATTN_TQ)
    for layer in range(DEPTH):
        lam_init = 0.8 - 0.6 * math.exp(-0.3 * layer)

        wi = w_in[layer].astype(F32) * attn_norm_w[layer].astype(F32)[:, None]
        c0, c1, c2, c3, c4 = (Q_LORA, Q_LORA + KV_LORA, Q_LORA + KV_LORA + MLA_ROPE,
                              Q_LORA + KV_LORA + MLA_ROPE + DIFF_QK_WIDTH,
                              Q_LORA + KV_LORA + MLA_ROPE + 2 * DIFF_QK_WIDTH)
        w_krope = wi[:, c1:c2]
        win_t = jnp.concatenate(
            [wi[:, :c1], w_krope, _rotate_half(w_krope), wi[:, c2:c3], wi[:, c3:c4], wi[:, c4:]],
            axis=1).T.astype(BF16)

        wq = (w_uq[layer].astype(F32) * q_a_norm_w[layer].astype(F32)[:, None]
              ).reshape(Q_LORA, MLA_HEADS, MLA_QK)
        wq_nope = wq[:, :, :MLA_NOPE].reshape(Q_LORA, -1)
        wq_rope = wq[:, :, MLA_NOPE:]
        wuq_t = jnp.concatenate(
            [wq_nope, wq_rope.reshape(Q_LORA, -1), _rotate_half(wq_rope).reshape(Q_LORA, -1)],
            axis=1).T.astype(BF16)

        wkv = (w_ukv[layer].astype(F32) * kv_a_norm_w[layer].astype(F32)[:, None]
               ).reshape(KV_LORA, MLA_HEADS, MLA_NOPE + MLA_V)
        wukv_t = jnp.concatenate(
            [wkv[:, :, :MLA_NOPE].reshape(KV_LORA, -1), wkv[:, :, MLA_NOPE:].reshape(KV_LORA, -1)],
            axis=1).T.astype(BF16)

        gk_row = jnp.concatenate(
            [mla_q_norm_w[layer].astype(F32) * mla_k_norm_w[layer].astype(F32),
             jnp.ones((MLA_QK_PAD - MLA_QK,), F32)])[None, :]
        gd = diff_q_norm_w[layer].astype(F32) * diff_k_norm_w[layer].astype(F32)
        gdk_row = jnp.concatenate([gd, gd])[None, :]

        qm_t, km, vm_t, qd_t, kd, vd_t = _projections(
            x, win_t, wuq_t, wukv_t, cos_t, sin_t, gk_row, gdk_row)

        o_mla = _mla_attention(qm_t, km, vm_t)
        o_diff = _diff_attention(
            qd_t, kd, vd_t, bias_rows,
            lambda_q1[layer].astype(F32)[None, :], lambda_k1[layer].astype(F32)[None, :],
            lambda_q2[layer].astype(F32)[None, :], lambda_k2[layer].astype(F32)[None, :],
            diff_out_norm_w[layer].astype(F32)[None, :], lam_init)

        wo = w_out[layer].astype(BF16)
        wup = (w_up[layer].astype(F32) * mlp_norm_w[layer].astype(F32)[:, None]).astype(BF16)
        x = _out_proj_mlp(
            x.reshape(b * s, d), o_mla.reshape(b * s, MLA_WIDTH), o_diff.reshape(b * s, DIFF_WIDTH),
            wo[:MLA_WIDTH], wo[MLA_WIDTH:], wup, w_down[layer].astype(BF16)).reshape(b, s, d)
    return x
```

```python
import functools
import math

import jax
import jax.numpy as jnp
from jax import lax
from jax.experimental import pallas as pl
from jax.experimental.pallas import tpu as pltpu

D_MODEL = 1024
DEPTH = 1
MLA_HEADS = 4
MLA_NOPE = 128
MLA_ROPE = 64
MLA_V = 128
Q_LORA = D_MODEL // 4
KV_LORA = D_MODEL // 8
MLA_QK = MLA_NOPE + MLA_ROPE
MLA_QK_PAD = 256
DIFF_HEADS = 4
DIFF_D = 64
DIFF_V = 2 * DIFF_D
DIFF_QK_WIDTH = DIFF_HEADS * 2 * DIFF_D
DIFF_WIDTH = DIFF_HEADS * DIFF_V
MLA_WIDTH = MLA_HEADS * MLA_V
D_FF = 4 * D_MODEL
N_BUCKETS = 32
MAX_DISTANCE = 128
ROPE_THETA = 10000.0
EPS = 1e-6
LOG2E = math.log2(math.e)

V7X_VMEM_LIMIT_BYTES = 56 * 1024 * 1024

PROJ_TOKENS = 1024
PROJ_SUB_TOKENS = 256
ATTN_TQ = 512
ATTN_TK = ATTN_TQ
ATTN_GROUP = 2
MLP_TOKENS = 1024
MLP_FF = 2048
MLP_FF_CHUNK = 1024

F32 = jnp.float32
BF16 = jnp.bfloat16


def _dot(a, b):
    return jnp.dot(a, b, preferred_element_type=F32)


def _dot_nt(a, b):
    return lax.dot_general(a, b, (((1,), (1,)), ((), ())), preferred_element_type=F32)


def _rotate_half(x):
    x1, x2 = jnp.split(x, 2, axis=-1)
    return jnp.concatenate([-x2, x1], axis=-1)


def _proj_kernel(x_ref, win_ref, wuq_ref, wukv_ref, cos_ref, sin_ref, gk_ref, gdk_ref,
                 qm_ref, km_ref, vm_ref, qd_ref, kd_ref, vd_ref):
    ys = []
    for i in range(x_ref.shape[1] // PROJ_SUB_TOKENS):
        x = x_ref[0, i * PROJ_SUB_TOKENS:(i + 1) * PROJ_SUB_TOKENS]
        h = x * lax.rsqrt(jnp.mean(x * x, axis=-1, keepdims=True) + EPS)
        ys.append(_dot_nt(win_ref[...], h.astype(BF16)))
    for i, y in enumerate(ys):
        _proj_heads(y, slice(i * PROJ_SUB_TOKENS, (i + 1) * PROJ_SUB_TOKENS),
                    wuq_ref, wukv_ref, cos_ref, sin_ref, gk_ref, gdk_ref,
                    qm_ref, km_ref, vm_ref, qd_ref, kd_ref, vd_ref)


def _proj_heads(y, tok, wuq_ref, wukv_ref, cos_ref, sin_ref, gk_ref, gdk_ref,
                qm_ref, km_ref, vm_ref, qd_ref, kd_ref, vd_ref):
    t = y.shape[1]
    cos = cos_ref[:, tok]
    sin = sin_ref[:, tok]

    o0 = 0
    c_q = y[o0:o0 + Q_LORA]
    o0 += Q_LORA
    c_kv = y[o0:o0 + KV_LORA]
    o0 += KV_LORA
    k_rope = y[o0:o0 + MLA_ROPE]
    o0 += MLA_ROPE
    k_rope_rot = y[o0:o0 + MLA_ROPE]
    o0 += MLA_ROPE
    dq = y[o0:o0 + DIFF_QK_WIDTH]
    o0 += DIFF_QK_WIDTH
    dk = y[o0:o0 + DIFF_QK_WIDTH]
    o0 += DIFF_QK_WIDTH
    dv = y[o0:o0 + DIFF_WIDTH]

    cq_n = c_q * lax.rsqrt(jnp.mean(c_q * c_q, axis=0, keepdims=True) + EPS)
    q_all = _dot(wuq_ref[...], cq_n.astype(BF16))
    q_scale = (MLA_QK ** -0.5) * LOG2E
    zeros_pad = jnp.zeros((MLA_QK_PAD - MLA_QK, t), F32)
    for hd in range(MLA_HEADS):
        nope = q_all[hd * MLA_NOPE:(hd + 1) * MLA_NOPE]
        r0 = MLA_HEADS * MLA_NOPE + hd * MLA_ROPE
        r1 = MLA_HEADS * (MLA_NOPE + MLA_ROPE) + hd * MLA_ROPE
        rope = q_all[r0:r0 + MLA_ROPE] * cos + q_all[r1:r1 + MLA_ROPE] * sin
        ss = jnp.sum(nope * nope, axis=0, keepdims=True) + jnp.sum(rope * rope, axis=0, keepdims=True)
        r = lax.rsqrt(ss * (1.0 / MLA_QK) + EPS) * q_scale
        qm_ref[0, hd, 0:MLA_NOPE, tok] = (nope * r).astype(BF16)
        qm_ref[0, hd, MLA_NOPE:MLA_QK, tok] = (rope * r).astype(BF16)
        qm_ref[0, hd, MLA_QK:MLA_QK_PAD, tok] = zeros_pad.astype(BF16)

    ckv_n = c_kv * lax.rsqrt(jnp.mean(c_kv * c_kv, axis=0, keepdims=True) + EPS)
    kv = _dot(wukv_ref[...], ckv_n.astype(BF16))
    k_r = k_rope * cos + k_rope_rot * sin
    k_r_ss = jnp.sum(k_r * k_r, axis=0, keepdims=True)
    gk = gk_ref[...]
    for hd in range(MLA_HEADS):
        k_nope = kv[hd * MLA_NOPE:(hd + 1) * MLA_NOPE]
        ss = jnp.sum(k_nope * k_nope, axis=0, keepdims=True) + k_r_ss
        r = lax.rsqrt(ss * (1.0 / MLA_QK) + EPS)
        k_t = jnp.concatenate([k_nope * r, k_r * r, zeros_pad], axis=0)
        km_ref[0, hd, tok, :] = (k_t.T * gk).astype(BF16)
        v0 = MLA_HEADS * MLA_NOPE + hd * MLA_V
        vm_ref[0, hd, :, tok] = kv[v0:v0 + MLA_V].astype(BF16)

    d_scale = (DIFF_D ** -0.5) * LOG2E
    zeros_half = jnp.zeros((DIFF_D, t), BF16)
    gdk = gdk_ref[...]
    for hd in range(DIFF_HEADS):
        k_parts = []
        for mp in range(2):
            r0 = hd * DIFF_V + mp * DIFF_D
            qv = dq[r0:r0 + DIFF_D]
            qn = qv * (lax.rsqrt(jnp.mean(qv * qv, axis=0, keepdims=True) + EPS) * d_scale)
            qd_ref[0, hd, mp, mp * DIFF_D:(mp + 1) * DIFF_D, tok] = qn.astype(BF16)
            qd_ref[0, hd, mp, (1 - mp) * DIFF_D:(2 - mp) * DIFF_D, tok] = zeros_half
            kv_ = dk[r0:r0 + DIFF_D]
            k_parts.append(kv_ * lax.rsqrt(jnp.mean(kv_ * kv_, axis=0, keepdims=True) + EPS))
        k_t = jnp.concatenate(k_parts, axis=0)
        kd_ref[0, hd, tok, :] = (k_t.T * gdk).astype(BF16)
        vd_ref[0, hd, :, tok] = dv[hd * DIFF_V:(hd + 1) * DIFF_V].astype(BF16)


def _projections(x, win_t, wuq_t, wukv_t, cos_t, sin_t, gk_row, gdk_row):
    b, s, d = x.shape
    t = PROJ_TOKENS
    n_in = win_t.shape[0]
    const = lambda shape: pl.BlockSpec(shape, lambda bi, si: (0,) * len(shape))
    out_shape = (
        jax.ShapeDtypeStruct((b, MLA_HEADS, MLA_QK_PAD, s), BF16),
        jax.ShapeDtypeStruct((b, MLA_HEADS, s, MLA_QK_PAD), BF16),
        jax.ShapeDtypeStruct((b, MLA_HEADS, MLA_V, s), BF16),
        jax.ShapeDtypeStruct((b, DIFF_HEADS, 2, DIFF_V, s), BF16),
        jax.ShapeDtypeStruct((b, DIFF_HEADS, s, DIFF_V), BF16),
        jax.ShapeDtypeStruct((b, DIFF_HEADS, DIFF_V, s), BF16),
    )
    out_specs = (
        pl.BlockSpec((1, MLA_HEADS, MLA_QK_PAD, t), lambda bi, si: (bi, 0, 0, si)),
        pl.BlockSpec((1, MLA_HEADS, t, MLA_QK_PAD), lambda bi, si: (bi, 0, si, 0)),
        pl.BlockSpec((1, MLA_HEADS, MLA_V, t), lambda bi, si: (bi, 0, 0, si)),
        pl.BlockSpec((1, DIFF_HEADS, 2, DIFF_V, t), lambda bi, si: (bi, 0, 0, 0, si)),
        pl.BlockSpec((1, DIFF_HEADS, t, DIFF_V), lambda bi, si: (bi, 0, si, 0)),
        pl.BlockSpec((1, DIFF_HEADS, DIFF_V, t), lambda bi, si: (bi, 0, 0, si)),
    )
    in_specs = [
        pl.BlockSpec((1, t, d), lambda bi, si: (bi, si, 0)),
        const((n_in, d)),
        const(wuq_t.shape),
        const(wukv_t.shape),
        pl.BlockSpec((MLA_ROPE, t), lambda bi, si: (0, si)),
        pl.BlockSpec((MLA_ROPE, t), lambda bi, si: (0, si)),
        const(gk_row.shape),
        const(gdk_row.shape),
    ]
    return pl.pallas_call(
        _proj_kernel,
        out_shape=out_shape,
        grid=(b, s // t),
        in_specs=in_specs,
        out_specs=out_specs,
        compiler_params=pltpu.CompilerParams(
            dimension_semantics=("arbitrary", "arbitrary"),
            vmem_limit_bytes=V7X_VMEM_LIMIT_BYTES),
        name="proj",
    )(x, win_t, wuq_t, wukv_t, cos_t, sin_t, gk_row, gdk_row)


def _attn_sweeps(k_ref, q_t, v_ref, score_bufs, prev_bufs, tk, bias=None):
    s_w, m_w, c_w = score_bufs
    s_r, m_r, c_r = prev_bufs
    n_keys, tq = s_w.shape
    n_chunks = n_keys // tk
    m_prev = m_r[...]
    m_new = None
    l = jnp.zeros((1, tq), F32)
    acc = jnp.zeros((v_ref.shape[2], tq), F32)
    if bias is not None:
        bias_ref, qi = bias
        n_bias = bias_ref.shape[0]
        far = (n_bias - 1) // 2
        n_near = n_bias - 2
        first_near = jnp.clip(qi - n_near // 2, 0, n_chunks - n_near)
        c_left = bias_ref[0, 0:1, :]
        c_right = bias_ref[n_bias - 1, 0:1, :]
    for j in range(n_chunks):
        if bias is None:
            rows_w = slice(j * tk, (j + 1) * tk)
            s = _dot(k_ref[0, 0, rows_w, :], q_t)
            m_c = jnp.max(s, axis=0, keepdims=True)
        else:
            c = first_near + j if j < n_near else lax.rem(first_near + j, n_chunks)
            rows_w = pl.ds(pl.multiple_of(c * tk, tk), tk)
            s = _dot(k_ref[0, 0, rows_w, :], q_t)
            if j < n_near:
                s = s + bias_ref[jnp.clip(c - qi, -far, far) + far]
                m_c = jnp.max(s, axis=0, keepdims=True)
                c_w[c] = jnp.zeros((1, tq), F32)
            else:
                off = jnp.where(c > qi, c_right, c_left)
                m_c = jnp.max(s, axis=0, keepdims=True) + off
                c_w[c] = off
        s_w[rows_w, :] = s
        m_new = m_c if m_new is None else jnp.maximum(m_new, m_c)

        rows_r = slice(j * tk, (j + 1) * tk)
        shift = m_prev if c_r is None else m_prev - c_r[j]
        p = jnp.exp2(s_r[rows_r, :] - shift)
        l = l + jnp.sum(p, axis=0, keepdims=True)
        acc = acc + _dot(v_ref[0, 0, :, rows_r], p.astype(BF16))
    m_w[...] = m_new
    return acc, l


def _mla_attn_kernel(qp_ref, qc_ref, kp_ref, kc_ref, v_ref, o_ref, s_a, s_b, m_a, m_b):
    tq = s_a.shape[1]
    bufs = ((s_a, m_a, None), (s_b, m_b, None))

    @pl.when(pl.program_id(0) == 0)
    def _():
        s_a[...] = jnp.zeros_like(s_a)
        m_a[...] = jnp.zeros_like(m_a)

    for i in range(ATTN_GROUP):
        if i + 1 < ATTN_GROUP:
            k_ref, q_t = kp_ref, qp_ref[0, 0, :, (i + 1) * tq:(i + 2) * tq]
        else:
            k_ref, q_t = kc_ref, qc_ref[0, 0]
        acc, l = _attn_sweeps(k_ref, q_t, v_ref, bufs[(i + 1) % 2], bufs[i % 2], ATTN_TK)
        o_ref[0, i * tq:(i + 1) * tq, :] = (acc * (1.0 / l)).T.astype(o_ref.dtype)


def _mla_attention(q_t, k, v_t):
    b, h, _, s = q_t.shape
    tq = ATTN_TQ
    n_qg = s // (ATTN_GROUP * tq)
    n_groups = b * h * n_qg

    def coords(p):
        return p // (h * n_qg), (p // n_qg) % h, p % n_qg

    def prev(g):
        return coords(jnp.maximum(g - 1, 0))

    def cur(g):
        return coords(jnp.minimum(g, n_groups - 1))

    return pl.pallas_call(
        _mla_attn_kernel,
        out_shape=jax.ShapeDtypeStruct((b, s, h * MLA_V), BF16),
        grid=(n_groups + 1,),
        in_specs=[
            pl.BlockSpec((1, 1, MLA_QK_PAD, ATTN_GROUP * tq),
                         lambda g: (prev(g)[0], prev(g)[1], 0, prev(g)[2])),
            pl.BlockSpec((1, 1, MLA_QK_PAD, tq),
                         lambda g: (cur(g)[0], cur(g)[1], 0, ATTN_GROUP * cur(g)[2])),
            pl.BlockSpec((1, 1, s, MLA_QK_PAD), lambda g: (prev(g)[0], prev(g)[1], 0, 0)),
            pl.BlockSpec((1, 1, s, MLA_QK_PAD), lambda g: (cur(g)[0], cur(g)[1], 0, 0)),
            pl.BlockSpec((1, 1, MLA_V, s), lambda g: (prev(g)[0], prev(g)[1], 0, 0)),
        ],
        out_specs=pl.BlockSpec((1, ATTN_GROUP * tq, MLA_V),
                               lambda g: (prev(g)[0], prev(g)[2], prev(g)[1])),
        scratch_shapes=[pltpu.VMEM((s, tq), F32), pltpu.VMEM((s, tq), F32),
                        pltpu.VMEM((1, tq), F32), pltpu.VMEM((1, tq), F32)],
        compiler_params=pltpu.CompilerParams(
            dimension_semantics=("arbitrary",),
            vmem_limit_bytes=V7X_VMEM_LIMIT_BYTES),
        name="mla_attn",
    )(q_t, q_t, k, k, v_t)


def _diff_attn_kernel(qp_ref, qc_ref, kp_ref, kc_ref, v_ref, brow_ref,
                      lq1_ref, lk1_ref, lq2_ref, lk2_ref, g_ref, o_ref,
                      s_a, s_b, m_a, m_b, c_a, c_b, bias_ref, *, n_groups, n_qg, lam_init):
    g = pl.program_id(1)
    tq = s_a.shape[1]
    n_qt = ATTN_GROUP // 2
    bufs = ((s_a, m_a, c_a), (s_b, m_b, c_b))

    @pl.when(g == 0)
    def _():
        s_a[...] = jnp.zeros_like(s_a)
        m_a[...] = jnp.zeros_like(m_a)
        c_a[...] = jnp.zeros_like(c_a)
        n_bias, tk, tq = bias_ref.shape
        for j in range(n_bias):
            rows = jnp.broadcast_to(brow_ref[0, j], (tk, brow_ref.shape[-1]))
            bias_ref[j] = pltpu.roll(rows, 0, 1, stride=1, stride_axis=0)[:, :tq]

    qi_prev = lax.rem(jnp.maximum(g - 1, 0), n_qg) * n_qt
    qi_cur = lax.rem(jnp.minimum(g, n_groups - 1), n_qg) * n_qt
    lam = (jnp.exp(jnp.sum(lq1_ref[...] * lk1_ref[...], axis=-1, keepdims=True))
           - jnp.exp(jnp.sum(lq2_ref[...] * lk2_ref[...], axis=-1, keepdims=True))
           + lam_init)
    o_map0 = None
    for i in range(ATTN_GROUP):
        if i + 1 < ATTN_GROUP:
            qt, mp = divmod(i + 1, 2)
            k_ref, q_t, qi = kp_ref, qp_ref[0, 0, mp, :, qt * tq:(qt + 1) * tq], qi_prev + qt
        else:
            k_ref, q_t, qi = kc_ref, qc_ref[0, 0, 0], qi_cur
        acc, l = _attn_sweeps(k_ref, q_t, v_ref, bufs[(i + 1) % 2], bufs[i % 2], ATTN_TK,
                              bias=(bias_ref, qi))
        o = acc * (1.0 / l)
        if i % 2 == 0:
            o_map0 = o
        else:
            o = o_map0 - lam * o
            o = o * lax.rsqrt(jnp.mean(o * o, axis=0, keepdims=True) + EPS)
            o_ref[0, (i // 2) * tq:(i // 2 + 1) * tq, :] = (
                o.T * (g_ref[...] * (1.0 - lam_init))).astype(o_ref.dtype)


def _diff_attention(q_t, k, v_t, bias_rows, lq1, lk1, lq2, lk2, g_row, lam_init):
    b, h, _, _, s = q_t.shape
    tq = ATTN_TQ
    n_qt = ATTN_GROUP // 2
    n_qg = s // (n_qt * tq)
    n_groups = b * n_qg
    n_bias, _, span = bias_rows.shape[1:]
    assert s // ATTN_TK >= n_bias - 2
    kern = functools.partial(_diff_attn_kernel, n_groups=n_groups, n_qg=n_qg, lam_init=lam_init)

    def prev(g):
        p = jnp.maximum(g - 1, 0)
        return p // n_qg, p % n_qg

    def cur(g):
        p = jnp.minimum(g, n_groups - 1)
        return p // n_qg, p % n_qg

    vec = pl.BlockSpec((1, DIFF_D), lambda hi, g: (0, 0))
    return pl.pallas_call(
        kern,
        out_shape=jax.ShapeDtypeStruct((b, s, h * DIFF_V), BF16),
        grid=(h, n_groups + 1),
        in_specs=[
            pl.BlockSpec((1, 1, 2, DIFF_V, n_qt * tq), lambda hi, g: (prev(g)[0], hi, 0, 0, prev(g)[1])),
            pl.BlockSpec((1, 1, 1, DIFF_V, tq), lambda hi, g: (cur(g)[0], hi, 0, 0, n_qt * cur(g)[1])),
            pl.BlockSpec((1, 1, s, DIFF_V), lambda hi, g: (prev(g)[0], hi, 0, 0)),
            pl.BlockSpec((1, 1, s, DIFF_V), lambda hi, g: (cur(g)[0], hi, 0, 0)),
            pl.BlockSpec((1, 1, DIFF_V, s), lambda hi, g: (prev(g)[0], hi, 0, 0)),
            pl.BlockSpec((1, n_bias, 1, span), lambda hi, g: (hi, 0, 0, 0)),
            vec, vec, vec, vec,
            pl.BlockSpec((1, DIFF_V), lambda hi, g: (0, 0)),
        ],
        out_specs=pl.BlockSpec((1, n_qt * tq, DIFF_V), lambda hi, g: (prev(g)[0], prev(g)[1], hi)),
        scratch_shapes=[pltpu.VMEM((s, tq), F32), pltpu.VMEM((s, tq), F32),
                        pltpu.VMEM((1, tq), F32), pltpu.VMEM((1, tq), F32),
                        pltpu.VMEM((s // ATTN_TK, 1, tq), F32), pltpu.VMEM((s // ATTN_TK, 1, tq), F32),
                        pltpu.VMEM((n_bias, ATTN_TK, tq), F32)],
        compiler_params=pltpu.CompilerParams(
            dimension_semantics=("arbitrary", "arbitrary"),
            vmem_limit_bytes=V7X_VMEM_LIMIT_BYTES),
        name="diff_attn",
    )(q_t, q_t, k, k, v_t, bias_rows, lq1, lk1, lq2, lk2, g_row)


def _t5_bucket(rel):
    half = N_BUCKETS // 2
    ret = jnp.where(rel > 0, half, 0)
    n = jnp.abs(rel)
    max_exact = half // 2
    large = max_exact + (jnp.log(jnp.maximum(n, 1).astype(jnp.float32) / max_exact)
                         / math.log(MAX_DISTANCE / max_exact)
                         * (half - max_exact)).astype(jnp.int32)
    large = jnp.minimum(large, half - 1)
    return ret + jnp.where(n < max_exact, n, large)


def _bias_rows(rel_bias, t):
    far = pl.cdiv(t - 1 + MAX_DISTANCE, t)
    offs = jnp.arange(-far, far + 1, dtype=jnp.int32) * t
    span = 2 * t
    m = jnp.arange(span, dtype=jnp.int32)
    q_minus_k = jnp.where(m < t, m, m - span)
    rel = offs[:, None] - q_minus_k[None, :]
    table = rel_bias.astype(F32) * LOG2E
    bucket = _t5_bucket(rel)[None]
    rows = jnp.zeros((table.shape[1],) + rel.shape, F32)
    for bkt in range(N_BUCKETS):
        rows = jnp.where(bucket == bkt, table[bkt][:, None, None], rows)
    return rows[:, :, None, :]


def _mlp_kernel(x_ref, om_ref, od_ref, wo1_ref, wo2_ref, wup_ref, wdn_ref, o_ref, h_ref):
    j = pl.program_id(1)

    @pl.when(j == 0)
    def _():
        x1 = x_ref[...] + _dot(om_ref[...], wo1_ref[...]) + _dot(od_ref[...], wo2_ref[...])
        o_ref[...] = x1
        h_ref[...] = (x1 * lax.rsqrt(jnp.mean(x1 * x1, axis=-1, keepdims=True) + EPS)).astype(BF16)

    for c in range(wup_ref.shape[1] // MLP_FF_CHUNK):
        cols = slice(c * MLP_FF_CHUNK, (c + 1) * MLP_FF_CHUNK)
        up = _dot(h_ref[...], wup_ref[:, cols])
        act = jnp.square(jnp.maximum(up, 0.0)).astype(BF16)
        o_ref[...] += _dot(act, wdn_ref[cols, :])


def _out_proj_mlp(x2d, o_mla, o_diff, wo1, wo2, wup, wdn):
    n, d = x2d.shape
    t, f = MLP_TOKENS, MLP_FF
    return pl.pallas_call(
        _mlp_kernel,
        out_shape=jax.ShapeDtypeStruct((n, d), F32),
        grid=(n // t, D_FF // f),
        in_specs=[
            pl.BlockSpec((t, d), lambda i, j: (i, 0)),
            pl.BlockSpec((t, MLA_WIDTH), lambda i, j: (i, 0)),
            pl.BlockSpec((t, DIFF_WIDTH), lambda i, j: (i, 0)),
            pl.BlockSpec((MLA_WIDTH, d), lambda i, j: (0, 0)),
            pl.BlockSpec((DIFF_WIDTH, d), lambda i, j: (0, 0)),
            pl.BlockSpec((d, f), lambda i, j: (0, j)),
            pl.BlockSpec((f, d), lambda i, j: (j, 0)),
        ],
        out_specs=pl.BlockSpec((t, d), lambda i, j: (i, 0)),
        scratch_shapes=[pltpu.VMEM((t, d), BF16)],
        compiler_params=pltpu.CompilerParams(
            dimension_semantics=("arbitrary", "arbitrary"),
            vmem_limit_bytes=V7X_VMEM_LIMIT_BYTES),
        name="out_mlp",
    )(x2d, o_mla, o_diff, wo1, wo2, wup, wdn)


def _rope_tables_t(seq):
    inv = ROPE_THETA ** (-jnp.arange(0, MLA_ROPE, 2, dtype=jnp.float32) / MLA_ROPE)
    ang = jnp.arange(seq, dtype=jnp.float32)[:, None] * inv[None, :]
    ang = jnp.concatenate([ang, ang], axis=-1)
    return jnp.cos(ang).T, jnp.sin(ang).T


def kernel(x, attn_norm_w, w_in, q_a_norm_w, w_uq, kv_a_norm_w, w_ukv, mla_q_norm_w, mla_k_norm_w, diff_q_norm_w, diff_k_norm_w, lambda_q1, lambda_k1, lambda_q2, lambda_k2, diff_out_norm_w, w_out, mlp_norm_w, w_up, w_down, rel_bias):
    b, s, d = x.shape
    cos_t, sin_t = _rope_tables_t(s)
    bias_rows = _bias_rows(rel_bias, ATTN_TQ)
    for layer in range(DEPTH):
        lam_init = 0.8 - 0.6 * math.exp(-0.3 * layer)

        wi = w_in[layer].astype(F32) * attn_norm_w[layer].astype(F32)[:, None]
        c0, c1, c2, c3, c4 = (Q_LORA, Q_LORA + KV_LORA, Q_LORA + KV_LORA + MLA_ROPE,
                              Q_LORA + KV_LORA + MLA_ROPE + DIFF_QK_WIDTH,
                              Q_LORA + KV_LORA + MLA_ROPE + 2 * DIFF_QK_WIDTH)
        w_krope = wi[:, c1:c2]
        win_t = jnp.concatenate(
            [wi[:, :c1], w_krope, _rotate_half(w_krope), wi[:, c2:c3], wi[:, c3:c4], wi[:, c4:]],
            axis=1).T.astype(BF16)

        wq = (w_uq[layer].astype(F32) * q_a_norm_w[layer].astype(F32)[:, None]
              ).reshape(Q_LORA, MLA_HEADS, MLA_QK)
        wq_nope = wq[:, :, :MLA_NOPE].reshape(Q_LORA, -1)
        wq_rope = wq[:, :, MLA_NOPE:]
        wuq_t = jnp.concatenate(
            [wq_nope, wq_rope.reshape(Q_LORA, -1), _rotate_half(wq_rope).reshape(Q_LORA, -1)],
            axis=1).T.astype(BF16)

        wkv = (w_ukv[layer].astype(F32) * kv_a_norm_w[layer].astype(F32)[:, None]
               ).reshape(KV_LORA, MLA_HEADS, MLA_NOPE + MLA_V)
        wukv_t = jnp.concatenate(
            [wkv[:, :, :MLA_NOPE].reshape(KV_LORA, -1), wkv[:, :, MLA_NOPE:].reshape(KV_LORA, -1)],
            axis=1).T.astype(BF16)

        gk_row = jnp.concatenate(
            [mla_q_norm_w[layer].astype(F32) * mla_k_norm_w[layer].astype(F32),
             jnp.ones((MLA_QK_PAD - MLA_QK,), F32)])[None, :]
        gd = diff_q_norm_w[layer].astype(F32) * diff_k_norm_w[layer].astype(F32)
        gdk_row = jnp.concatenate([gd, gd])[None, :]

        qm_t, km, vm_t, qd_t, kd, vd_t = _projections(
            x, win_t, wuq_t, wukv_t, cos_t, sin_t, gk_row, gdk_row)

        o_mla = _mla_attention(qm_t, km, vm_t)
        o_diff = _diff_attention(
            qd_t, kd, vd_t, bias_rows,
            lambda_q1[layer].astype(F32)[None, :], lambda_k1[layer].astype(F32)[None, :],
            lambda_q2[layer].astype(F32)[None, :], lambda_k2[layer].astype(F32)[None, :],
            diff_out_norm_w[layer].astype(F32)[None, :], lam_init)

        wo = w_out[layer].astype(BF16)
        wup = (w_up[layer].astype(F32) * mlp_norm_w[layer].astype(F32)[:, None]).astype(BF16)
        x = _out_proj_mlp(
            x.reshape(b * s, d), o_mla.reshape(b * s, MLA_WIDTH), o_diff.reshape(b * s, DIFF_WIDTH),
            wo[:MLA_WIDTH], wo[MLA_WIDTH:], wup, w_down[layer].astype(BF16)).reshape(b, s, d)
    return x
```

```python
import functools
import math

import jax
import jax.numpy as jnp
from jax import lax
from jax.experimental import pallas as pl
from jax.experimental.pallas import tpu as pltpu

D_MODEL = 1024
DEPTH = 1
MLA_HEADS = 4
MLA_NOPE = 128
MLA_ROPE = 64
MLA_V = 128
Q_LORA = D_MODEL // 4
KV_LORA = D_MODEL // 8
MLA_QK = MLA_NOPE + MLA_ROPE
MLA_QK_PAD = 256
DIFF_HEADS = 4
DIFF_D = 64
DIFF_V = 2 * DIFF_D
DIFF_QK_WIDTH = DIFF_HEADS * 2 * DIFF_D
DIFF_WIDTH = DIFF_HEADS * DIFF_V
MLA_WIDTH = MLA_HEADS * MLA_V
D_FF = 4 * D_MODEL
N_BUCKETS = 32
MAX_DISTANCE = 128
ROPE_THETA = 10000.0
EPS = 1e-6
LOG2E = math.log2(math.e)

V7X_VMEM_LIMIT_BYTES = 56 * 1024 * 1024

PROJ_TOKENS = 1024
PROJ_SUB_TOKENS = 256
ATTN_TQ = 512
ATTN_TK = ATTN_TQ
MLP_TOKENS = 1024
MLP_FF = 2048
MLP_FF_CHUNK = 1024

F32 = jnp.float32
BF16 = jnp.bfloat16


def _dot(a, b):
    return jnp.dot(a, b, preferred_element_type=F32)


def _dot_nt(a, b):
    return lax.dot_general(a, b, (((1,), (1,)), ((), ())), preferred_element_type=F32)


def _rotate_half(x):
    x1, x2 = jnp.split(x, 2, axis=-1)
    return jnp.concatenate([-x2, x1], axis=-1)


def _proj_kernel(x_ref, win_ref, wuq_ref, wukv_ref, cos_ref, sin_ref, gk_ref, gdk_ref,
                 qm_ref, km_ref, vm_ref, qd_ref, kd_ref, vd_ref):
    ys = []
    for i in range(x_ref.shape[1] // PROJ_SUB_TOKENS):
        x = x_ref[0, i * PROJ_SUB_TOKENS:(i + 1) * PROJ_SUB_TOKENS]
        h = x * lax.rsqrt(jnp.mean(x * x, axis=-1, keepdims=True) + EPS)
        ys.append(_dot_nt(win_ref[...], h.astype(BF16)))
    for i, y in enumerate(ys):
        _proj_heads(y, slice(i * PROJ_SUB_TOKENS, (i + 1) * PROJ_SUB_TOKENS),
                    wuq_ref, wukv_ref, cos_ref, sin_ref, gk_ref, gdk_ref,
                    qm_ref, km_ref, vm_ref, qd_ref, kd_ref, vd_ref)


def _proj_heads(y, tok, wuq_ref, wukv_ref, cos_ref, sin_ref, gk_ref, gdk_ref,
                qm_ref, km_ref, vm_ref, qd_ref, kd_ref, vd_ref):
    t = y.shape[1]
    cos = cos_ref[:, tok]
    sin = sin_ref[:, tok]

    o0 = 0
    c_q = y[o0:o0 + Q_LORA]
    o0 += Q_LORA
    c_kv = y[o0:o0 + KV_LORA]
    o0 += KV_LORA
    k_rope = y[o0:o0 + MLA_ROPE]
    o0 += MLA_ROPE
    k_rope_rot = y[o0:o0 + MLA_ROPE]
    o0 += MLA_ROPE
    dq = y[o0:o0 + DIFF_QK_WIDTH]
    o0 += DIFF_QK_WIDTH
    dk = y[o0:o0 + DIFF_QK_WIDTH]
    o0 += DIFF_QK_WIDTH
    dv = y[o0:o0 + DIFF_WIDTH]

    cq_n = c_q * lax.rsqrt(jnp.mean(c_q * c_q, axis=0, keepdims=True) + EPS)
    q_all = _dot(wuq_ref[...], cq_n.astype(BF16))
    q_scale = (MLA_QK ** -0.5) * LOG2E
    zeros_pad = jnp.zeros((MLA_QK_PAD - MLA_QK, t), F32)
    for hd in range(MLA_HEADS):
        nope = q_all[hd * MLA_NOPE:(hd + 1) * MLA_NOPE]
        r0 = MLA_HEADS * MLA_NOPE + hd * MLA_ROPE
        r1 = MLA_HEADS * (MLA_NOPE + MLA_ROPE) + hd * MLA_ROPE
        rope = q_all[r0:r0 + MLA_ROPE] * cos + q_all[r1:r1 + MLA_ROPE] * sin
        ss = jnp.sum(nope * nope, axis=0, keepdims=True) + jnp.sum(rope * rope, axis=0, keepdims=True)
        r = lax.rsqrt(ss * (1.0 / MLA_QK) + EPS) * q_scale
        qm_ref[0, hd, 0:MLA_NOPE, tok] = (nope * r).astype(BF16)
        qm_ref[0, hd, MLA_NOPE:MLA_QK, tok] = (rope * r).astype(BF16)
        qm_ref[0, hd, MLA_QK:MLA_QK_PAD, tok] = zeros_pad.astype(BF16)

    ckv_n = c_kv * lax.rsqrt(jnp.mean(c_kv * c_kv, axis=0, keepdims=True) + EPS)
    kv = _dot(wukv_ref[...], ckv_n.astype(BF16))
    k_r = k_rope * cos + k_rope_rot * sin
    k_r_ss = jnp.sum(k_r * k_r, axis=0, keepdims=True)
    gk = gk_ref[...]
    for hd in range(MLA_HEADS):
        k_nope = kv[hd * MLA_NOPE:(hd + 1) * MLA_NOPE]
        ss = jnp.sum(k_nope * k_nope, axis=0, keepdims=True) + k_r_ss
        r = lax.rsqrt(ss * (1.0 / MLA_QK) + EPS)
        k_t = jnp.concatenate([k_nope * r, k_r * r, zeros_pad], axis=0)
        km_ref[0, hd, tok, :] = (k_t.T * gk).astype(BF16)
        v0 = MLA_HEADS * MLA_NOPE + hd * MLA_V
        vm_ref[0, hd, :, tok] = kv[v0:v0 + MLA_V].astype(BF16)

    d_scale = (DIFF_D ** -0.5) * LOG2E
    zeros_half = jnp.zeros((DIFF_D, t), BF16)
    gdk = gdk_ref[...]
    for hd in range(DIFF_HEADS):
        k_parts = []
        for mp in range(2):
            r0 = hd * DIFF_V + mp * DIFF_D
            qv = dq[r0:r0 + DIFF_D]
            qn = qv * (lax.rsqrt(jnp.mean(qv * qv, axis=0, keepdims=True) + EPS) * d_scale)
            qd_ref[0, hd, mp, mp * DIFF_D:(mp + 1) * DIFF_D, tok] = qn.astype(BF16)
            qd_ref[0, hd, mp, (1 - mp) * DIFF_D:(2 - mp) * DIFF_D, tok] = zeros_half
            kv_ = dk[r0:r0 + DIFF_D]
            k_parts.append(kv_ * lax.rsqrt(jnp.mean(kv_ * kv_, axis=0, keepdims=True) + EPS))
        k_t = jnp.concatenate(k_parts, axis=0)
        kd_ref[0, hd, tok, :] = (k_t.T * gdk).astype(BF16)
        vd_ref[0, hd, :, tok] = dv[hd * DIFF_V:(hd + 1) * DIFF_V].astype(BF16)


def _projections(x, win_t, wuq_t, wukv_t, cos_t, sin_t, gk_row, gdk_row):
    b, s, d = x.shape
    t = PROJ_TOKENS
    n_in = win_t.shape[0]
    const = lambda shape: pl.BlockSpec(shape, lambda bi, si: (0,) * len(shape))
    out_shape = (
        jax.ShapeDtypeStruct((b, MLA_HEADS, MLA_QK_PAD, s), BF16),
        jax.ShapeDtypeStruct((b, MLA_HEADS, s, MLA_QK_PAD), BF16),
        jax.ShapeDtypeStruct((b, MLA_HEADS, MLA_V, s), BF16),
        jax.ShapeDtypeStruct((b, DIFF_HEADS, 2, DIFF_V, s), BF16),
        jax.ShapeDtypeStruct((b, DIFF_HEADS, s, DIFF_V), BF16),
        jax.ShapeDtypeStruct((b, DIFF_HEADS, DIFF_V, s), BF16),
    )
    out_specs = (
        pl.BlockSpec((1, MLA_HEADS, MLA_QK_PAD, t), lambda bi, si: (bi, 0, 0, si)),
        pl.BlockSpec((1, MLA_HEADS, t, MLA_QK_PAD), lambda bi, si: (bi, 0, si, 0)),
        pl.BlockSpec((1, MLA_HEADS, MLA_V, t), lambda bi, si: (bi, 0, 0, si)),
        pl.BlockSpec((1, DIFF_HEADS, 2, DIFF_V, t), lambda bi, si: (bi, 0, 0, 0, si)),
        pl.BlockSpec((1, DIFF_HEADS, t, DIFF_V), lambda bi, si: (bi, 0, si, 0)),
        pl.BlockSpec((1, DIFF_HEADS, DIFF_V, t), lambda bi, si: (bi, 0, 0, si)),
    )
    in_specs = [
        pl.BlockSpec((1, t, d), lambda bi, si: (bi, si, 0)),
        const((n_in, d)),
        const(wuq_t.shape),
        const(wukv_t.shape),
        pl.BlockSpec((MLA_ROPE, t), lambda bi, si: (0, si)),
        pl.BlockSpec((MLA_ROPE, t), lambda bi, si: (0, si)),
        const(gk_row.shape),
        const(gdk_row.shape),
    ]
    return pl.pallas_call(
        _proj_kernel,
        out_shape=out_shape,
        grid=(b, s // t),
        in_specs=in_specs,
        out_specs=out_specs,
        compiler_params=pltpu.CompilerParams(
            dimension_semantics=("arbitrary", "arbitrary"),
            vmem_limit_bytes=V7X_VMEM_LIMIT_BYTES),
        name="proj",
    )(x, win_t, wuq_t, wukv_t, cos_t, sin_t, gk_row, gdk_row)


def _attn_sweeps(k_ref, q_t, v_ref, score_bufs, prev_bufs, tk, bias=None, read_base=None):
    s_w, m_w, c_w = score_bufs
    s_r, m_r, c_r = prev_bufs
    n_keys, tq = s_w.shape
    n_chunks = n_keys // tk
    m_prev = m_r[...]
    m_new = None
    l = jnp.zeros((1, tq), F32)
    acc = jnp.zeros((v_ref.shape[2], tq), F32)
    if bias is not None:
        bias_ref, qi = bias
        n_bias = bias_ref.shape[0]
        far = (n_bias - 1) // 2
        n_near = n_bias - 2
        first_near = jnp.clip(qi - n_near // 2, 0, n_chunks - n_near)
        c_left = bias_ref[0, 0:1, :]
        c_right = bias_ref[n_bias - 1, 0:1, :]
    for j in range(n_chunks):
        if bias is None:
            rows_w = slice(j * tk, (j + 1) * tk)
            s = _dot(k_ref[0, 0, rows_w, :], q_t)
            m_c = jnp.max(s, axis=0, keepdims=True)
        else:
            c = first_near + j if j < n_near else lax.rem(first_near + j, n_chunks)
            rows_w = pl.ds(pl.multiple_of(c * tk, tk), tk)
            s = _dot(k_ref[0, 0, rows_w, :], q_t)
            if j < n_near:
                s = s + bias_ref[jnp.clip(c - qi, -far, far) + far]
                m_c = jnp.max(s, axis=0, keepdims=True)
                c_w[c] = jnp.zeros((1, tq), F32)
            else:
                off = jnp.where(c > qi, c_right, c_left)
                m_c = jnp.max(s, axis=0, keepdims=True) + off
                c_w[c] = off
        s_w[rows_w, :] = s
        m_new = m_c if m_new is None else jnp.maximum(m_new, m_c)

        if read_base is None:
            rows_r = slice(j * tk, (j + 1) * tk)
        else:
            rows_r = pl.ds(pl.multiple_of(read_base + j * tk, tk), tk)
        shift = m_prev if c_r is None else m_prev - c_r[j]
        p = jnp.exp2(s_r[rows_r, :] - shift)
        l = l + jnp.sum(p, axis=0, keepdims=True)
        acc = acc + _dot(v_ref[0, 0, :, rows_r], p.astype(BF16))
    m_w[...] = m_new
    return acc, l


def _mla_attn_kernel(q_ref, k_ref, vp_ref, vc_ref, o_ref, s_a, s_b, m_a, m_b, acc_d, l_d):
    tq = s_a.shape[1]

    @pl.when(pl.program_id(0) == 0)
    def _():
        s_b[...] = jnp.zeros_like(s_b)
        m_b[...] = jnp.zeros_like(m_b)
        acc_d[...] = jnp.zeros_like(acc_d)
        l_d[...] = jnp.ones_like(l_d)

    def store(rows, acc, l):
        o_ref[0, rows, :] = (acc * (1.0 / l)).astype(o_ref.dtype).T

    store(slice(0, tq), acc_d[...], l_d[...])
    acc, l = _attn_sweeps(k_ref, q_ref[0, 0, :, 0:tq], vp_ref, (s_a, m_a, None), (s_b, m_b, None),
                          ATTN_TK)
    store(slice(tq, 2 * tq), acc, l)
    acc, l = _attn_sweeps(k_ref, q_ref[0, 0, :, tq:2 * tq], vc_ref, (s_b, m_b, None), (s_a, m_a, None),
                          ATTN_TK, read_base=jnp.minimum(pl.program_id(0), 0))
    acc_d[...] = acc
    l_d[...] = l


def _mla_attention(q_t, k, v_t):
    b, h, _, s = q_t.shape
    tq = ATTN_TQ
    n_qp = s // (2 * tq)
    n_pairs = b * h * n_qp

    def coords(p):
        return p // (h * n_qp), (p // n_qp) % h, p % n_qp

    def prev(g):
        return coords(jnp.maximum(g - 1, 0))

    def cur(g):
        return coords(jnp.minimum(g, n_pairs - 1))

    return pl.pallas_call(
        _mla_attn_kernel,
        out_shape=jax.ShapeDtypeStruct((b, s, h * MLA_V), BF16),
        grid=(n_pairs + 1,),
        in_specs=[
            pl.BlockSpec((1, 1, MLA_QK_PAD, 2 * tq), lambda g: (cur(g)[0], cur(g)[1], 0, cur(g)[2])),
            pl.BlockSpec((1, 1, s, MLA_QK_PAD), lambda g: (cur(g)[0], cur(g)[1], 0, 0)),
            pl.BlockSpec((1, 1, MLA_V, s), lambda g: (prev(g)[0], prev(g)[1], 0, 0)),
            pl.BlockSpec((1, 1, MLA_V, s), lambda g: (cur(g)[0], cur(g)[1], 0, 0)),
        ],
        out_specs=pl.BlockSpec((1, 2 * tq, MLA_V), lambda g: (prev(g)[0], prev(g)[2], prev(g)[1])),
        scratch_shapes=[pltpu.VMEM((s, tq), F32), pltpu.VMEM((s, tq), F32),
                        pltpu.VMEM((1, tq), F32), pltpu.VMEM((1, tq), F32),
                        pltpu.VMEM((MLA_V, tq), F32), pltpu.VMEM((1, tq), F32)],
        compiler_params=pltpu.CompilerParams(
            dimension_semantics=("arbitrary",),
            vmem_limit_bytes=V7X_VMEM_LIMIT_BYTES),
        name="mla_attn",
    )(q_t, k, v_t, v_t)


def _diff_attn_kernel(q_ref, k_ref, vp_ref, vc_ref, brow_ref,
                      lq1_ref, lk1_ref, lq2_ref, lk2_ref, g_ref, o_ref,
                      s_a, s_b, m_a, m_b, c_a, c_b, bias_ref, o1_d, *, n_pairs, n_q, lam_init):
    g = pl.program_id(1)

    @pl.when(g == 0)
    def _():
        s_b[...] = jnp.zeros_like(s_b)
        m_b[...] = jnp.zeros_like(m_b)
        c_b[...] = jnp.zeros_like(c_b)
        o1_d[...] = jnp.zeros_like(o1_d)
        n_bias, tk, tq = bias_ref.shape
        for j in range(n_bias):
            rows = jnp.broadcast_to(brow_ref[0, j], (tk, brow_ref.shape[-1]))
            bias_ref[j] = pltpu.roll(rows, 0, 1, stride=1, stride_axis=0)[:, :tq]

    qi = lax.rem(jnp.minimum(g, n_pairs - 1), n_q)
    lam = (jnp.exp(jnp.sum(lq1_ref[...] * lk1_ref[...], axis=-1, keepdims=True))
           - jnp.exp(jnp.sum(lq2_ref[...] * lk2_ref[...], axis=-1, keepdims=True))
           + lam_init)
    a2, l2 = _attn_sweeps(k_ref, q_ref[0, 0, 0], vp_ref, (s_a, m_a, c_a), (s_b, m_b, c_b),
                          ATTN_TK, bias=(bias_ref, qi))
    o = o1_d[...] - lam * (a2 * (1.0 / l2))
    o = o * lax.rsqrt(jnp.mean(o * o, axis=0, keepdims=True) + EPS)
    o_ref[0] = (o.T * (g_ref[...] * (1.0 - lam_init))).astype(o_ref.dtype)
    a1, l1 = _attn_sweeps(k_ref, q_ref[0, 0, 1], vc_ref, (s_b, m_b, c_b), (s_a, m_a, c_a),
                          ATTN_TK, bias=(bias_ref, qi))
    o1_d[...] = a1 * (1.0 / l1)


def _diff_attention(q_t, k, v_t, bias_rows, lq1, lk1, lq2, lk2, g_row, lam_init):
    b, h, _, _, s = q_t.shape
    tq = ATTN_TQ
    n_q = s // tq
    n_pairs = b * n_q
    n_bias, _, span = bias_rows.shape[1:]
    assert s // ATTN_TK >= n_bias - 2
    kern = functools.partial(_diff_attn_kernel, n_pairs=n_pairs, n_q=n_q, lam_init=lam_init)

    def prev(g):
        p = jnp.maximum(g - 1, 0)
        return p // n_q, p % n_q

    def cur(g):
        p = jnp.minimum(g, n_pairs - 1)
        return p // n_q, p % n_q

    vec = pl.BlockSpec((1, DIFF_D), lambda hi, g: (0, 0))
    return pl.pallas_call(
        kern,
        out_shape=jax.ShapeDtypeStruct((b, s, h * DIFF_V), BF16),
        grid=(h, n_pairs + 1),
        in_specs=[
            pl.BlockSpec((1, 1, 2, DIFF_V, tq), lambda hi, g: (cur(g)[0], hi, 0, 0, cur(g)[1])),
            pl.BlockSpec((1, 1, s, DIFF_V), lambda hi, g: (cur(g)[0], hi, 0, 0)),
            pl.BlockSpec((1, 1, DIFF_V, s), lambda hi, g: (prev(g)[0], hi, 0, 0)),
            pl.BlockSpec((1, 1, DIFF_V, s), lambda hi, g: (cur(g)[0], hi, 0, 0)),
            pl.BlockSpec((1, n_bias, 1, span), lambda hi, g: (hi, 0, 0, 0)),
            vec, vec, vec, vec,
            pl.BlockSpec((1, DIFF_V), lambda hi, g: (0, 0)),
        ],
        out_specs=pl.BlockSpec((1, tq, DIFF_V), lambda hi, g: (prev(g)[0], prev(g)[1], hi)),
        scratch_shapes=[pltpu.VMEM((s, tq), F32), pltpu.VMEM((s, tq), F32),
                        pltpu.VMEM((1, tq), F32), pltpu.VMEM((1, tq), F32),
                        pltpu.VMEM((s // ATTN_TK, 1, tq), F32), pltpu.VMEM((s // ATTN_TK, 1, tq), F32),
                        pltpu.VMEM((n_bias, ATTN_TK, tq), F32),
                        pltpu.VMEM((DIFF_V, tq), F32)],
        compiler_params=pltpu.CompilerParams(
            dimension_semantics=("arbitrary", "arbitrary"),
            vmem_limit_bytes=V7X_VMEM_LIMIT_BYTES),
        name="diff_attn",
    )(q_t, k, v_t, v_t, bias_rows, lq1, lk1, lq2, lk2, g_row)


def _t5_bucket(rel):
    half = N_BUCKETS // 2
    ret = jnp.where(rel > 0, half, 0)
    n = jnp.abs(rel)
    max_exact = half // 2
    large = max_exact + (jnp.log(jnp.maximum(n, 1).astype(jnp.float32) / max_exact)
                         / math.log(MAX_DISTANCE / max_exact)
                         * (half - max_exact)).astype(jnp.int32)
    large = jnp.minimum(large, half - 1)
    return ret + jnp.where(n < max_exact, n, large)


def _bias_rows(rel_bias, t):
    far = pl.cdiv(t - 1 + MAX_DISTANCE, t)
    offs = jnp.arange(-far, far + 1, dtype=jnp.int32) * t
    span = 2 * t
    m = jnp.arange(span, dtype=jnp.int32)
    q_minus_k = jnp.where(m < t, m, m - span)
    rel = offs[:, None] - q_minus_k[None, :]
    table = rel_bias.astype(F32) * LOG2E
    bucket = _t5_bucket(rel)[None]
    rows = jnp.zeros((table.shape[1],) + rel.shape, F32)
    for bkt in range(N_BUCKETS):
        rows = jnp.where(bucket == bkt, table[bkt][:, None, None], rows)
    return rows[:, :, None, :]


def _mlp_kernel(x_ref, om_ref, od_ref, wo1_ref, wo2_ref, wup_ref, wdn_ref, o_ref, h_ref):
    j = pl.program_id(1)

    @pl.when(j == 0)
    def _():
        x1 = x_ref[...] + _dot(om_ref[...], wo1_ref[...]) + _dot(od_ref[...], wo2_ref[...])
        o_ref[...] = x1
        h_ref[...] = (x1 * lax.rsqrt(jnp.mean(x1 * x1, axis=-1, keepdims=True) + EPS)).astype(BF16)

    for c in range(wup_ref.shape[1] // MLP_FF_CHUNK):
        cols = slice(c * MLP_FF_CHUNK, (c + 1) * MLP_FF_CHUNK)
        up = _dot(h_ref[...], wup_ref[:, cols])
        act = jnp.square(jnp.maximum(up, 0.0)).astype(BF16)
        o_ref[...] += _dot(act, wdn_ref[cols, :])


def _out_proj_mlp(x2d, o_mla, o_diff, wo1, wo2, wup, wdn):
    n, d = x2d.shape
    t, f = MLP_TOKENS, MLP_FF
    return pl.pallas_call(
        _mlp_kernel,
        out_shape=jax.ShapeDtypeStruct((n, d), F32),
        grid=(n // t, D_FF // f),
        in_specs=[
            pl.BlockSpec((t, d), lambda i, j: (i, 0)),
            pl.BlockSpec((t, MLA_WIDTH), lambda i, j: (i, 0)),
            pl.BlockSpec((t, DIFF_WIDTH), lambda i, j: (i, 0)),
            pl.BlockSpec((MLA_WIDTH, d), lambda i, j: (0, 0)),
            pl.BlockSpec((DIFF_WIDTH, d), lambda i, j: (0, 0)),
            pl.BlockSpec((d, f), lambda i, j: (0, j)),
            pl.BlockSpec((f, d), lambda i, j: (j, 0)),
        ],
        out_specs=pl.BlockSpec((t, d), lambda i, j: (i, 0)),
        scratch_shapes=[pltpu.VMEM((t, d), BF16)],
        compiler_params=pltpu.CompilerParams(
            dimension_semantics=("arbitrary", "arbitrary"),
            vmem_limit_bytes=V7X_VMEM_LIMIT_BYTES),
        name="out_mlp",
    )(x2d, o_mla, o_diff, wo1, wo2, wup, wdn)


def _rope_tables_t(seq):
    inv = ROPE_THETA ** (-jnp.arange(0, MLA_ROPE, 2, dtype=jnp.float32) / MLA_ROPE)
    ang = jnp.arange(seq, dtype=jnp.float32)[:, None] * inv[None, :]
    ang = jnp.concatenate([ang, ang], axis=-1)
    return jnp.cos(ang).T, jnp.sin(ang).T


def kernel(x, attn_norm_w, w_in, q_a_norm_w, w_uq, kv_a_norm_w, w_ukv, mla_q_norm_w, mla_k_norm_w, diff_q_norm_w, diff_k_norm_w, lambda_q1, lambda_k1, lambda_q2, lambda_k2, diff_out_norm_w, w_out, mlp_norm_w, w_up, w_down, rel_bias):
    b, s, d = x.shape
    cos_t, sin_t = _rope_tables_t(s)
    bias_rows = _bias_rows(rel_bias, ATTN_TQ)
    for layer in range(DEPTH):
        lam_init = 0.8 - 0.6 * math.exp(-0.3 * layer)

        wi = w_in[layer].astype(F32) * attn_norm_w[layer].astype(F32)[:, None]
        c0, c1, c2, c3, c4 = (Q_LORA, Q_LORA + KV_LORA, Q_LORA + KV_LORA + MLA_ROPE,
                              Q_LORA + KV_LORA + MLA_ROPE + DIFF_QK_WIDTH,
                              Q_LORA + KV_LORA + MLA_ROPE + 2 * DIFF_QK_WIDTH)
        w_krope = wi[:, c1:c2]
        win_t = jnp.concatenate(
            [wi[:, :c1], w_krope, _rotate_half(w_krope), wi[:, c2:c3], wi[:, c3:c4], wi[:, c4:]],
            axis=1).T.astype(BF16)

        wq = (w_uq[layer].astype(F32) * q_a_norm_w[layer].astype(F32)[:, None]
              ).reshape(Q_LORA, MLA_HEADS, MLA_QK)
        wq_nope = wq[:, :, :MLA_NOPE].reshape(Q_LORA, -1)
        wq_rope = wq[:, :, MLA_NOPE:]
        wuq_t = jnp.concatenate(
            [wq_nope, wq_rope.reshape(Q_LORA, -1), _rotate_half(wq_rope).reshape(Q_LORA, -1)],
            axis=1).T.astype(BF16)

        wkv = (w_ukv[layer].astype(F32) * kv_a_norm_w[layer].astype(F32)[:, None]
               ).reshape(KV_LORA, MLA_HEADS, MLA_NOPE + MLA_V)
        wukv_t = jnp.concatenate(
            [wkv[:, :, :MLA_NOPE].reshape(KV_LORA, -1), wkv[:, :, MLA_NOPE:].reshape(KV_LORA, -1)],
            axis=1).T.astype(BF16)

        gk_row = jnp.concatenate(
            [mla_q_norm_w[layer].astype(F32) * mla_k_norm_w[layer].astype(F32),
             jnp.ones((MLA_QK_PAD - MLA_QK,), F32)])[None, :]
        gd = diff_q_norm_w[layer].astype(F32) * diff_k_norm_w[layer].astype(F32)
        gdk_row = jnp.concatenate([gd, gd])[None, :]

        qm_t, km, vm_t, qd_t, kd, vd_t = _projections(
            x, win_t, wuq_t, wukv_t, cos_t, sin_t, gk_row, gdk_row)

        o_mla = _mla_attention(qm_t, km, vm_t)
        o_diff = _diff_attention(
            qd_t, kd, vd_t, bias_rows,
            lambda_q1[layer].astype(F32)[None, :], lambda_k1[layer].astype(F32)[None, :],
            lambda_q2[layer].astype(F32)[None, :], lambda_k2[layer].astype(F32)[None, :],
            diff_out_norm_w[layer].astype(F32)[None, :], lam_init)

        wo = w_out[layer].astype(BF16)
        wup = (w_up[layer].astype(F32) * mlp_norm_w[layer].astype(F32)[:, None]).astype(BF16)
        x = _out_proj_mlp(
            x.reshape(b * s, d), o_mla.reshape(b * s, MLA_WIDTH), o_diff.reshape(b * s, DIFF_WIDTH),
            wo[:MLA_WIDTH], wo[MLA_WIDTH:], wup, w_down[layer].astype(BF16)).reshape(b, s, d)
    return x
```

```python
import functools
import math

import jax
import jax.numpy as jnp
from jax import lax
from jax.experimental import pallas as pl
from jax.experimental.pallas import tpu as pltpu

D_MODEL = 1024
DEPTH = 1
MLA_HEADS = 4
MLA_NOPE = 128
MLA_ROPE = 64
MLA_V = 128
Q_LORA = D_MODEL // 4
KV_LORA = D_MODEL // 8
MLA_QK = MLA_NOPE + MLA_ROPE
MLA_QK_PAD = 256
DIFF_HEADS = 4
DIFF_D = 64
DIFF_V = 2 * DIFF_D
DIFF_QK_WIDTH = DIFF_HEADS * 2 * DIFF_D
DIFF_WIDTH = DIFF_HEADS * DIFF_V
MLA_WIDTH = MLA_HEADS * MLA_V
D_FF = 4 * D_MODEL
N_BUCKETS = 32
MAX_DISTANCE = 128
ROPE_THETA = 10000.0
EPS = 1e-6
LOG2E = math.log2(math.e)

V7X_VMEM_LIMIT_BYTES = 56 * 1024 * 1024

PROJ_TOKENS = 1024
PROJ_SUB_TOKENS = 256
ATTN_TQ = 512
ATTN_TK = ATTN_TQ
MLP_TOKENS = 1024
MLP_FF = 2048
MLP_FF_CHUNK = 1024
MLP_ROWS = 256

F32 = jnp.float32
BF16 = jnp.bfloat16


def _dot(a, b):
    return jnp.dot(a, b, preferred_element_type=F32)


def _dot_nt(a, b):
    return lax.dot_general(a, b, (((1,), (1,)), ((), ())), preferred_element_type=F32)


def _rotate_half(x):
    x1, x2 = jnp.split(x, 2, axis=-1)
    return jnp.concatenate([-x2, x1], axis=-1)


def _proj_kernel(x_ref, win_ref, wuq_ref, wukv_ref, cos_ref, sin_ref, gk_ref, gdk_ref,
                 qm_ref, km_ref, vm_ref, qd_ref, kd_ref, vd_ref):
    ys = []
    for i in range(x_ref.shape[1] // PROJ_SUB_TOKENS):
        x = x_ref[0, i * PROJ_SUB_TOKENS:(i + 1) * PROJ_SUB_TOKENS]
        h = x * lax.rsqrt(jnp.mean(x * x, axis=-1, keepdims=True) + EPS)
        ys.append(_dot_nt(win_ref[...], h.astype(BF16)))
    for i, y in enumerate(ys):
        _proj_heads(y, slice(i * PROJ_SUB_TOKENS, (i + 1) * PROJ_SUB_TOKENS),
                    wuq_ref, wukv_ref, cos_ref, sin_ref, gk_ref, gdk_ref,
                    qm_ref, km_ref, vm_ref, qd_ref, kd_ref, vd_ref)


def _proj_heads(y, tok, wuq_ref, wukv_ref, cos_ref, sin_ref, gk_ref, gdk_ref,
                qm_ref, km_ref, vm_ref, qd_ref, kd_ref, vd_ref):
    t = y.shape[1]
    cos = cos_ref[:, tok]
    sin = sin_ref[:, tok]

    o0 = 0
    c_q = y[o0:o0 + Q_LORA]
    o0 += Q_LORA
    c_kv = y[o0:o0 + KV_LORA]
    o0 += KV_LORA
    k_rope = y[o0:o0 + MLA_ROPE]
    o0 += MLA_ROPE
    k_rope_rot = y[o0:o0 + MLA_ROPE]
    o0 += MLA_ROPE
    dq = y[o0:o0 + DIFF_QK_WIDTH]
    o0 += DIFF_QK_WIDTH
    dk = y[o0:o0 + DIFF_QK_WIDTH]
    o0 += DIFF_QK_WIDTH
    dv = y[o0:o0 + DIFF_WIDTH]

    cq_n = c_q * lax.rsqrt(jnp.mean(c_q * c_q, axis=0, keepdims=True) + EPS)
    q_all = _dot(wuq_ref[...], cq_n.astype(BF16))
    q_scale = (MLA_QK ** -0.5) * LOG2E
    zeros_pad = jnp.zeros((MLA_QK_PAD - MLA_QK, t), F32)
    for hd in range(MLA_HEADS):
        nope = q_all[hd * MLA_NOPE:(hd + 1) * MLA_NOPE]
        r0 = MLA_HEADS * MLA_NOPE + hd * MLA_ROPE
        r1 = MLA_HEADS * (MLA_NOPE + MLA_ROPE) + hd * MLA_ROPE
        rope = q_all[r0:r0 + MLA_ROPE] * cos + q_all[r1:r1 + MLA_ROPE] * sin
        ss = jnp.sum(nope * nope, axis=0, keepdims=True) + jnp.sum(rope * rope, axis=0, keepdims=True)
        r = lax.rsqrt(ss * (1.0 / MLA_QK) + EPS) * q_scale
        qm_ref[0, hd, 0:MLA_NOPE, tok] = (nope * r).astype(BF16)
        qm_ref[0, hd, MLA_NOPE:MLA_QK, tok] = (rope * r).astype(BF16)
        qm_ref[0, hd, MLA_QK:MLA_QK_PAD, tok] = zeros_pad.astype(BF16)

    ckv_n = c_kv * lax.rsqrt(jnp.mean(c_kv * c_kv, axis=0, keepdims=True) + EPS)
    kv = _dot(wukv_ref[...], ckv_n.astype(BF16))
    k_r = k_rope * cos + k_rope_rot * sin
    k_r_ss = jnp.sum(k_r * k_r, axis=0, keepdims=True)
    gk = gk_ref[...]
    for hd in range(MLA_HEADS):
        k_nope = kv[hd * MLA_NOPE:(hd + 1) * MLA_NOPE]
        ss = jnp.sum(k_nope * k_nope, axis=0, keepdims=True) + k_r_ss
        r = lax.rsqrt(ss * (1.0 / MLA_QK) + EPS)
        k_t = jnp.concatenate([k_nope * r, k_r * r, zeros_pad], axis=0)
        km_ref[0, hd, tok, :] = (k_t.T * gk).astype(BF16)
        v0 = MLA_HEADS * MLA_NOPE + hd * MLA_V
        vm_ref[0, hd, :, tok] = kv[v0:v0 + MLA_V].astype(BF16)

    d_scale = (DIFF_D ** -0.5) * LOG2E
    zeros_half = jnp.zeros((DIFF_D, t), BF16)
    gdk = gdk_ref[...]
    for hd in range(DIFF_HEADS):
        k_parts = []
        for mp in range(2):
            r0 = hd * DIFF_V + mp * DIFF_D
            qv = dq[r0:r0 + DIFF_D]
            qn = qv * (lax.rsqrt(jnp.mean(qv * qv, axis=0, keepdims=True) + EPS) * d_scale)
            qd_ref[0, hd, mp, mp * DIFF_D:(mp + 1) * DIFF_D, tok] = qn.astype(BF16)
            qd_ref[0, hd, mp, (1 - mp) * DIFF_D:(2 - mp) * DIFF_D, tok] = zeros_half
            kv_ = dk[r0:r0 + DIFF_D]
            k_parts.append(kv_ * lax.rsqrt(jnp.mean(kv_ * kv_, axis=0, keepdims=True) + EPS))
        k_t = jnp.concatenate(k_parts, axis=0)
        kd_ref[0, hd, tok, :] = (k_t.T * gdk).astype(BF16)
        vd_ref[0, hd, :, tok] = dv[hd * DIFF_V:(hd + 1) * DIFF_V].astype(BF16)


def _projections(x, win_t, wuq_t, wukv_t, cos_t, sin_t, gk_row, gdk_row):
    b, s, d = x.shape
    t = PROJ_TOKENS
    n_in = win_t.shape[0]
    const = lambda shape: pl.BlockSpec(shape, lambda bi, si: (0,) * len(shape))
    out_shape = (
        jax.ShapeDtypeStruct((b, MLA_HEADS, MLA_QK_PAD, s), BF16),
        jax.ShapeDtypeStruct((b, MLA_HEADS, s, MLA_QK_PAD), BF16),
        jax.ShapeDtypeStruct((b, MLA_HEADS, MLA_V, s), BF16),
        jax.ShapeDtypeStruct((b, DIFF_HEADS, 2, DIFF_V, s), BF16),
        jax.ShapeDtypeStruct((b, DIFF_HEADS, s, DIFF_V), BF16),
        jax.ShapeDtypeStruct((b, DIFF_HEADS, DIFF_V, s), BF16),
    )
    out_specs = (
        pl.BlockSpec((1, MLA_HEADS, MLA_QK_PAD, t), lambda bi, si: (bi, 0, 0, si)),
        pl.BlockSpec((1, MLA_HEADS, t, MLA_QK_PAD), lambda bi, si: (bi, 0, si, 0)),
        pl.BlockSpec((1, MLA_HEADS, MLA_V, t), lambda bi, si: (bi, 0, 0, si)),
        pl.BlockSpec((1, DIFF_HEADS, 2, DIFF_V, t), lambda bi, si: (bi, 0, 0, 0, si)),
        pl.BlockSpec((1, DIFF_HEADS, t, DIFF_V), lambda bi, si: (bi, 0, si, 0)),
        pl.BlockSpec((1, DIFF_HEADS, DIFF_V, t), lambda bi, si: (bi, 0, 0, si)),
    )
    in_specs = [
        pl.BlockSpec((1, t, d), lambda bi, si: (bi, si, 0)),
        const((n_in, d)),
        const(wuq_t.shape),
        const(wukv_t.shape),
        pl.BlockSpec((MLA_ROPE, t), lambda bi, si: (0, si)),
        pl.BlockSpec((MLA_ROPE, t), lambda bi, si: (0, si)),
        const(gk_row.shape),
        const(gdk_row.shape),
    ]
    return pl.pallas_call(
        _proj_kernel,
        out_shape=out_shape,
        grid=(b, s // t),
        in_specs=in_specs,
        out_specs=out_specs,
        compiler_params=pltpu.CompilerParams(
            dimension_semantics=("arbitrary", "arbitrary"),
            vmem_limit_bytes=V7X_VMEM_LIMIT_BYTES),
        name="proj",
    )(x, win_t, wuq_t, wukv_t, cos_t, sin_t, gk_row, gdk_row)


def _attn_sweeps(k_ref, q_t, v_ref, score_bufs, prev_bufs, tk, bias=None, read_base=None):
    s_w, m_w, c_w = score_bufs
    s_r, m_r, c_r = prev_bufs
    n_keys, tq = s_w.shape
    n_chunks = n_keys // tk
    m_prev = m_r[...]
    m_new = None
    l = jnp.zeros((1, tq), F32)
    acc = jnp.zeros((v_ref.shape[2], tq), F32)
    if bias is not None:
        bias_ref, qi = bias
        n_bias = bias_ref.shape[0]
        far = (n_bias - 1) // 2
        n_near = n_bias - 2
        first_near = jnp.clip(qi - n_near // 2, 0, n_chunks - n_near)
        c_left = bias_ref[0, 0:1, :]
        c_right = bias_ref[n_bias - 1, 0:1, :]
    for j in range(n_chunks):
        if bias is None:
            rows_w = slice(j * tk, (j + 1) * tk)
            s = _dot(k_ref[0, 0, rows_w, :], q_t)
            m_c = jnp.max(s, axis=0, keepdims=True)
        else:
            c = first_near + j if j < n_near else lax.rem(first_near + j, n_chunks)
            rows_w = pl.ds(pl.multiple_of(c * tk, tk), tk)
            s = _dot(k_ref[0, 0, rows_w, :], q_t)
            if j < n_near:
                s = s + bias_ref[jnp.clip(c - qi, -far, far) + far]
                m_c = jnp.max(s, axis=0, keepdims=True)
                c_w[c] = jnp.zeros((1, tq), F32)
            else:
                off = jnp.where(c > qi, c_right, c_left)
                m_c = jnp.max(s, axis=0, keepdims=True) + off
                c_w[c] = off
        s_w[rows_w, :] = s
        m_new = m_c if m_new is None else jnp.maximum(m_new, m_c)

        if read_base is None:
            rows_r = slice(j * tk, (j + 1) * tk)
        else:
            rows_r = pl.ds(pl.multiple_of(read_base + j * tk, tk), tk)
        shift = m_prev if c_r is None else m_prev - c_r[j]
        p = jnp.exp2(s_r[rows_r, :] - shift)
        l = l + jnp.sum(p, axis=0, keepdims=True)
        acc = acc + _dot(v_ref[0, 0, :, rows_r], p.astype(BF16))
    m_w[...] = m_new
    return acc, l


def _mla_attn_kernel(q_ref, k_ref, vp_ref, vc_ref, o_ref, s_a, s_b, m_a, m_b, acc_d, l_d):
    tq = s_a.shape[1]

    @pl.when(pl.program_id(0) == 0)
    def _():
        s_b[...] = jnp.zeros_like(s_b)
        m_b[...] = jnp.zeros_like(m_b)
        acc_d[...] = jnp.zeros_like(acc_d)
        l_d[...] = jnp.ones_like(l_d)

    def store(rows, acc, l):
        o_ref[0, rows, :] = (acc * (1.0 / l)).astype(o_ref.dtype).T

    store(slice(0, tq), acc_d[...], l_d[...])
    acc, l = _attn_sweeps(k_ref, q_ref[0, 0, :, 0:tq], vp_ref, (s_a, m_a, None), (s_b, m_b, None),
                          ATTN_TK)
    store(slice(tq, 2 * tq), acc, l)
    acc, l = _attn_sweeps(k_ref, q_ref[0, 0, :, tq:2 * tq], vc_ref, (s_b, m_b, None), (s_a, m_a, None),
                          ATTN_TK, read_base=jnp.minimum(pl.program_id(0), 0))
    acc_d[...] = acc
    l_d[...] = l


def _mla_attention(q_t, k, v_t):
    b, h, _, s = q_t.shape
    tq = ATTN_TQ
    n_qp = s // (2 * tq)
    n_pairs = b * h * n_qp

    def coords(p):
        return p // (h * n_qp), (p // n_qp) % h, p % n_qp

    def prev(g):
        return coords(jnp.maximum(g - 1, 0))

    def cur(g):
        return coords(jnp.minimum(g, n_pairs - 1))

    return pl.pallas_call(
        _mla_attn_kernel,
        out_shape=jax.ShapeDtypeStruct((b, s, h * MLA_V), BF16),
        grid=(n_pairs + 1,),
        in_specs=[
            pl.BlockSpec((1, 1, MLA_QK_PAD, 2 * tq), lambda g: (cur(g)[0], cur(g)[1], 0, cur(g)[2])),
            pl.BlockSpec((1, 1, s, MLA_QK_PAD), lambda g: (cur(g)[0], cur(g)[1], 0, 0)),
            pl.BlockSpec((1, 1, MLA_V, s), lambda g: (prev(g)[0], prev(g)[1], 0, 0)),
            pl.BlockSpec((1, 1, MLA_V, s), lambda g: (cur(g)[0], cur(g)[1], 0, 0)),
        ],
        out_specs=pl.BlockSpec((1, 2 * tq, MLA_V), lambda g: (prev(g)[0], prev(g)[2], prev(g)[1])),
        scratch_shapes=[pltpu.VMEM((s, tq), F32), pltpu.VMEM((s, tq), F32),
                        pltpu.VMEM((1, tq), F32), pltpu.VMEM((1, tq), F32),
                        pltpu.VMEM((MLA_V, tq), F32), pltpu.VMEM((1, tq), F32)],
        compiler_params=pltpu.CompilerParams(
            dimension_semantics=("arbitrary",),
            vmem_limit_bytes=V7X_VMEM_LIMIT_BYTES),
        name="mla_attn",
    )(q_t, k, v_t, v_t)


def _diff_attn_kernel(q_ref, k_ref, vp_ref, vc_ref, brow_ref,
                      lq1_ref, lk1_ref, lq2_ref, lk2_ref, g_ref, o_ref,
                      s_a, s_b, m_a, m_b, c_a, c_b, bias_ref, o1_d, *, n_pairs, n_q, lam_init):
    g = pl.program_id(1)

    @pl.when(g == 0)
    def _():
        s_b[...] = jnp.zeros_like(s_b)
        m_b[...] = jnp.zeros_like(m_b)
        c_b[...] = jnp.zeros_like(c_b)
        o1_d[...] = jnp.zeros_like(o1_d)
        n_bias, tk, tq = bias_ref.shape
        for j in range(n_bias):
            rows = jnp.broadcast_to(brow_ref[0, j], (tk, brow_ref.shape[-1]))
            bias_ref[j] = pltpu.roll(rows, 0, 1, stride=1, stride_axis=0)[:, :tq]

    qi = lax.rem(jnp.minimum(g, n_pairs - 1), n_q)
    lam = (jnp.exp(jnp.sum(lq1_ref[...] * lk1_ref[...], axis=-1, keepdims=True))
           - jnp.exp(jnp.sum(lq2_ref[...] * lk2_ref[...], axis=-1, keepdims=True))
           + lam_init)
    a2, l2 = _attn_sweeps(k_ref, q_ref[0, 0, 0], vp_ref, (s_a, m_a, c_a), (s_b, m_b, c_b),
                          ATTN_TK, bias=(bias_ref, qi))
    o = o1_d[...] - lam * (a2 * (1.0 / l2))
    o = o * lax.rsqrt(jnp.mean(o * o, axis=0, keepdims=True) + EPS)
    o_ref[0] = (o.T * (g_ref[...] * (1.0 - lam_init))).astype(o_ref.dtype)
    a1, l1 = _attn_sweeps(k_ref, q_ref[0, 0, 1], vc_ref, (s_b, m_b, c_b), (s_a, m_a, c_a),
                          ATTN_TK, bias=(bias_ref, qi))
    o1_d[...] = a1 * (1.0 / l1)


def _diff_attention(q_t, k, v_t, bias_rows, lq1, lk1, lq2, lk2, g_row, lam_init):
    b, h, _, _, s = q_t.shape
    tq = ATTN_TQ
    n_q = s // tq
    n_pairs = b * n_q
    n_bias, _, span = bias_rows.shape[1:]
    assert s // ATTN_TK >= n_bias - 2
    kern = functools.partial(_diff_attn_kernel, n_pairs=n_pairs, n_q=n_q, lam_init=lam_init)

    def prev(g):
        p = jnp.maximum(g - 1, 0)
        return p // n_q, p % n_q

    def cur(g):
        p = jnp.minimum(g, n_pairs - 1)
        return p // n_q, p % n_q

    vec = pl.BlockSpec((1, DIFF_D), lambda hi, g: (0, 0))
    return pl.pallas_call(
        kern,
        out_shape=jax.ShapeDtypeStruct((b, s, h * DIFF_V), BF16),
        grid=(h, n_pairs + 1),
        in_specs=[
            pl.BlockSpec((1, 1, 2, DIFF_V, tq), lambda hi, g: (cur(g)[0], hi, 0, 0, cur(g)[1])),
            pl.BlockSpec((1, 1, s, DIFF_V), lambda hi, g: (cur(g)[0], hi, 0, 0)),
            pl.BlockSpec((1, 1, DIFF_V, s), lambda hi, g: (prev(g)[0], hi, 0, 0)),
            pl.BlockSpec((1, 1, DIFF_V, s), lambda hi, g: (cur(g)[0], hi, 0, 0)),
            pl.BlockSpec((1, n_bias, 1, span), lambda hi, g: (hi, 0, 0, 0)),
            vec, vec, vec, vec,
            pl.BlockSpec((1, DIFF_V), lambda hi, g: (0, 0)),
        ],
        out_specs=pl.BlockSpec((1, tq, DIFF_V), lambda hi, g: (prev(g)[0], prev(g)[1], hi)),
        scratch_shapes=[pltpu.VMEM((s, tq), F32), pltpu.VMEM((s, tq), F32),
                        pltpu.VMEM((1, tq), F32), pltpu.VMEM((1, tq), F32),
                        pltpu.VMEM((s // ATTN_TK, 1, tq), F32), pltpu.VMEM((s // ATTN_TK, 1, tq), F32),
                        pltpu.VMEM((n_bias, ATTN_TK, tq), F32),
                        pltpu.VMEM((DIFF_V, tq), F32)],
        compiler_params=pltpu.CompilerParams(
            dimension_semantics=("arbitrary", "arbitrary"),
            vmem_limit_bytes=V7X_VMEM_LIMIT_BYTES),
        name="diff_attn",
    )(q_t, k, v_t, v_t, bias_rows, lq1, lk1, lq2, lk2, g_row)


def _t5_bucket(rel):
    half = N_BUCKETS // 2
    ret = jnp.where(rel > 0, half, 0)
    n = jnp.abs(rel)
    max_exact = half // 2
    large = max_exact + (jnp.log(jnp.maximum(n, 1).astype(jnp.float32) / max_exact)
                         / math.log(MAX_DISTANCE / max_exact)
                         * (half - max_exact)).astype(jnp.int32)
    large = jnp.minimum(large, half - 1)
    return ret + jnp.where(n < max_exact, n, large)


def _bias_rows(rel_bias, t):
    far = pl.cdiv(t - 1 + MAX_DISTANCE, t)
    offs = jnp.arange(-far, far + 1, dtype=jnp.int32) * t
    span = 2 * t
    m = jnp.arange(span, dtype=jnp.int32)
    q_minus_k = jnp.where(m < t, m, m - span)
    rel = offs[:, None] - q_minus_k[None, :]
    table = rel_bias.astype(F32) * LOG2E
    bucket = _t5_bucket(rel)[None]
    rows = jnp.zeros((table.shape[1],) + rel.shape, F32)
    for bkt in range(N_BUCKETS):
        rows = jnp.where(bucket == bkt, table[bkt][:, None, None], rows)
    return rows[:, :, None, :]


def _mlp_kernel(x_ref, om_ref, od_ref, wo1_ref, wo2_ref, wup_ref, wdn_ref, o_ref, h_ref):
    j = pl.program_id(1)

    @pl.when(j == 0)
    def _():
        for r in range(x_ref.shape[0] // MLP_ROWS):
            rows = slice(r * MLP_ROWS, (r + 1) * MLP_ROWS)
            x1 = (x_ref[rows, :] + _dot(om_ref[rows, :], wo1_ref[...])
                  + _dot(od_ref[rows, :], wo2_ref[...]))
            o_ref[rows, :] = x1
            h_ref[rows, :] = (x1 * lax.rsqrt(jnp.mean(x1 * x1, axis=-1, keepdims=True) + EPS)
                              ).astype(BF16)

    for c in range(wup_ref.shape[1] // MLP_FF_CHUNK):
        cols = slice(c * MLP_FF_CHUNK, (c + 1) * MLP_FF_CHUNK)
        up = _dot(h_ref[...], wup_ref[:, cols])
        act = jnp.square(jnp.maximum(up, 0.0)).astype(BF16)
        o_ref[...] += _dot(act, wdn_ref[cols, :])


def _out_proj_mlp(x2d, o_mla, o_diff, wo1, wo2, wup, wdn):
    n, d = x2d.shape
    t, f = MLP_TOKENS, MLP_FF
    return pl.pallas_call(
        _mlp_kernel,
        out_shape=jax.ShapeDtypeStruct((n, d), F32),
        grid=(n // t, D_FF // f),
        in_specs=[
            pl.BlockSpec((t, d), lambda i, j: (i, 0)),
            pl.BlockSpec((t, MLA_WIDTH), lambda i, j: (i, 0)),
            pl.BlockSpec((t, DIFF_WIDTH), lambda i, j: (i, 0)),
            pl.BlockSpec((MLA_WIDTH, d), lambda i, j: (0, 0)),
            pl.BlockSpec((DIFF_WIDTH, d), lambda i, j: (0, 0)),
            pl.BlockSpec((d, f), lambda i, j: (0, j)),
            pl.BlockSpec((f, d), lambda i, j: (j, 0)),
        ],
        out_specs=pl.BlockSpec((t, d), lambda i, j: (i, 0)),
        scratch_shapes=[pltpu.VMEM((t, d), BF16)],
        compiler_params=pltpu.CompilerParams(
            dimension_semantics=("arbitrary", "arbitrary"),
            vmem_limit_bytes=V7X_VMEM_LIMIT_BYTES),
        name="out_mlp",
    )(x2d, o_mla, o_diff, wo1, wo2, wup, wdn)


def _rope_tables_t(seq):
    inv = ROPE_THETA ** (-jnp.arange(0, MLA_ROPE, 2, dtype=jnp.float32) / MLA_ROPE)
    ang = jnp.arange(seq, dtype=jnp.float32)[:, None] * inv[None, :]
    ang = jnp.concatenate([ang, ang], axis=-1)
    return jnp.cos(ang).T, jnp.sin(ang).T


def kernel(x, attn_norm_w, w_in, q_a_norm_w, w_uq, kv_a_norm_w, w_ukv, mla_q_norm_w, mla_k_norm_w, diff_q_norm_w, diff_k_norm_w, lambda_q1, lambda_k1, lambda_q2, lambda_k2, diff_out_norm_w, w_out, mlp_norm_w, w_up, w_down, rel_bias):
    b, s, d = x.shape
    cos_t, sin_t = _rope_tables_t(s)
    bias_rows = _bias_rows(rel_bias, ATTN_TQ)
    for layer in range(DEPTH):
        lam_init = 0.8 - 0.6 * math.exp(-0.3 * layer)

        wi = w_in[layer].astype(F32) * attn_norm_w[layer].astype(F32)[:, None]
        c0, c1, c2, c3, c4 = (Q_LORA, Q_LORA + KV_LORA, Q_LORA + KV_LORA + MLA_ROPE,
                              Q_LORA + KV_LORA + MLA_ROPE + DIFF_QK_WIDTH,
                              Q_LORA + KV_LORA + MLA_ROPE + 2 * DIFF_QK_WIDTH)
        w_krope = wi[:, c1:c2]
        win_t = jnp.concatenate(
            [wi[:, :c1], w_krope, _rotate_half(w_krope), wi[:, c2:c3], wi[:, c3:c4], wi[:, c4:]],
            axis=1).T.astype(BF16)

        wq = (w_uq[layer].astype(F32) * q_a_norm_w[layer].astype(F32)[:, None]
              ).reshape(Q_LORA, MLA_HEADS, MLA_QK)
        wq_nope = wq[:, :, :MLA_NOPE].reshape(Q_LORA, -1)
        wq_rope = wq[:, :, MLA_NOPE:]
        wuq_t = jnp.concatenate(
            [wq_nope, wq_rope.reshape(Q_LORA, -1), _rotate_half(wq_rope).reshape(Q_LORA, -1)],
            axis=1).T.astype(BF16)

        wkv = (w_ukv[layer].astype(F32) * kv_a_norm_w[layer].astype(F32)[:, None]
               ).reshape(KV_LORA, MLA_HEADS, MLA_NOPE + MLA_V)
        wukv_t = jnp.concatenate(
            [wkv[:, :, :MLA_NOPE].reshape(KV_LORA, -1), wkv[:, :, MLA_NOPE:].reshape(KV_LORA, -1)],
            axis=1).T.astype(BF16)

        gk_row = jnp.concatenate(
            [mla_q_norm_w[layer].astype(F32) * mla_k_norm_w[layer].astype(F32),
             jnp.ones((MLA_QK_PAD - MLA_QK,), F32)])[None, :]
        gd = diff_q_norm_w[layer].astype(F32) * diff_k_norm_w[layer].astype(F32)
        gdk_row = jnp.concatenate([gd, gd])[None, :]

        qm_t, km, vm_t, qd_t, kd, vd_t = _projections(
            x, win_t, wuq_t, wukv_t, cos_t, sin_t, gk_row, gdk_row)

        o_mla = _mla_attention(qm_t, km, vm_t)
        o_diff = _diff_attention(
            qd_t, kd, vd_t, bias_rows,
            lambda_q1[layer].astype(F32)[None, :], lambda_k1[layer].astype(F32)[None, :],
            lambda_q2[layer].astype(F32)[None, :], lambda_k2[layer].astype(F32)[None, :],
            diff_out_norm_w[layer].astype(F32)[None, :], lam_init)

        wo = w_out[layer].astype(BF16)
        wup = (w_up[layer].astype(F32) * mlp_norm_w[layer].astype(F32)[:, None]).astype(BF16)
        x = _out_proj_mlp(
            x.reshape(b * s, d), o_mla.reshape(b * s, MLA_WIDTH), o_diff.reshape(b * s, DIFF_WIDTH),
            wo[:MLA_WIDTH], wo[MLA_WIDTH:], wup, w_down[layer].astype(BF16)).reshape(b, s, d)
    return x
```

```python
import functools
import math

import jax
import jax.numpy as jnp
from jax import lax
from jax.experimental import pallas as pl
from jax.experimental.pallas import tpu as pltpu

D_MODEL = 1024
DEPTH = 1
MLA_HEADS = 4
MLA_NOPE = 128
MLA_ROPE = 64
MLA_V = 128
Q_LORA = D_MODEL // 4
KV_LORA = D_MODEL // 8
MLA_QK = MLA_NOPE + MLA_ROPE
MLA_QK_PAD = 256
DIFF_HEADS = 4
DIFF_D = 64
DIFF_V = 2 * DIFF_D
DIFF_QK_WIDTH = DIFF_HEADS * 2 * DIFF_D
DIFF_WIDTH = DIFF_HEADS * DIFF_V
MLA_WIDTH = MLA_HEADS * MLA_V
D_FF = 4 * D_MODEL
N_BUCKETS = 32
MAX_DISTANCE = 128
ROPE_THETA = 10000.0
EPS = 1e-6
LOG2E = math.log2(math.e)

V7X_VMEM_LIMIT_BYTES = 56 * 1024 * 1024

PROJ_TOKENS = 1024
PROJ_SUB_TOKENS = 256
ATTN_TQ = 512
ATTN_TK = ATTN_TQ
MLA_TQ = 1024
MLP_TOKENS = 1024
MLP_FF = 2048
MLP_FF_CHUNK = 1024
MLP_ROWS = 256

F32 = jnp.float32
BF16 = jnp.bfloat16


def _dot(a, b):
    return jnp.dot(a, b, preferred_element_type=F32)


def _dot_nt(a, b):
    return lax.dot_general(a, b, (((1,), (1,)), ((), ())), preferred_element_type=F32)


def _rotate_half(x):
    x1, x2 = jnp.split(x, 2, axis=-1)
    return jnp.concatenate([-x2, x1], axis=-1)


def _proj_kernel(x_ref, win_ref, wuq_ref, wukv_ref, cos_ref, sin_ref, gk_ref, gdk_ref,
                 qm_ref, km_ref, vm_ref, qd_ref, kd_ref, vd_ref):
    ys = []
    for i in range(x_ref.shape[1] // PROJ_SUB_TOKENS):
        x = x_ref[0, i * PROJ_SUB_TOKENS:(i + 1) * PROJ_SUB_TOKENS]
        h = x * lax.rsqrt(jnp.mean(x * x, axis=-1, keepdims=True) + EPS)
        ys.append(_dot_nt(win_ref[...], h.astype(BF16)))
    for i, y in enumerate(ys):
        _proj_heads(y, slice(i * PROJ_SUB_TOKENS, (i + 1) * PROJ_SUB_TOKENS),
                    wuq_ref, wukv_ref, cos_ref, sin_ref, gk_ref, gdk_ref,
                    qm_ref, km_ref, vm_ref, qd_ref, kd_ref, vd_ref)


def _proj_heads(y, tok, wuq_ref, wukv_ref, cos_ref, sin_ref, gk_ref, gdk_ref,
                qm_ref, km_ref, vm_ref, qd_ref, kd_ref, vd_ref):
    t = y.shape[1]
    cos = cos_ref[:, tok]
    sin = sin_ref[:, tok]

    o0 = 0
    c_q = y[o0:o0 + Q_LORA]
    o0 += Q_LORA
    c_kv = y[o0:o0 + KV_LORA]
    o0 += KV_LORA
    k_rope = y[o0:o0 + MLA_ROPE]
    o0 += MLA_ROPE
    k_rope_rot = y[o0:o0 + MLA_ROPE]
    o0 += MLA_ROPE
    dq = y[o0:o0 + DIFF_QK_WIDTH]
    o0 += DIFF_QK_WIDTH
    dk = y[o0:o0 + DIFF_QK_WIDTH]
    o0 += DIFF_QK_WIDTH
    dv = y[o0:o0 + DIFF_WIDTH]

    cq_n = c_q * lax.rsqrt(jnp.mean(c_q * c_q, axis=0, keepdims=True) + EPS)
    q_all = _dot(wuq_ref[...], cq_n.astype(BF16))
    q_scale = (MLA_QK ** -0.5) * LOG2E
    zeros_pad = jnp.zeros((MLA_QK_PAD - MLA_QK, t), F32)
    for hd in range(MLA_HEADS):
        nope = q_all[hd * MLA_NOPE:(hd + 1) * MLA_NOPE]
        r0 = MLA_HEADS * MLA_NOPE + hd * MLA_ROPE
        r1 = MLA_HEADS * (MLA_NOPE + MLA_ROPE) + hd * MLA_ROPE
        rope = q_all[r0:r0 + MLA_ROPE] * cos + q_all[r1:r1 + MLA_ROPE] * sin
        ss = jnp.sum(nope * nope, axis=0, keepdims=True) + jnp.sum(rope * rope, axis=0, keepdims=True)
        r = lax.rsqrt(ss * (1.0 / MLA_QK) + EPS) * q_scale
        qm_ref[0, hd, 0:MLA_NOPE, tok] = (nope * r).astype(BF16)
        qm_ref[0, hd, MLA_NOPE:MLA_QK, tok] = (rope * r).astype(BF16)
        qm_ref[0, hd, MLA_QK:MLA_QK_PAD, tok] = zeros_pad.astype(BF16)

    ckv_n = c_kv * lax.rsqrt(jnp.mean(c_kv * c_kv, axis=0, keepdims=True) + EPS)
    kv = _dot(wukv_ref[...], ckv_n.astype(BF16))
    k_r = k_rope * cos + k_rope_rot * sin
    k_r_ss = jnp.sum(k_r * k_r, axis=0, keepdims=True)
    gk = gk_ref[...]
    for hd in range(MLA_HEADS):
        k_nope = kv[hd * MLA_NOPE:(hd + 1) * MLA_NOPE]
        ss = jnp.sum(k_nope * k_nope, axis=0, keepdims=True) + k_r_ss
        r = lax.rsqrt(ss * (1.0 / MLA_QK) + EPS)
        k_t = jnp.concatenate([k_nope * r, k_r * r, zeros_pad], axis=0)
        km_ref[0, hd, tok, :] = (k_t.T * gk).astype(BF16)
        v0 = MLA_HEADS * MLA_NOPE + hd * MLA_V
        vm_ref[0, hd, :, tok] = kv[v0:v0 + MLA_V].astype(BF16)

    d_scale = (DIFF_D ** -0.5) * LOG2E
    zeros_half = jnp.zeros((DIFF_D, t), BF16)
    gdk = gdk_ref[...]
    for hd in range(DIFF_HEADS):
        k_parts = []
        for mp in range(2):
            r0 = hd * DIFF_V + mp * DIFF_D
            qv = dq[r0:r0 + DIFF_D]
            qn = qv * (lax.rsqrt(jnp.mean(qv * qv, axis=0, keepdims=True) + EPS) * d_scale)
            qd_ref[0, hd, mp, mp * DIFF_D:(mp + 1) * DIFF_D, tok] = qn.astype(BF16)
            qd_ref[0, hd, mp, (1 - mp) * DIFF_D:(2 - mp) * DIFF_D, tok] = zeros_half
            kv_ = dk[r0:r0 + DIFF_D]
            k_parts.append(kv_ * lax.rsqrt(jnp.mean(kv_ * kv_, axis=0, keepdims=True) + EPS))
        k_t = jnp.concatenate(k_parts, axis=0)
        kd_ref[0, hd, tok, :] = (k_t.T * gdk).astype(BF16)
        vd_ref[0, hd, :, tok] = dv[hd * DIFF_V:(hd + 1) * DIFF_V].astype(BF16)


def _projections(x, win_t, wuq_t, wukv_t, cos_t, sin_t, gk_row, gdk_row):
    b, s, d = x.shape
    t = PROJ_TOKENS
    n_in = win_t.shape[0]
    const = lambda shape: pl.BlockSpec(shape, lambda bi, si: (0,) * len(shape))
    out_shape = (
        jax.ShapeDtypeStruct((b, MLA_HEADS, MLA_QK_PAD, s), BF16),
        jax.ShapeDtypeStruct((b, MLA_HEADS, s, MLA_QK_PAD), BF16),
        jax.ShapeDtypeStruct((b, MLA_HEADS, MLA_V, s), BF16),
        jax.ShapeDtypeStruct((b, DIFF_HEADS, 2, DIFF_V, s), BF16),
        jax.ShapeDtypeStruct((b, DIFF_HEADS, s, DIFF_V), BF16),
        jax.ShapeDtypeStruct((b, DIFF_HEADS, DIFF_V, s), BF16),
    )
    out_specs = (
        pl.BlockSpec((1, MLA_HEADS, MLA_QK_PAD, t), lambda bi, si: (bi, 0, 0, si)),
        pl.BlockSpec((1, MLA_HEADS, t, MLA_QK_PAD), lambda bi, si: (bi, 0, si, 0)),
        pl.BlockSpec((1, MLA_HEADS, MLA_V, t), lambda bi, si: (bi, 0, 0, si)),
        pl.BlockSpec((1, DIFF_HEADS, 2, DIFF_V, t), lambda bi, si: (bi, 0, 0, 0, si)),
        pl.BlockSpec((1, DIFF_HEADS, t, DIFF_V), lambda bi, si: (bi, 0, si, 0)),
        pl.BlockSpec((1, DIFF_HEADS, DIFF_V, t), lambda bi, si: (bi, 0, 0, si)),
    )
    in_specs = [
        pl.BlockSpec((1, t, d), lambda bi, si: (bi, si, 0)),
        const((n_in, d)),
        const(wuq_t.shape),
        const(wukv_t.shape),
        pl.BlockSpec((MLA_ROPE, t), lambda bi, si: (0, si)),
        pl.BlockSpec((MLA_ROPE, t), lambda bi, si: (0, si)),
        const(gk_row.shape),
        const(gdk_row.shape),
    ]
    return pl.pallas_call(
        _proj_kernel,
        out_shape=out_shape,
        grid=(b, s // t),
        in_specs=in_specs,
        out_specs=out_specs,
        compiler_params=pltpu.CompilerParams(
            dimension_semantics=("arbitrary", "arbitrary"),
            vmem_limit_bytes=V7X_VMEM_LIMIT_BYTES),
        name="proj",
    )(x, win_t, wuq_t, wukv_t, cos_t, sin_t, gk_row, gdk_row)


def _attn_sweeps(k_ref, q_t, v_ref, score_bufs, prev_bufs, tk, bias=None, read_base=None):
    s_w, m_w, c_w = score_bufs
    s_r, m_r, c_r = prev_bufs
    n_keys, tq = s_w.shape
    n_chunks = n_keys // tk
    m_prev = m_r[...]
    m_new = None
    l = jnp.zeros((1, tq), F32)
    acc = jnp.zeros((v_ref.shape[2], tq), F32)
    if bias is not None:
        bias_ref, qi = bias
        n_bias = bias_ref.shape[0]
        far = (n_bias - 1) // 2
        n_near = n_bias - 2
        first_near = jnp.clip(qi - n_near // 2, 0, n_chunks - n_near)
        c_left = bias_ref[0, 0:1, :]
        c_right = bias_ref[n_bias - 1, 0:1, :]
    for j in range(n_chunks):
        if bias is None:
            rows_w = slice(j * tk, (j + 1) * tk)
            s = _dot(k_ref[0, 0, rows_w, :], q_t)
            m_c = jnp.max(s, axis=0, keepdims=True)
        else:
            c = first_near + j if j < n_near else lax.rem(first_near + j, n_chunks)
            rows_w = pl.ds(pl.multiple_of(c * tk, tk), tk)
            s = _dot(k_ref[0, 0, rows_w, :], q_t)
            if j < n_near:
                s = s + bias_ref[jnp.clip(c - qi, -far, far) + far]
                m_c = jnp.max(s, axis=0, keepdims=True)
                c_w[c] = jnp.zeros((1, tq), F32)
            else:
                off = jnp.where(c > qi, c_right, c_left)
                m_c = jnp.max(s, axis=0, keepdims=True) + off
                c_w[c] = off
        s_w[rows_w, :] = s
        m_new = m_c if m_new is None else jnp.maximum(m_new, m_c)

        if read_base is None:
            rows_r = slice(j * tk, (j + 1) * tk)
        else:
            rows_r = pl.ds(pl.multiple_of(read_base + j * tk, tk), tk)
        shift = m_prev if c_r is None else m_prev - c_r[j]
        p = jnp.exp2(s_r[rows_r, :] - shift)
        l = l + jnp.sum(p, axis=0, keepdims=True)
        acc = acc + _dot(v_ref[0, 0, :, rows_r], p.astype(BF16))
    m_w[...] = m_new
    return acc, l


def _mla_attn_kernel(q_ref, k_ref, vp_ref, vc_ref, o_ref, s_a, s_b, m_a, m_b, acc_d, l_d):
    tq = s_a.shape[1]

    @pl.when(pl.program_id(0) == 0)
    def _():
        s_b[...] = jnp.zeros_like(s_b)
        m_b[...] = jnp.zeros_like(m_b)
        acc_d[...] = jnp.zeros_like(acc_d)
        l_d[...] = jnp.ones_like(l_d)

    def store(rows, acc, l):
        o_ref[0, rows, :] = (acc * (1.0 / l)).astype(o_ref.dtype).T

    store(slice(0, tq), acc_d[...], l_d[...])
    acc, l = _attn_sweeps(k_ref, q_ref[0, 0, :, 0:tq], vp_ref, (s_a, m_a, None), (s_b, m_b, None),
                          ATTN_TK)
    store(slice(tq, 2 * tq), acc, l)
    acc, l = _attn_sweeps(k_ref, q_ref[0, 0, :, tq:2 * tq], vc_ref, (s_b, m_b, None), (s_a, m_a, None),
                          ATTN_TK, read_base=jnp.minimum(pl.program_id(0), 0))
    acc_d[...] = acc
    l_d[...] = l


def _mla_attention(q_t, k, v_t):
    b, h, _, s = q_t.shape
    tq = MLA_TQ
    n_qp = s // (2 * tq)
    n_pairs = b * h * n_qp

    def coords(p):
        return p // (h * n_qp), (p // n_qp) % h, p % n_qp

    def prev(g):
        return coords(jnp.maximum(g - 1, 0))

    def cur(g):
        return coords(jnp.minimum(g, n_pairs - 1))

    return pl.pallas_call(
        _mla_attn_kernel,
        out_shape=jax.ShapeDtypeStruct((b, s, h * MLA_V), BF16),
        grid=(n_pairs + 1,),
        in_specs=[
            pl.BlockSpec((1, 1, MLA_QK_PAD, 2 * tq), lambda g: (cur(g)[0], cur(g)[1], 0, cur(g)[2])),
            pl.BlockSpec((1, 1, s, MLA_QK_PAD), lambda g: (cur(g)[0], cur(g)[1], 0, 0)),
            pl.BlockSpec((1, 1, MLA_V, s), lambda g: (prev(g)[0], prev(g)[1], 0, 0)),
            pl.BlockSpec((1, 1, MLA_V, s), lambda g: (cur(g)[0], cur(g)[1], 0, 0)),
        ],
        out_specs=pl.BlockSpec((1, 2 * tq, MLA_V), lambda g: (prev(g)[0], prev(g)[2], prev(g)[1])),
        scratch_shapes=[pltpu.VMEM((s, tq), F32), pltpu.VMEM((s, tq), F32),
                        pltpu.VMEM((1, tq), F32), pltpu.VMEM((1, tq), F32),
                        pltpu.VMEM((MLA_V, tq), F32), pltpu.VMEM((1, tq), F32)],
        compiler_params=pltpu.CompilerParams(
            dimension_semantics=("arbitrary",),
            vmem_limit_bytes=V7X_VMEM_LIMIT_BYTES),
        name="mla_attn",
    )(q_t, k, v_t, v_t)


def _diff_attn_kernel(q_ref, k_ref, vp_ref, vc_ref, brow_ref,
                      lq1_ref, lk1_ref, lq2_ref, lk2_ref, g_ref, o_ref,
                      s_a, s_b, m_a, m_b, c_a, c_b, bias_ref, o1_d, *, n_pairs, n_q, lam_init):
    g = pl.program_id(1)

    @pl.when(g == 0)
    def _():
        s_b[...] = jnp.zeros_like(s_b)
        m_b[...] = jnp.zeros_like(m_b)
        c_b[...] = jnp.zeros_like(c_b)
        o1_d[...] = jnp.zeros_like(o1_d)
        n_bias, tk, tq = bias_ref.shape
        for j in range(n_bias):
            rows = jnp.broadcast_to(brow_ref[0, j], (tk, brow_ref.shape[-1]))
            bias_ref[j] = pltpu.roll(rows, 0, 1, stride=1, stride_axis=0)[:, :tq]

    qi = lax.rem(jnp.minimum(g, n_pairs - 1), n_q)
    lam = (jnp.exp(jnp.sum(lq1_ref[...] * lk1_ref[...], axis=-1, keepdims=True))
           - jnp.exp(jnp.sum(lq2_ref[...] * lk2_ref[...], axis=-1, keepdims=True))
           + lam_init)
    a2, l2 = _attn_sweeps(k_ref, q_ref[0, 0, 0], vp_ref, (s_a, m_a, c_a), (s_b, m_b, c_b),
                          ATTN_TK, bias=(bias_ref, qi))
    o = o1_d[...] - lam * (a2 * (1.0 / l2))
    o = o * lax.rsqrt(jnp.mean(o * o, axis=0, keepdims=True) + EPS)
    o_ref[0] = (o.T * (g_ref[...] * (1.0 - lam_init))).astype(o_ref.dtype)
    a1, l1 = _attn_sweeps(k_ref, q_ref[0, 0, 1], vc_ref, (s_b, m_b, c_b), (s_a, m_a, c_a),
                          ATTN_TK, bias=(bias_ref, qi))
    o1_d[...] = a1 * (1.0 / l1)


def _diff_attention(q_t, k, v_t, bias_rows, lq1, lk1, lq2, lk2, g_row, lam_init):
    b, h, _, _, s = q_t.shape
    tq = ATTN_TQ
    n_q = s // tq
    n_pairs = b * n_q
    n_bias, _, span = bias_rows.shape[1:]
    assert s // ATTN_TK >= n_bias - 2
    kern = functools.partial(_diff_attn_kernel, n_pairs=n_pairs, n_q=n_q, lam_init=lam_init)

    def prev(g):
        p = jnp.maximum(g - 1, 0)
        return p // n_q, p % n_q

    def cur(g):
        p = jnp.minimum(g, n_pairs - 1)
        return p // n_q, p % n_q

    vec = pl.BlockSpec((1, DIFF_D), lambda hi, g: (0, 0))
    return pl.pallas_call(
        kern,
        out_shape=jax.ShapeDtypeStruct((b, s, h * DIFF_V), BF16),
        grid=(h, n_pairs + 1),
        in_specs=[
            pl.BlockSpec((1, 1, 2, DIFF_V, tq), lambda hi, g: (cur(g)[0], hi, 0, 0, cur(g)[1])),
            pl.BlockSpec((1, 1, s, DIFF_V), lambda hi, g: (cur(g)[0], hi, 0, 0)),
            pl.BlockSpec((1, 1, DIFF_V, s), lambda hi, g: (prev(g)[0], hi, 0, 0)),
            pl.BlockSpec((1, 1, DIFF_V, s), lambda hi, g: (cur(g)[0], hi, 0, 0)),
            pl.BlockSpec((1, n_bias, 1, span), lambda hi, g: (hi, 0, 0, 0)),
            vec, vec, vec, vec,
            pl.BlockSpec((1, DIFF_V), lambda hi, g: (0, 0)),
        ],
        out_specs=pl.BlockSpec((1, tq, DIFF_V), lambda hi, g: (prev(g)[0], prev(g)[1], hi)),
        scratch_shapes=[pltpu.VMEM((s, tq), F32), pltpu.VMEM((s, tq), F32),
                        pltpu.VMEM((1, tq), F32), pltpu.VMEM((1, tq), F32),
                        pltpu.VMEM((s // ATTN_TK, 1, tq), F32), pltpu.VMEM((s // ATTN_TK, 1, tq), F32),
                        pltpu.VMEM((n_bias, ATTN_TK, tq), F32),
                        pltpu.VMEM((DIFF_V, tq), F32)],
        compiler_params=pltpu.CompilerParams(
            dimension_semantics=("arbitrary", "arbitrary"),
            vmem_limit_bytes=V7X_VMEM_LIMIT_BYTES),
        name="diff_attn",
    )(q_t, k, v_t, v_t, bias_rows, lq1, lk1, lq2, lk2, g_row)


def _t5_bucket(rel):
    half = N_BUCKETS // 2
    ret = jnp.where(rel > 0, half, 0)
    n = jnp.abs(rel)
    max_exact = half // 2
    large = max_exact + (jnp.log(jnp.maximum(n, 1).astype(jnp.float32) / max_exact)
                         / math.log(MAX_DISTANCE / max_exact)
                         * (half - max_exact)).astype(jnp.int32)
    large = jnp.minimum(large, half - 1)
    return ret + jnp.where(n < max_exact, n, large)


def _bias_rows(rel_bias, t):
    far = pl.cdiv(t - 1 + MAX_DISTANCE, t)
    offs = jnp.arange(-far, far + 1, dtype=jnp.int32) * t
    span = 2 * t
    m = jnp.arange(span, dtype=jnp.int32)
    q_minus_k = jnp.where(m < t, m, m - span)
    rel = offs[:, None] - q_minus_k[None, :]
    table = rel_bias.astype(F32) * LOG2E
    bucket = _t5_bucket(rel)[None]
    rows = jnp.zeros((table.shape[1],) + rel.shape, F32)
    for bkt in range(N_BUCKETS):
        rows = jnp.where(bucket == bkt, table[bkt][:, None, None], rows)
    return rows[:, :, None, :]


def _mlp_kernel(x_ref, om_ref, od_ref, wo1_ref, wo2_ref, wup_ref, wdn_ref, o_ref, h_ref):
    j = pl.program_id(1)

    @pl.when(j == 0)
    def _():
        for r in range(x_ref.shape[0] // MLP_ROWS):
            rows = slice(r * MLP_ROWS, (r + 1) * MLP_ROWS)
            x1 = (x_ref[rows, :] + _dot(om_ref[rows, :], wo1_ref[...])
                  + _dot(od_ref[rows, :], wo2_ref[...]))
            o_ref[rows, :] = x1
            h_ref[rows, :] = (x1 * lax.rsqrt(jnp.mean(x1 * x1, axis=-1, keepdims=True) + EPS)
                              ).astype(BF16)

    for c in range(wup_ref.shape[1] // MLP_FF_CHUNK):
        cols = slice(c * MLP_FF_CHUNK, (c + 1) * MLP_FF_CHUNK)
        up = _dot(h_ref[...], wup_ref[:, cols])
        act = jnp.square(jnp.maximum(up, 0.0)).astype(BF16)
        o_ref[...] += _dot(act, wdn_ref[cols, :])


def _out_proj_mlp(x2d, o_mla, o_diff, wo1, wo2, wup, wdn):
    n, d = x2d.shape
    t, f = MLP_TOKENS, MLP_FF
    return pl.pallas_call(
        _mlp_kernel,
        out_shape=jax.ShapeDtypeStruct((n, d), F32),
        grid=(n // t, D_FF // f),
        in_specs=[
            pl.BlockSpec((t, d), lambda i, j: (i, 0)),
            pl.BlockSpec((t, MLA_WIDTH), lambda i, j: (i, 0)),
            pl.BlockSpec((t, DIFF_WIDTH), lambda i, j: (i, 0)),
            pl.BlockSpec((MLA_WIDTH, d), lambda i, j: (0, 0)),
            pl.BlockSpec((DIFF_WIDTH, d), lambda i, j: (0, 0)),
            pl.BlockSpec((d, f), lambda i, j: (0, j)),
            pl.BlockSpec((f, d), lambda i, j: (j, 0)),
        ],
        out_specs=pl.BlockSpec((t, d), lambda i, j: (i, 0)),
        scratch_shapes=[pltpu.VMEM((t, d), BF16)],
        compiler_params=pltpu.CompilerParams(
            dimension_semantics=("arbitrary", "arbitrary"),
            vmem_limit_bytes=V7X_VMEM_LIMIT_BYTES),
        name="out_mlp",
    )(x2d, o_mla, o_diff, wo1, wo2, wup, wdn)


def _rope_tables_t(seq):
    inv = ROPE_THETA ** (-jnp.arange(0, MLA_ROPE, 2, dtype=jnp.float32) / MLA_ROPE)
    ang = jnp.arange(seq, dtype=jnp.float32)[:, None] * inv[None, :]
    ang = jnp.concatenate([ang, ang], axis=-1)
    return jnp.cos(ang).T, jnp.sin(ang).T


def kernel(x, attn_norm_w, w_in, q_a_norm_w, w_uq, kv_a_norm_w, w_ukv, mla_q_norm_w, mla_k_norm_w, diff_q_norm_w, diff_k_norm_w, lambda_q1, lambda_k1, lambda_q2, lambda_k2, diff_out_norm_w, w_out, mlp_norm_w, w_up, w_down, rel_bias):
    b, s, d = x.shape
    cos_t, sin_t = _rope_tables_t(s)
    bias_rows = _bias_rows(rel_bias, ATTN_TQ)
    for layer in range(DEPTH):
        lam_init = 0.8 - 0.6 * math.exp(-0.3 * layer)

        wi = w_in[layer].astype(F32) * attn_norm_w[layer].astype(F32)[:, None]
        c0, c1, c2, c3, c4 = (Q_LORA, Q_LORA + KV_LORA, Q_LORA + KV_LORA + MLA_ROPE,
                              Q_LORA + KV_LORA + MLA_ROPE + DIFF_QK_WIDTH,
                              Q_LORA + KV_LORA + MLA_ROPE + 2 * DIFF_QK_WIDTH)
        w_krope = wi[:, c1:c2]
        win_t = jnp.concatenate(
            [wi[:, :c1], w_krope, _rotate_half(w_krope), wi[:, c2:c3], wi[:, c3:c4], wi[:, c4:]],
            axis=1).T.astype(BF16)

        wq = (w_uq[layer].astype(F32) * q_a_norm_w[layer].astype(F32)[:, None]
              ).reshape(Q_LORA, MLA_HEADS, MLA_QK)
        wq_nope = wq[:, :, :MLA_NOPE].reshape(Q_LORA, -1)
        wq_rope = wq[:, :, MLA_NOPE:]
        wuq_t = jnp.concatenate(
            [wq_nope, wq_rope.reshape(Q_LORA, -1), _rotate_half(wq_rope).reshape(Q_LORA, -1)],
            axis=1).T.astype(BF16)

        wkv = (w_ukv[layer].astype(F32) * kv_a_norm_w[layer].astype(F32)[:, None]
               ).reshape(KV_LORA, MLA_HEADS, MLA_NOPE + MLA_V)
        wukv_t = jnp.concatenate(
            [wkv[:, :, :MLA_NOPE].reshape(KV_LORA, -1), wkv[:, :, MLA_NOPE:].reshape(KV_LORA, -1)],
            axis=1).T.astype(BF16)

        gk_row = jnp.concatenate(
            [mla_q_norm_w[layer].astype(F32) * mla_k_norm_w[layer].astype(F32),
             jnp.ones((MLA_QK_PAD - MLA_QK,), F32)])[None, :]
        gd = diff_q_norm_w[layer].astype(F32) * diff_k_norm_w[layer].astype(F32)
        gdk_row = jnp.concatenate([gd, gd])[None, :]

        qm_t, km, vm_t, qd_t, kd, vd_t = _projections(
            x, win_t, wuq_t, wukv_t, cos_t, sin_t, gk_row, gdk_row)

        o_mla = _mla_attention(qm_t, km, vm_t)
        o_diff = _diff_attention(
            qd_t, kd, vd_t, bias_rows,
            lambda_q1[layer].astype(F32)[None, :], lambda_k1[layer].astype(F32)[None, :],
            lambda_q2[layer].astype(F32)[None, :], lambda_k2[layer].astype(F32)[None, :],
            diff_out_norm_w[layer].astype(F32)[None, :], lam_init)

        wo = w_out[layer].astype(BF16)
        wup = (w_up[layer].astype(F32) * mlp_norm_w[layer].astype(F32)[:, None]).astype(BF16)
        x = _out_proj_mlp(
            x.reshape(b * s, d), o_mla.reshape(b * s, MLA_WIDTH), o_diff.reshape(b * s, DIFF_WIDTH),
            wo[:MLA_WIDTH], wo[MLA_WIDTH:], wup, w_down[layer].astype(BF16)).reshape(b, s, d)
    return x
```

```python
import functools
import math

import jax
import jax.numpy as jnp
from jax import lax
from jax.experimental import pallas as pl
from jax.experimental.pallas import tpu as pltpu

D_MODEL = 1024
DEPTH = 1
MLA_HEADS = 4
MLA_NOPE = 128
MLA_ROPE = 64
MLA_V = 128
Q_LORA = D_MODEL // 4
KV_LORA = D_MODEL // 8
MLA_QK = MLA_NOPE + MLA_ROPE
MLA_QK_PAD = 256
DIFF_HEADS = 4
DIFF_D = 64
DIFF_V = 2 * DIFF_D
DIFF_QK_WIDTH = DIFF_HEADS * 2 * DIFF_D
DIFF_WIDTH = DIFF_HEADS * DIFF_V
MLA_WIDTH = MLA_HEADS * MLA_V
D_FF = 4 * D_MODEL
N_BUCKETS = 32
MAX_DISTANCE = 128
ROPE_THETA = 10000.0
EPS = 1e-6
LOG2E = math.log2(math.e)

V7X_VMEM_LIMIT_BYTES = 56 * 1024 * 1024

PROJ_TOKENS = 1024
PROJ_SUB_TOKENS = 256
ATTN_TQ = 512
ATTN_TK = ATTN_TQ
MLP_TOKENS = 1024
MLP_FF_CHUNK = 1024
MLP_ROWS = 256

F32 = jnp.float32
BF16 = jnp.bfloat16


def _dot(a, b):
    return jnp.dot(a, b, preferred_element_type=F32)


def _dot_nt(a, b):
    return lax.dot_general(a, b, (((1,), (1,)), ((), ())), preferred_element_type=F32)


def _rotate_half(x):
    x1, x2 = jnp.split(x, 2, axis=-1)
    return jnp.concatenate([-x2, x1], axis=-1)


def _proj_kernel(x_ref, win_ref, wuq_ref, wukv_ref, cos_ref, sin_ref, gk_ref, gdk_ref,
                 qm_ref, km_ref, vm_ref, qd_ref, kd_ref, vd_ref):
    ys = []
    for i in range(x_ref.shape[1] // PROJ_SUB_TOKENS):
        x = x_ref[0, i * PROJ_SUB_TOKENS:(i + 1) * PROJ_SUB_TOKENS]
        h = x * lax.rsqrt(jnp.mean(x * x, axis=-1, keepdims=True) + EPS)
        ys.append(_dot_nt(win_ref[...], h.astype(BF16)))
    for i, y in enumerate(ys):
        _proj_heads(y, slice(i * PROJ_SUB_TOKENS, (i + 1) * PROJ_SUB_TOKENS),
                    wuq_ref, wukv_ref, cos_ref, sin_ref, gk_ref, gdk_ref,
                    qm_ref, km_ref, vm_ref, qd_ref, kd_ref, vd_ref)


def _proj_heads(y, tok, wuq_ref, wukv_ref, cos_ref, sin_ref, gk_ref, gdk_ref,
                qm_ref, km_ref, vm_ref, qd_ref, kd_ref, vd_ref):
    t = y.shape[1]
    cos = cos_ref[:, tok]
    sin = sin_ref[:, tok]

    o0 = 0
    c_q = y[o0:o0 + Q_LORA]
    o0 += Q_LORA
    c_kv = y[o0:o0 + KV_LORA]
    o0 += KV_LORA
    k_rope = y[o0:o0 + MLA_ROPE]
    o0 += MLA_ROPE
    k_rope_rot = y[o0:o0 + MLA_ROPE]
    o0 += MLA_ROPE
    dq = y[o0:o0 + DIFF_QK_WIDTH]
    o0 += DIFF_QK_WIDTH
    dk = y[o0:o0 + DIFF_QK_WIDTH]
    o0 += DIFF_QK_WIDTH
    dv = y[o0:o0 + DIFF_WIDTH]

    cq_n = c_q * lax.rsqrt(jnp.mean(c_q * c_q, axis=0, keepdims=True) + EPS)
    q_all = _dot(wuq_ref[...], cq_n.astype(BF16))
    q_scale = (MLA_QK ** -0.5) * LOG2E
    zeros_pad = jnp.zeros((MLA_QK_PAD - MLA_QK, t), F32)
    for hd in range(MLA_HEADS):
        nope = q_all[hd * MLA_NOPE:(hd + 1) * MLA_NOPE]
        r0 = MLA_HEADS * MLA_NOPE + hd * MLA_ROPE
        r1 = MLA_HEADS * (MLA_NOPE + MLA_ROPE) + hd * MLA_ROPE
        rope = q_all[r0:r0 + MLA_ROPE] * cos + q_all[r1:r1 + MLA_ROPE] * sin
        ss = jnp.sum(nope * nope, axis=0, keepdims=True) + jnp.sum(rope * rope, axis=0, keepdims=True)
        r = lax.rsqrt(ss * (1.0 / MLA_QK) + EPS) * q_scale
        qm_ref[0, hd, 0:MLA_NOPE, tok] = (nope * r).astype(BF16)
        qm_ref[0, hd, MLA_NOPE:MLA_QK, tok] = (rope * r).astype(BF16)
        qm_ref[0, hd, MLA_QK:MLA_QK_PAD, tok] = zeros_pad.astype(BF16)

    ckv_n = c_kv * lax.rsqrt(jnp.mean(c_kv * c_kv, axis=0, keepdims=True) + EPS)
    kv = _dot(wukv_ref[...], ckv_n.astype(BF16))
    k_r = k_rope * cos + k_rope_rot * sin
    k_r_ss = jnp.sum(k_r * k_r, axis=0, keepdims=True)
    gk = gk_ref[...]
    for hd in range(MLA_HEADS):
        k_nope = kv[hd * MLA_NOPE:(hd + 1) * MLA_NOPE]
        ss = jnp.sum(k_nope * k_nope, axis=0, keepdims=True) + k_r_ss
        r = lax.rsqrt(ss * (1.0 / MLA_QK) + EPS)
        k_t = jnp.concatenate([k_nope * r, k_r * r, zeros_pad], axis=0)
        km_ref[0, hd, tok, :] = (k_t.T * gk).astype(BF16)
        v0 = MLA_HEADS * MLA_NOPE + hd * MLA_V
        vm_ref[0, hd, :, tok] = kv[v0:v0 + MLA_V].astype(BF16)

    d_scale = (DIFF_D ** -0.5) * LOG2E
    zeros_half = jnp.zeros((DIFF_D, t), BF16)
    gdk = gdk_ref[...]
    for hd in range(DIFF_HEADS):
        k_parts = []
        for mp in range(2):
            r0 = hd * DIFF_V + mp * DIFF_D
            qv = dq[r0:r0 + DIFF_D]
            qn = qv * (lax.rsqrt(jnp.mean(qv * qv, axis=0, keepdims=True) + EPS) * d_scale)
            qd_ref[0, hd, mp, mp * DIFF_D:(mp + 1) * DIFF_D, tok] = qn.astype(BF16)
            qd_ref[0, hd, mp, (1 - mp) * DIFF_D:(2 - mp) * DIFF_D, tok] = zeros_half
            kv_ = dk[r0:r0 + DIFF_D]
            k_parts.append(kv_ * lax.rsqrt(jnp.mean(kv_ * kv_, axis=0, keepdims=True) + EPS))
        k_t = jnp.concatenate(k_parts, axis=0)
        kd_ref[0, hd, tok, :] = (k_t.T * gdk).astype(BF16)
        vd_ref[0, hd, :, tok] = dv[hd * DIFF_V:(hd + 1) * DIFF_V].astype(BF16)


def _projections(x, win_t, wuq_t, wukv_t, cos_t, sin_t, gk_row, gdk_row):
    b, s, d = x.shape
    t = PROJ_TOKENS
    n_in = win_t.shape[0]
    const = lambda shape: pl.BlockSpec(shape, lambda bi, si: (0,) * len(shape))
    out_shape = (
        jax.ShapeDtypeStruct((b, MLA_HEADS, MLA_QK_PAD, s), BF16),
        jax.ShapeDtypeStruct((b, MLA_HEADS, s, MLA_QK_PAD), BF16),
        jax.ShapeDtypeStruct((b, MLA_HEADS, MLA_V, s), BF16),
        jax.ShapeDtypeStruct((b, DIFF_HEADS, 2, DIFF_V, s), BF16),
        jax.ShapeDtypeStruct((b, DIFF_HEADS, s, DIFF_V), BF16),
        jax.ShapeDtypeStruct((b, DIFF_HEADS, DIFF_V, s), BF16),
    )
    out_specs = (
        pl.BlockSpec((1, MLA_HEADS, MLA_QK_PAD, t), lambda bi, si: (bi, 0, 0, si)),
        pl.BlockSpec((1, MLA_HEADS, t, MLA_QK_PAD), lambda bi, si: (bi, 0, si, 0)),
        pl.BlockSpec((1, MLA_HEADS, MLA_V, t), lambda bi, si: (bi, 0, 0, si)),
        pl.BlockSpec((1, DIFF_HEADS, 2, DIFF_V, t), lambda bi, si: (bi, 0, 0, 0, si)),
        pl.BlockSpec((1, DIFF_HEADS, t, DIFF_V), lambda bi, si: (bi, 0, si, 0)),
        pl.BlockSpec((1, DIFF_HEADS, DIFF_V, t), lambda bi, si: (bi, 0, 0, si)),
    )
    in_specs = [
        pl.BlockSpec((1, t, d), lambda bi, si: (bi, si, 0)),
        const((n_in, d)),
        const(wuq_t.shape),
        const(wukv_t.shape),
        pl.BlockSpec((MLA_ROPE, t), lambda bi, si: (0, si)),
        pl.BlockSpec((MLA_ROPE, t), lambda bi, si: (0, si)),
        const(gk_row.shape),
        const(gdk_row.shape),
    ]
    return pl.pallas_call(
        _proj_kernel,
        out_shape=out_shape,
        grid=(b, s // t),
        in_specs=in_specs,
        out_specs=out_specs,
        compiler_params=pltpu.CompilerParams(
            dimension_semantics=("arbitrary", "arbitrary"),
            vmem_limit_bytes=V7X_VMEM_LIMIT_BYTES),
        name="proj",
    )(x, win_t, wuq_t, wukv_t, cos_t, sin_t, gk_row, gdk_row)


def _attn_sweeps(k_ref, q_t, v_ref, score_bufs, prev_bufs, tk, bias=None, read_base=None):
    s_w, m_w, c_w = score_bufs
    s_r, m_r, c_r = prev_bufs
    n_keys, tq = s_w.shape
    n_chunks = n_keys // tk
    m_prev = m_r[...]
    m_new = None
    l = jnp.zeros((1, tq), F32)
    acc = jnp.zeros((v_ref.shape[2], tq), F32)
    if bias is not None:
        bias_ref, qi = bias
        n_bias = bias_ref.shape[0]
        far = (n_bias - 1) // 2
        n_near = n_bias - 2
        first_near = jnp.clip(qi - n_near // 2, 0, n_chunks - n_near)
        c_left = bias_ref[0, 0:1, :]
        c_right = bias_ref[n_bias - 1, 0:1, :]
    for j in range(n_chunks):
        if bias is None:
            rows_w = slice(j * tk, (j + 1) * tk)
            s = _dot(k_ref[0, 0, rows_w, :], q_t)
            m_c = jnp.max(s, axis=0, keepdims=True)
        else:
            c = first_near + j if j < n_near else lax.rem(first_near + j, n_chunks)
            rows_w = pl.ds(pl.multiple_of(c * tk, tk), tk)
            s = _dot(k_ref[0, 0, rows_w, :], q_t)
            if j < n_near:
                s = s + bias_ref[jnp.clip(c - qi, -far, far) + far]
                m_c = jnp.max(s, axis=0, keepdims=True)
                c_w[c] = jnp.zeros((1, tq), F32)
            else:
                off = jnp.where(c > qi, c_right, c_left)
                m_c = jnp.max(s, axis=0, keepdims=True) + off
                c_w[c] = off
        s_w[rows_w, :] = s
        m_new = m_c if m_new is None else jnp.maximum(m_new, m_c)

        if read_base is None:
            rows_r = slice(j * tk, (j + 1) * tk)
        else:
            rows_r = pl.ds(pl.multiple_of(read_base + j * tk, tk), tk)
        shift = m_prev if c_r is None else m_prev - c_r[j]
        p = jnp.exp2(s_r[rows_r, :] - shift)
        l = l + jnp.sum(p, axis=0, keepdims=True)
        acc = acc + _dot(v_ref[0, 0, :, rows_r], p.astype(BF16))
    m_w[...] = m_new
    return acc, l


def _mla_attn_kernel(q_ref, k_ref, vp_ref, vc_ref, o_ref, s_a, s_b, m_a, m_b, acc_d, l_d):
    tq = s_a.shape[1]

    @pl.when(pl.program_id(0) == 0)
    def _():
        s_b[...] = jnp.zeros_like(s_b)
        m_b[...] = jnp.zeros_like(m_b)
        acc_d[...] = jnp.zeros_like(acc_d)
        l_d[...] = jnp.ones_like(l_d)

    def store(rows, acc, l):
        o_ref[0, rows, :] = (acc * (1.0 / l)).astype(o_ref.dtype).T

    store(slice(0, tq), acc_d[...], l_d[...])
    acc, l = _attn_sweeps(k_ref, q_ref[0, 0, :, 0:tq], vp_ref, (s_a, m_a, None), (s_b, m_b, None),
                          ATTN_TK)
    store(slice(tq, 2 * tq), acc, l)
    acc, l = _attn_sweeps(k_ref, q_ref[0, 0, :, tq:2 * tq], vc_ref, (s_b, m_b, None), (s_a, m_a, None),
                          ATTN_TK, read_base=jnp.minimum(pl.program_id(0), 0))
    acc_d[...] = acc
    l_d[...] = l


def _mla_attention(q_t, k, v_t):
    b, h, _, s = q_t.shape
    tq = ATTN_TQ
    n_qp = s // (2 * tq)
    n_pairs = b * h * n_qp

    def coords(p):
        return p // (h * n_qp), (p // n_qp) % h, p % n_qp

    def prev(g):
        return coords(jnp.maximum(g - 1, 0))

    def cur(g):
        return coords(jnp.minimum(g, n_pairs - 1))

    return pl.pallas_call(
        _mla_attn_kernel,
        out_shape=jax.ShapeDtypeStruct((b, s, h * MLA_V), BF16),
        grid=(n_pairs + 1,),
        in_specs=[
            pl.BlockSpec((1, 1, MLA_QK_PAD, 2 * tq), lambda g: (cur(g)[0], cur(g)[1], 0, cur(g)[2])),
            pl.BlockSpec((1, 1, s, MLA_QK_PAD), lambda g: (cur(g)[0], cur(g)[1], 0, 0)),
            pl.BlockSpec((1, 1, MLA_V, s), lambda g: (prev(g)[0], prev(g)[1], 0, 0)),
            pl.BlockSpec((1, 1, MLA_V, s), lambda g: (cur(g)[0], cur(g)[1], 0, 0)),
        ],
        out_specs=pl.BlockSpec((1, 2 * tq, MLA_V), lambda g: (prev(g)[0], prev(g)[2], prev(g)[1])),
        scratch_shapes=[pltpu.VMEM((s, tq), F32), pltpu.VMEM((s, tq), F32),
                        pltpu.VMEM((1, tq), F32), pltpu.VMEM((1, tq), F32),
                        pltpu.VMEM((MLA_V, tq), F32), pltpu.VMEM((1, tq), F32)],
        compiler_params=pltpu.CompilerParams(
            dimension_semantics=("arbitrary",),
            vmem_limit_bytes=V7X_VMEM_LIMIT_BYTES),
        name="mla_attn",
    )(q_t, k, v_t, v_t)


def _diff_attn_kernel(q_ref, k_ref, vp_ref, vc_ref, brow_ref,
                      lq1_ref, lk1_ref, lq2_ref, lk2_ref, g_ref, o_ref,
                      s_a, s_b, m_a, m_b, c_a, c_b, bias_ref, o1_d, *, n_pairs, n_q, lam_init):
    g = pl.program_id(1)

    @pl.when(g == 0)
    def _():
        s_b[...] = jnp.zeros_like(s_b)
        m_b[...] = jnp.zeros_like(m_b)
        c_b[...] = jnp.zeros_like(c_b)
        o1_d[...] = jnp.zeros_like(o1_d)
        n_bias, tk, tq = bias_ref.shape
        for j in range(n_bias):
            rows = jnp.broadcast_to(brow_ref[0, j], (tk, brow_ref.shape[-1]))
            bias_ref[j] = pltpu.roll(rows, 0, 1, stride=1, stride_axis=0)[:, :tq]

    qi = lax.rem(jnp.minimum(g, n_pairs - 1), n_q)
    lam = (jnp.exp(jnp.sum(lq1_ref[...] * lk1_ref[...], axis=-1, keepdims=True))
           - jnp.exp(jnp.sum(lq2_ref[...] * lk2_ref[...], axis=-1, keepdims=True))
           + lam_init)
    a2, l2 = _attn_sweeps(k_ref, q_ref[0, 0, 0], vp_ref, (s_a, m_a, c_a), (s_b, m_b, c_b),
                          ATTN_TK, bias=(bias_ref, qi))
    o = o1_d[...] - lam * (a2 * (1.0 / l2))
    o = o * lax.rsqrt(jnp.mean(o * o, axis=0, keepdims=True) + EPS)
    o_ref[0] = (o.T * (g_ref[...] * (1.0 - lam_init))).astype(o_ref.dtype)
    a1, l1 = _attn_sweeps(k_ref, q_ref[0, 0, 1], vc_ref, (s_b, m_b, c_b), (s_a, m_a, c_a),
                          ATTN_TK, bias=(bias_ref, qi))
    o1_d[...] = a1 * (1.0 / l1)


def _diff_attention(q_t, k, v_t, bias_rows, lq1, lk1, lq2, lk2, g_row, lam_init):
    b, h, _, _, s = q_t.shape
    tq = ATTN_TQ
    n_q = s // tq
    n_pairs = b * n_q
    n_bias, _, span = bias_rows.shape[1:]
    assert s // ATTN_TK >= n_bias - 2
    kern = functools.partial(_diff_attn_kernel, n_pairs=n_pairs, n_q=n_q, lam_init=lam_init)

    def prev(g):
        p = jnp.maximum(g - 1, 0)
        return p // n_q, p % n_q

    def cur(g):
        p = jnp.minimum(g, n_pairs - 1)
        return p // n_q, p % n_q

    vec = pl.BlockSpec((1, DIFF_D), lambda hi, g: (0, 0))
    return pl.pallas_call(
        kern,
        out_shape=jax.ShapeDtypeStruct((b, s, h * DIFF_V), BF16),
        grid=(h, n_pairs + 1),
        in_specs=[
            pl.BlockSpec((1, 1, 2, DIFF_V, tq), lambda hi, g: (cur(g)[0], hi, 0, 0, cur(g)[1])),
            pl.BlockSpec((1, 1, s, DIFF_V), lambda hi, g: (cur(g)[0], hi, 0, 0)),
            pl.BlockSpec((1, 1, DIFF_V, s), lambda hi, g: (prev(g)[0], hi, 0, 0)),
            pl.BlockSpec((1, 1, DIFF_V, s), lambda hi, g: (cur(g)[0], hi, 0, 0)),
            pl.BlockSpec((1, n_bias, 1, span), lambda hi, g: (hi, 0, 0, 0)),
            vec, vec, vec, vec,
            pl.BlockSpec((1, DIFF_V), lambda hi, g: (0, 0)),
        ],
        out_specs=pl.BlockSpec((1, tq, DIFF_V), lambda hi, g: (prev(g)[0], prev(g)[1], hi)),
        scratch_shapes=[pltpu.VMEM((s, tq), F32), pltpu.VMEM((s, tq), F32),
                        pltpu.VMEM((1, tq), F32), pltpu.VMEM((1, tq), F32),
                        pltpu.VMEM((s // ATTN_TK, 1, tq), F32), pltpu.VMEM((s // ATTN_TK, 1, tq), F32),
                        pltpu.VMEM((n_bias, ATTN_TK, tq), F32),
                        pltpu.VMEM((DIFF_V, tq), F32)],
        compiler_params=pltpu.CompilerParams(
            dimension_semantics=("arbitrary", "arbitrary"),
            vmem_limit_bytes=V7X_VMEM_LIMIT_BYTES),
        name="diff_attn",
    )(q_t, k, v_t, v_t, bias_rows, lq1, lk1, lq2, lk2, g_row)


def _t5_bucket(rel):
    half = N_BUCKETS // 2
    ret = jnp.where(rel > 0, half, 0)
    n = jnp.abs(rel)
    max_exact = half // 2
    large = max_exact + (jnp.log(jnp.maximum(n, 1).astype(jnp.float32) / max_exact)
                         / math.log(MAX_DISTANCE / max_exact)
                         * (half - max_exact)).astype(jnp.int32)
    large = jnp.minimum(large, half - 1)
    return ret + jnp.where(n < max_exact, n, large)


def _bias_rows(rel_bias, t):
    far = pl.cdiv(t - 1 + MAX_DISTANCE, t)
    offs = jnp.arange(-far, far + 1, dtype=jnp.int32) * t
    span = 2 * t
    m = jnp.arange(span, dtype=jnp.int32)
    q_minus_k = jnp.where(m < t, m, m - span)
    rel = offs[:, None] - q_minus_k[None, :]
    table = rel_bias.astype(F32) * LOG2E
    bucket = _t5_bucket(rel)[None]
    rows = jnp.zeros((table.shape[1],) + rel.shape, F32)
    for bkt in range(N_BUCKETS):
        rows = jnp.where(bucket == bkt, table[bkt][:, None, None], rows)
    return rows[:, :, None, :]


def _mlp_kernel(x_ref, om_ref, od_ref, wo1_ref, wo2_ref, wup_ref, wdn_ref, o_ref, h_ref):
    for r in range(x_ref.shape[0] // MLP_ROWS):
        rows = slice(r * MLP_ROWS, (r + 1) * MLP_ROWS)
        x1 = (x_ref[rows, :] + _dot(om_ref[rows, :], wo1_ref[...])
              + _dot(od_ref[rows, :], wo2_ref[...]))
        o_ref[rows, :] = x1
        h_ref[rows, :] = (x1 * lax.rsqrt(jnp.mean(x1 * x1, axis=-1, keepdims=True) + EPS)
                          ).astype(BF16)

    for c in range(wup_ref.shape[1] // MLP_FF_CHUNK):
        cols = slice(c * MLP_FF_CHUNK, (c + 1) * MLP_FF_CHUNK)
        up = _dot(h_ref[...], wup_ref[:, cols])
        act = jnp.square(jnp.maximum(up, 0.0)).astype(BF16)
        o_ref[...] += _dot(act, wdn_ref[cols, :])


def _out_proj_mlp(x2d, o_mla, o_diff, wo1, wo2, wup, wdn):
    n, d = x2d.shape
    t = MLP_TOKENS
    resident = lambda a: pl.BlockSpec(a.shape, lambda i: (0, 0), pipeline_mode=pl.Buffered(1))
    return pl.pallas_call(
        _mlp_kernel,
        out_shape=jax.ShapeDtypeStruct((n, d), F32),
        grid=(n // t,),
        in_specs=[
            pl.BlockSpec((t, d), lambda i: (i, 0)),
            pl.BlockSpec((t, MLA_WIDTH), lambda i: (i, 0)),
            pl.BlockSpec((t, DIFF_WIDTH), lambda i: (i, 0)),
            resident(wo1), resident(wo2), resident(wup), resident(wdn),
        ],
        out_specs=pl.BlockSpec((t, d), lambda i: (i, 0)),
        scratch_shapes=[pltpu.VMEM((t, d), BF16)],
        compiler_params=pltpu.CompilerParams(
            dimension_semantics=("arbitrary",),
            vmem_limit_bytes=V7X_VMEM_LIMIT_BYTES),
        name="out_mlp",
    )(x2d, o_mla, o_diff, wo1, wo2, wup, wdn)


def _rope_tables_t(seq):
    inv = ROPE_THETA ** (-jnp.arange(0, MLA_ROPE, 2, dtype=jnp.float32) / MLA_ROPE)
    ang = jnp.arange(seq, dtype=jnp.float32)[:, None] * inv[None, :]
    ang = jnp.concatenate([ang, ang], axis=-1)
    return jnp.cos(ang).T, jnp.sin(ang).T


def kernel(x, attn_norm_w, w_in, q_a_norm_w, w_uq, kv_a_norm_w, w_ukv, mla_q_norm_w, mla_k_norm_w, diff_q_norm_w, diff_k_norm_w, lambda_q1, lambda_k1, lambda_q2, lambda_k2, diff_out_norm_w, w_out, mlp_norm_w, w_up, w_down, rel_bias):
    b, s, d = x.shape
    cos_t, sin_t = _rope_tables_t(s)
    bias_rows = _bias_rows(rel_bias, ATTN_TQ)
    for layer in range(DEPTH):
        lam_init = 0.8 - 0.6 * math.exp(-0.3 * layer)

        wi = w_in[layer].astype(F32) * attn_norm_w[layer].astype(F32)[:, None]
        c0, c1, c2, c3, c4 = (Q_LORA, Q_LORA + KV_LORA, Q_LORA + KV_LORA + MLA_ROPE,
                              Q_LORA + KV_LORA + MLA_ROPE + DIFF_QK_WIDTH,
                              Q_LORA + KV_LORA + MLA_ROPE + 2 * DIFF_QK_WIDTH)
        w_krope = wi[:, c1:c2]
        win_t = jnp.concatenate(
            [wi[:, :c1], w_krope, _rotate_half(w_krope), wi[:, c2:c3], wi[:, c3:c4], wi[:, c4:]],
            axis=1).T.astype(BF16)

        wq = (w_uq[layer].astype(F32) * q_a_norm_w[layer].astype(F32)[:, None]
              ).reshape(Q_LORA, MLA_HEADS, MLA_QK)
        wq_nope = wq[:, :, :MLA_NOPE].reshape(Q_LORA, -1)
        wq_rope = wq[:, :, MLA_NOPE:]
        wuq_t = jnp.concatenate(
            [wq_nope, wq_rope.reshape(Q_LORA, -1), _rotate_half(wq_rope).reshape(Q_LORA, -1)],
            axis=1).T.astype(BF16)

        wkv = (w_ukv[layer].astype(F32) * kv_a_norm_w[layer].astype(F32)[:, None]
               ).reshape(KV_LORA, MLA_HEADS, MLA_NOPE + MLA_V)
        wukv_t = jnp.concatenate(
            [wkv[:, :, :MLA_NOPE].reshape(KV_LORA, -1), wkv[:, :, MLA_NOPE:].reshape(KV_LORA, -1)],
            axis=1).T.astype(BF16)

        gk_row = jnp.concatenate(
            [mla_q_norm_w[layer].astype(F32) * mla_k_norm_w[layer].astype(F32),
             jnp.ones((MLA_QK_PAD - MLA_QK,), F32)])[None, :]
        gd = diff_q_norm_w[layer].astype(F32) * diff_k_norm_w[layer].astype(F32)
        gdk_row = jnp.concatenate([gd, gd])[None, :]

        qm_t, km, vm_t, qd_t, kd, vd_t = _projections(
            x, win_t, wuq_t, wukv_t, cos_t, sin_t, gk_row, gdk_row)

        o_mla = _mla_attention(qm_t, km, vm_t)
        o_diff = _diff_attention(
            qd_t, kd, vd_t, bias_rows,
            lambda_q1[layer].astype(F32)[None, :], lambda_k1[layer].astype(F32)[None, :],
            lambda_q2[layer].astype(F32)[None, :], lambda_k2[layer].astype(F32)[None, :],
            diff_out_norm_w[layer].astype(F32)[None, :], lam_init)

        wo = w_out[layer].astype(BF16)
        wup = (w_up[layer].astype(F32) * mlp_norm_w[layer].astype(F32)[:, None]).astype(BF16)
        x = _out_proj_mlp(
            x.reshape(b * s, d), o_mla.reshape(b * s, MLA_WIDTH), o_diff.reshape(b * s, DIFF_WIDTH),
            wo[:MLA_WIDTH], wo[MLA_WIDTH:], wup, w_down[layer].astype(BF16)).reshape(b, s, d)
    return x
```

```python
import functools
import math

import jax
import jax.numpy as jnp
from jax import lax
from jax.experimental import pallas as pl
from jax.experimental.pallas import tpu as pltpu

D_MODEL = 1024
DEPTH = 1
MLA_HEADS = 4
MLA_NOPE = 128
MLA_ROPE = 64
MLA_V = 128
Q_LORA = D_MODEL // 4
KV_LORA = D_MODEL // 8
MLA_QK = MLA_NOPE + MLA_ROPE
MLA_QK_PAD = 256
DIFF_HEADS = 4
DIFF_D = 64
DIFF_V = 2 * DIFF_D
DIFF_QK_WIDTH = DIFF_HEADS * 2 * DIFF_D
DIFF_WIDTH = DIFF_HEADS * DIFF_V
MLA_WIDTH = MLA_HEADS * MLA_V
D_FF = 4 * D_MODEL
N_BUCKETS = 32
MAX_DISTANCE = 128
ROPE_THETA = 10000.0
EPS = 1e-6
LOG2E = math.log2(math.e)

V7X_VMEM_LIMIT_BYTES = 56 * 1024 * 1024

PROJ_TOKENS = 1024
PROJ_SUB_TOKENS = 256
ATTN_TQ = 512
ATTN_TK = ATTN_TQ
MLA_GROUP = 4
MLP_TOKENS = 1024
MLP_FF_CHUNK = 1024
MLP_ROWS = 256

F32 = jnp.float32
BF16 = jnp.bfloat16


def _dot(a, b):
    return jnp.dot(a, b, preferred_element_type=F32)


def _dot_nt(a, b):
    return lax.dot_general(a, b, (((1,), (1,)), ((), ())), preferred_element_type=F32)


def _rotate_half(x):
    x1, x2 = jnp.split(x, 2, axis=-1)
    return jnp.concatenate([-x2, x1], axis=-1)


def _proj_kernel(x_ref, win_ref, wuq_ref, wukv_ref, cos_ref, sin_ref, gk_ref, gdk_ref,
                 qm_ref, km_ref, vm_ref, qd_ref, kd_ref, vd_ref):
    ys = []
    for i in range(x_ref.shape[1] // PROJ_SUB_TOKENS):
        x = x_ref[0, i * PROJ_SUB_TOKENS:(i + 1) * PROJ_SUB_TOKENS]
        h = x * lax.rsqrt(jnp.mean(x * x, axis=-1, keepdims=True) + EPS)
        ys.append(_dot_nt(win_ref[...], h.astype(BF16)))
    for i, y in enumerate(ys):
        _proj_heads(y, slice(i * PROJ_SUB_TOKENS, (i + 1) * PROJ_SUB_TOKENS),
                    wuq_ref, wukv_ref, cos_ref, sin_ref, gk_ref, gdk_ref,
                    qm_ref, km_ref, vm_ref, qd_ref, kd_ref, vd_ref)


def _proj_heads(y, tok, wuq_ref, wukv_ref, cos_ref, sin_ref, gk_ref, gdk_ref,
                qm_ref, km_ref, vm_ref, qd_ref, kd_ref, vd_ref):
    t = y.shape[1]
    cos = cos_ref[:, tok]
    sin = sin_ref[:, tok]

    o0 = 0
    c_q = y[o0:o0 + Q_LORA]
    o0 += Q_LORA
    c_kv = y[o0:o0 + KV_LORA]
    o0 += KV_LORA
    k_rope = y[o0:o0 + MLA_ROPE]
    o0 += MLA_ROPE
    k_rope_rot = y[o0:o0 + MLA_ROPE]
    o0 += MLA_ROPE
    dq = y[o0:o0 + DIFF_QK_WIDTH]
    o0 += DIFF_QK_WIDTH
    dk = y[o0:o0 + DIFF_QK_WIDTH]
    o0 += DIFF_QK_WIDTH
    dv = y[o0:o0 + DIFF_WIDTH]

    cq_n = c_q * lax.rsqrt(jnp.mean(c_q * c_q, axis=0, keepdims=True) + EPS)
    q_all = _dot(wuq_ref[...], cq_n.astype(BF16))
    q_scale = (MLA_QK ** -0.5) * LOG2E
    zeros_pad = jnp.zeros((MLA_QK_PAD - MLA_QK, t), F32)
    for hd in range(MLA_HEADS):
        nope = q_all[hd * MLA_NOPE:(hd + 1) * MLA_NOPE]
        r0 = MLA_HEADS * MLA_NOPE + hd * MLA_ROPE
        r1 = MLA_HEADS * (MLA_NOPE + MLA_ROPE) + hd * MLA_ROPE
        rope = q_all[r0:r0 + MLA_ROPE] * cos + q_all[r1:r1 + MLA_ROPE] * sin
        ss = jnp.sum(nope * nope, axis=0, keepdims=True) + jnp.sum(rope * rope, axis=0, keepdims=True)
        r = lax.rsqrt(ss * (1.0 / MLA_QK) + EPS) * q_scale
        qm_ref[0, hd, 0:MLA_NOPE, tok] = (nope * r).astype(BF16)
        qm_ref[0, hd, MLA_NOPE:MLA_QK, tok] = (rope * r).astype(BF16)
        qm_ref[0, hd, MLA_QK:MLA_QK_PAD, tok] = zeros_pad.astype(BF16)

    ckv_n = c_kv * lax.rsqrt(jnp.mean(c_kv * c_kv, axis=0, keepdims=True) + EPS)
    kv = _dot(wukv_ref[...], ckv_n.astype(BF16))
    k_r = k_rope * cos + k_rope_rot * sin
    k_r_ss = jnp.sum(k_r * k_r, axis=0, keepdims=True)
    gk = gk_ref[...]
    for hd in range(MLA_HEADS):
        k_nope = kv[hd * MLA_NOPE:(hd + 1) * MLA_NOPE]
        ss = jnp.sum(k_nope * k_nope, axis=0, keepdims=True) + k_r_ss
        r = lax.rsqrt(ss * (1.0 / MLA_QK) + EPS)
        k_t = jnp.concatenate([k_nope * r, k_r * r, zeros_pad], axis=0)
        km_ref[0, hd, tok, :] = (k_t.T * gk).astype(BF16)
        v0 = MLA_HEADS * MLA_NOPE + hd * MLA_V
        vm_ref[0, hd, :, tok] = kv[v0:v0 + MLA_V].astype(BF16)

    d_scale = (DIFF_D ** -0.5) * LOG2E
    zeros_half = jnp.zeros((DIFF_D, t), BF16)
    gdk = gdk_ref[...]
    for hd in range(DIFF_HEADS):
        k_parts = []
        for mp in range(2):
            r0 = hd * DIFF_V + mp * DIFF_D
            qv = dq[r0:r0 + DIFF_D]
            qn = qv * (lax.rsqrt(jnp.mean(qv * qv, axis=0, keepdims=True) + EPS) * d_scale)
            qd_ref[0, hd, mp, mp * DIFF_D:(mp + 1) * DIFF_D, tok] = qn.astype(BF16)
            qd_ref[0, hd, mp, (1 - mp) * DIFF_D:(2 - mp) * DIFF_D, tok] = zeros_half
            kv_ = dk[r0:r0 + DIFF_D]
            k_parts.append(kv_ * lax.rsqrt(jnp.mean(kv_ * kv_, axis=0, keepdims=True) + EPS))
        k_t = jnp.concatenate(k_parts, axis=0)
        kd_ref[0, hd, tok, :] = (k_t.T * gdk).astype(BF16)
        vd_ref[0, hd, :, tok] = dv[hd * DIFF_V:(hd + 1) * DIFF_V].astype(BF16)


def _projections(x, win_t, wuq_t, wukv_t, cos_t, sin_t, gk_row, gdk_row):
    b, s, d = x.shape
    t = PROJ_TOKENS
    n_in = win_t.shape[0]
    const = lambda shape: pl.BlockSpec(shape, lambda bi, si: (0,) * len(shape))
    out_shape = (
        jax.ShapeDtypeStruct((b, MLA_HEADS, MLA_QK_PAD, s), BF16),
        jax.ShapeDtypeStruct((b, MLA_HEADS, s, MLA_QK_PAD), BF16),
        jax.ShapeDtypeStruct((b, MLA_HEADS, MLA_V, s), BF16),
        jax.ShapeDtypeStruct((b, DIFF_HEADS, 2, DIFF_V, s), BF16),
        jax.ShapeDtypeStruct((b, DIFF_HEADS, s, DIFF_V), BF16),
        jax.ShapeDtypeStruct((b, DIFF_HEADS, DIFF_V, s), BF16),
    )
    out_specs = (
        pl.BlockSpec((1, MLA_HEADS, MLA_QK_PAD, t), lambda bi, si: (bi, 0, 0, si)),
        pl.BlockSpec((1, MLA_HEADS, t, MLA_QK_PAD), lambda bi, si: (bi, 0, si, 0)),
        pl.BlockSpec((1, MLA_HEADS, MLA_V, t), lambda bi, si: (bi, 0, 0, si)),
        pl.BlockSpec((1, DIFF_HEADS, 2, DIFF_V, t), lambda bi, si: (bi, 0, 0, 0, si)),
        pl.BlockSpec((1, DIFF_HEADS, t, DIFF_V), lambda bi, si: (bi, 0, si, 0)),
        pl.BlockSpec((1, DIFF_HEADS, DIFF_V, t), lambda bi, si: (bi, 0, 0, si)),
    )
    in_specs = [
        pl.BlockSpec((1, t, d), lambda bi, si: (bi, si, 0)),
        const((n_in, d)),
        const(wuq_t.shape),
        const(wukv_t.shape),
        pl.BlockSpec((MLA_ROPE, t), lambda bi, si: (0, si)),
        pl.BlockSpec((MLA_ROPE, t), lambda bi, si: (0, si)),
        const(gk_row.shape),
        const(gdk_row.shape),
    ]
    return pl.pallas_call(
        _proj_kernel,
        out_shape=out_shape,
        grid=(b, s // t),
        in_specs=in_specs,
        out_specs=out_specs,
        compiler_params=pltpu.CompilerParams(
            dimension_semantics=("arbitrary", "arbitrary"),
            vmem_limit_bytes=V7X_VMEM_LIMIT_BYTES),
        name="proj",
    )(x, win_t, wuq_t, wukv_t, cos_t, sin_t, gk_row, gdk_row)


def _attn_sweeps(k_ref, q_t, v_ref, score_bufs, prev_bufs, tk, bias=None, read_base=None):
    s_w, m_w, c_w = score_bufs
    s_r, m_r, c_r = prev_bufs
    n_keys, tq = s_w.shape
    n_chunks = n_keys // tk
    m_prev = m_r[...]
    m_new = None
    l = jnp.zeros((1, tq), F32)
    acc = jnp.zeros((v_ref.shape[2], tq), F32)
    if bias is not None:
        bias_ref, qi = bias
        n_bias = bias_ref.shape[0]
        far = (n_bias - 1) // 2
        n_near = n_bias - 2
        first_near = jnp.clip(qi - n_near // 2, 0, n_chunks - n_near)
        c_left = bias_ref[0, 0:1, :]
        c_right = bias_ref[n_bias - 1, 0:1, :]
    for j in range(n_chunks):
        if bias is None:
            rows_w = slice(j * tk, (j + 1) * tk)
            s = _dot(k_ref[0, 0, rows_w, :], q_t)
            m_c = jnp.max(s, axis=0, keepdims=True)
        else:
            c = first_near + j if j < n_near else lax.rem(first_near + j, n_chunks)
            rows_w = pl.ds(pl.multiple_of(c * tk, tk), tk)
            s = _dot(k_ref[0, 0, rows_w, :], q_t)
            if j < n_near:
                s = s + bias_ref[jnp.clip(c - qi, -far, far) + far]
                m_c = jnp.max(s, axis=0, keepdims=True)
                c_w[c] = jnp.zeros((1, tq), F32)
            else:
                off = jnp.where(c > qi, c_right, c_left)
                m_c = jnp.max(s, axis=0, keepdims=True) + off
                c_w[c] = off
        s_w[rows_w, :] = s
        m_new = m_c if m_new is None else jnp.maximum(m_new, m_c)

        if read_base is None:
            rows_r = slice(j * tk, (j + 1) * tk)
        else:
            rows_r = pl.ds(pl.multiple_of(read_base + j * tk, tk), tk)
        shift = m_prev if c_r is None else m_prev - c_r[j]
        p = jnp.exp2(s_r[rows_r, :] - shift)
        l = l + jnp.sum(p, axis=0, keepdims=True)
        acc = acc + _dot(v_ref[0, 0, :, rows_r], p.astype(BF16))
    m_w[...] = m_new
    return acc, l


def _mla_attn_kernel(q_ref, k_ref, vp_ref, vc_ref, o_ref, s_a, s_b, m_a, m_b, acc_d, l_d):
    tq = s_a.shape[1]
    bufs = ((s_a, m_a, None), (s_b, m_b, None))

    @pl.when(pl.program_id(0) == 0)
    def _():
        s_b[...] = jnp.zeros_like(s_b)
        m_b[...] = jnp.zeros_like(m_b)
        acc_d[...] = jnp.zeros_like(acc_d)
        l_d[...] = jnp.ones_like(l_d)

    def store(slot, acc, l):
        o_ref[0, slot * tq:(slot + 1) * tq, :] = (acc * (1.0 / l)).astype(o_ref.dtype).T

    zero = jnp.minimum(pl.program_id(0), 0)
    for i in range(MLA_GROUP):
        acc, l = _attn_sweeps(k_ref, q_ref[0, 0, :, i * tq:(i + 1) * tq], vp_ref if i == 0 else vc_ref,
                              bufs[i % 2], bufs[(i + 1) % 2], ATTN_TK,
                              read_base=None if i == 0 else zero)
        if i == 0:
            store(MLA_GROUP - 1, acc, l)
        else:
            store(i - 1, acc_d[i - 1], l_d[i - 1])
            acc_d[i - 1] = acc
            l_d[i - 1] = l


def _mla_attention(q_t, k, v_t):
    b, h, _, s = q_t.shape
    tq = ATTN_TQ
    n_qp = s // (MLA_GROUP * tq)
    n_pairs = b * h * n_qp

    def coords(p):
        return p // (h * n_qp), (p // n_qp) % h, p % n_qp

    def prev(g):
        return coords(jnp.maximum(g - 1, 0))

    def cur(g):
        return coords(jnp.minimum(g, n_pairs - 1))

    return pl.pallas_call(
        _mla_attn_kernel,
        out_shape=jax.ShapeDtypeStruct((b, s, h * MLA_V), BF16),
        grid=(n_pairs + 1,),
        in_specs=[
            pl.BlockSpec((1, 1, MLA_QK_PAD, MLA_GROUP * tq),
                         lambda g: (cur(g)[0], cur(g)[1], 0, cur(g)[2])),
            pl.BlockSpec((1, 1, s, MLA_QK_PAD), lambda g: (cur(g)[0], cur(g)[1], 0, 0)),
            pl.BlockSpec((1, 1, MLA_V, s), lambda g: (prev(g)[0], prev(g)[1], 0, 0)),
            pl.BlockSpec((1, 1, MLA_V, s), lambda g: (cur(g)[0], cur(g)[1], 0, 0)),
        ],
        out_specs=pl.BlockSpec((1, MLA_GROUP * tq, MLA_V),
                               lambda g: (prev(g)[0], prev(g)[2], prev(g)[1])),
        scratch_shapes=[pltpu.VMEM((s, tq), F32), pltpu.VMEM((s, tq), F32),
                        pltpu.VMEM((1, tq), F32), pltpu.VMEM((1, tq), F32),
                        pltpu.VMEM((MLA_GROUP - 1, MLA_V, tq), F32),
                        pltpu.VMEM((MLA_GROUP - 1, 1, tq), F32)],
        compiler_params=pltpu.CompilerParams(
            dimension_semantics=("arbitrary",),
            vmem_limit_bytes=V7X_VMEM_LIMIT_BYTES),
        name="mla_attn",
    )(q_t, k, v_t, v_t)


def _diff_attn_kernel(q_ref, k_ref, vp_ref, vc_ref, brow_ref,
                      lq1_ref, lk1_ref, lq2_ref, lk2_ref, g_ref, o_ref,
                      s_a, s_b, m_a, m_b, c_a, c_b, bias_ref, o1_d, *, n_pairs, n_q, lam_init):
    g = pl.program_id(1)

    @pl.when(g == 0)
    def _():
        s_b[...] = jnp.zeros_like(s_b)
        m_b[...] = jnp.zeros_like(m_b)
        c_b[...] = jnp.zeros_like(c_b)
        o1_d[...] = jnp.zeros_like(o1_d)
        n_bias, tk, tq = bias_ref.shape
        for j in range(n_bias):
            rows = jnp.broadcast_to(brow_ref[0, j], (tk, brow_ref.shape[-1]))
            bias_ref[j] = pltpu.roll(rows, 0, 1, stride=1, stride_axis=0)[:, :tq]

    qi = lax.rem(jnp.minimum(g, n_pairs - 1), n_q)
    lam = (jnp.exp(jnp.sum(lq1_ref[...] * lk1_ref[...], axis=-1, keepdims=True))
           - jnp.exp(jnp.sum(lq2_ref[...] * lk2_ref[...], axis=-1, keepdims=True))
           + lam_init)
    a2, l2 = _attn_sweeps(k_ref, q_ref[0, 0, 0], vp_ref, (s_a, m_a, c_a), (s_b, m_b, c_b),
                          ATTN_TK, bias=(bias_ref, qi))
    o = o1_d[...] - lam * (a2 * (1.0 / l2))
    o = o * lax.rsqrt(jnp.mean(o * o, axis=0, keepdims=True) + EPS)
    o_ref[0] = (o.T * (g_ref[...] * (1.0 - lam_init))).astype(o_ref.dtype)
    a1, l1 = _attn_sweeps(k_ref, q_ref[0, 0, 1], vc_ref, (s_b, m_b, c_b), (s_a, m_a, c_a),
                          ATTN_TK, bias=(bias_ref, qi))
    o1_d[...] = a1 * (1.0 / l1)


def _diff_attention(q_t, k, v_t, bias_rows, lq1, lk1, lq2, lk2, g_row, lam_init):
    b, h, _, _, s = q_t.shape
    tq = ATTN_TQ
    n_q = s // tq
    n_pairs = b * n_q
    n_bias, _, span = bias_rows.shape[1:]
    assert s // ATTN_TK >= n_bias - 2
    kern = functools.partial(_diff_attn_kernel, n_pairs=n_pairs, n_q=n_q, lam_init=lam_init)

    def prev(g):
        p = jnp.maximum(g - 1, 0)
        return p // n_q, p % n_q

    def cur(g):
        p = jnp.minimum(g, n_pairs - 1)
        return p // n_q, p % n_q

    vec = pl.BlockSpec((1, DIFF_D), lambda hi, g: (0, 0))
    return pl.pallas_call(
        kern,
        out_shape=jax.ShapeDtypeStruct((b, s, h * DIFF_V), BF16),
        grid=(h, n_pairs + 1),
        in_specs=[
            pl.BlockSpec((1, 1, 2, DIFF_V, tq), lambda hi, g: (cur(g)[0], hi, 0, 0, cur(g)[1])),
            pl.BlockSpec((1, 1, s, DIFF_V), lambda hi, g: (cur(g)[0], hi, 0, 0)),
            pl.BlockSpec((1, 1, DIFF_V, s), lambda hi, g: (prev(g)[0], hi, 0, 0)),
            pl.BlockSpec((1, 1, DIFF_V, s), lambda hi, g: (cur(g)[0], hi, 0, 0)),
            pl.BlockSpec((1, n_bias, 1, span), lambda hi, g: (hi, 0, 0, 0)),
            vec, vec, vec, vec,
            pl.BlockSpec((1, DIFF_V), lambda hi, g: (0, 0)),
        ],
        out_specs=pl.BlockSpec((1, tq, DIFF_V), lambda hi, g: (prev(g)[0], prev(g)[1], hi)),
        scratch_shapes=[pltpu.VMEM((s, tq), F32), pltpu.VMEM((s, tq), F32),
                        pltpu.VMEM((1, tq), F32), pltpu.VMEM((1, tq), F32),
                        pltpu.VMEM((s // ATTN_TK, 1, tq), F32), pltpu.VMEM((s // ATTN_TK, 1, tq), F32),
                        pltpu.VMEM((n_bias, ATTN_TK, tq), F32),
                        pltpu.VMEM((DIFF_V, tq), F32)],
        compiler_params=pltpu.CompilerParams(
            dimension_semantics=("arbitrary", "arbitrary"),
            vmem_limit_bytes=V7X_VMEM_LIMIT_BYTES),
        name="diff_attn",
    )(q_t, k, v_t, v_t, bias_rows, lq1, lk1, lq2, lk2, g_row)


def _t5_bucket(rel):
    half = N_BUCKETS // 2
    ret = jnp.where(rel > 0, half, 0)
    n = jnp.abs(rel)
    max_exact = half // 2
    large = max_exact + (jnp.log(jnp.maximum(n, 1).astype(jnp.float32) / max_exact)
                         / math.log(MAX_DISTANCE / max_exact)
                         * (half - max_exact)).astype(jnp.int32)
    large = jnp.minimum(large, half - 1)
    return ret + jnp.where(n < max_exact, n, large)


def _bias_rows(rel_bias, t):
    far = pl.cdiv(t - 1 + MAX_DISTANCE, t)
    offs = jnp.arange(-far, far + 1, dtype=jnp.int32) * t
    span = 2 * t
    m = jnp.arange(span, dtype=jnp.int32)
    q_minus_k = jnp.where(m < t, m, m - span)
    rel = offs[:, None] - q_minus_k[None, :]
    table = rel_bias.astype(F32) * LOG2E
    bucket = _t5_bucket(rel)[None]
    rows = jnp.zeros((table.shape[1],) + rel.shape, F32)
    for bkt in range(N_BUCKETS):
        rows = jnp.where(bucket == bkt, table[bkt][:, None, None], rows)
    return rows[:, :, None, :]


def _mlp_kernel(x_ref, om_ref, od_ref, wo1_ref, wo2_ref, wup_ref, wdn_ref, o_ref, h_ref):
    for r in range(x_ref.shape[0] // MLP_ROWS):
        rows = slice(r * MLP_ROWS, (r + 1) * MLP_ROWS)
        x1 = (x_ref[rows, :] + _dot(om_ref[rows, :], wo1_ref[...])
              + _dot(od_ref[rows, :], wo2_ref[...]))
        o_ref[rows, :] = x1
        h_ref[rows, :] = (x1 * lax.rsqrt(jnp.mean(x1 * x1, axis=-1, keepdims=True) + EPS)
                          ).astype(BF16)

    for c in range(wup_ref.shape[1] // MLP_FF_CHUNK):
        cols = slice(c * MLP_FF_CHUNK, (c + 1) * MLP_FF_CHUNK)
        up = _dot(h_ref[...], wup_ref[:, cols])
        act = jnp.square(jnp.maximum(up, 0.0)).astype(BF16)
        o_ref[...] += _dot(act, wdn_ref[cols, :])


def _out_proj_mlp(x2d, o_mla, o_diff, wo1, wo2, wup, wdn):
    n, d = x2d.shape
    t = MLP_TOKENS
    resident = lambda a: pl.BlockSpec(a.shape, lambda i: (0, 0), pipeline_mode=pl.Buffered(1))
    return pl.pallas_call(
        _mlp_kernel,
        out_shape=jax.ShapeDtypeStruct((n, d), F32),
        grid=(n // t,),
        in_specs=[
            pl.BlockSpec((t, d), lambda i: (i, 0)),
            pl.BlockSpec((t, MLA_WIDTH), lambda i: (i, 0)),
            pl.BlockSpec((t, DIFF_WIDTH), lambda i: (i, 0)),
            resident(wo1), resident(wo2), resident(wup), resident(wdn),
        ],
        out_specs=pl.BlockSpec((t, d), lambda i: (i, 0)),
        scratch_shapes=[pltpu.VMEM((t, d), BF16)],
        compiler_params=pltpu.CompilerParams(
            dimension_semantics=("arbitrary",),
            vmem_limit_bytes=V7X_VMEM_LIMIT_BYTES),
        name="out_mlp",
    )(x2d, o_mla, o_diff, wo1, wo2, wup, wdn)


def _rope_tables_t(seq):
    inv = ROPE_THETA ** (-jnp.arange(0, MLA_ROPE, 2, dtype=jnp.float32) / MLA_ROPE)
    ang = jnp.arange(seq, dtype=jnp.float32)[:, None] * inv[None, :]
    ang = jnp.concatenate([ang, ang], axis=-1)
    return jnp.cos(ang).T, jnp.sin(ang).T


def kernel(x, attn_norm_w, w_in, q_a_norm_w, w_uq, kv_a_norm_w, w_ukv, mla_q_norm_w, mla_k_norm_w, diff_q_norm_w, diff_k_norm_w, lambda_q1, lambda_k1, lambda_q2, lambda_k2, diff_out_norm_w, w_out, mlp_norm_w, w_up, w_down, rel_bias):
    b, s, d = x.shape
    cos_t, sin_t = _rope_tables_t(s)
    bias_rows = _bias_rows(rel_bias, ATTN_TQ)
    for layer in range(DEPTH):
        lam_init = 0.8 - 0.6 * math.exp(-0.3 * layer)

        wi = w_in[layer].astype(F32) * attn_norm_w[layer].astype(F32)[:, None]
        c0, c1, c2, c3, c4 = (Q_LORA, Q_LORA + KV_LORA, Q_LORA + KV_LORA + MLA_ROPE,
                              Q_LORA + KV_LORA + MLA_ROPE + DIFF_QK_WIDTH,
                              Q_LORA + KV_LORA + MLA_ROPE + 2 * DIFF_QK_WIDTH)
        w_krope = wi[:, c1:c2]
        win_t = jnp.concatenate(
            [wi[:, :c1], w_krope, _rotate_half(w_krope), wi[:, c2:c3], wi[:, c3:c4], wi[:, c4:]],
            axis=1).T.astype(BF16)

        wq = (w_uq[layer].astype(F32) * q_a_norm_w[layer].astype(F32)[:, None]
              ).reshape(Q_LORA, MLA_HEADS, MLA_QK)
        wq_nope = wq[:, :, :MLA_NOPE].reshape(Q_LORA, -1)
        wq_rope = wq[:, :, MLA_NOPE:]
        wuq_t = jnp.concatenate(
            [wq_nope, wq_rope.reshape(Q_LORA, -1), _rotate_half(wq_rope).reshape(Q_LORA, -1)],
            axis=1).T.astype(BF16)

        wkv = (w_ukv[layer].astype(F32) * kv_a_norm_w[layer].astype(F32)[:, None]
               ).reshape(KV_LORA, MLA_HEADS, MLA_NOPE + MLA_V)
        wukv_t = jnp.concatenate(
            [wkv[:, :, :MLA_NOPE].reshape(KV_LORA, -1), wkv[:, :, MLA_NOPE:].reshape(KV_LORA, -1)],
            axis=1).T.astype(BF16)

        gk_row = jnp.concatenate(
            [mla_q_norm_w[layer].astype(F32) * mla_k_norm_w[layer].astype(F32),
             jnp.ones((MLA_QK_PAD - MLA_QK,), F32)])[None, :]
        gd = diff_q_norm_w[layer].astype(F32) * diff_k_norm_w[layer].astype(F32)
        gdk_row = jnp.concatenate([gd, gd])[None, :]

        qm_t, km, vm_t, qd_t, kd, vd_t = _projections(
            x, win_t, wuq_t, wukv_t, cos_t, sin_t, gk_row, gdk_row)

        o_mla = _mla_attention(qm_t, km, vm_t)
        o_diff = _diff_attention(
            qd_t, kd, vd_t, bias_rows,
            lambda_q1[layer].astype(F32)[None, :], lambda_k1[layer].astype(F32)[None, :],
            lambda_q2[layer].astype(F32)[None, :], lambda_k2[layer].astype(F32)[None, :],
            diff_out_norm_w[layer].astype(F32)[None, :], lam_init)

        wo = w_out[layer].astype(BF16)
        wup = (w_up[layer].astype(F32) * mlp_norm_w[layer].astype(F32)[:, None]).astype(BF16)
        x = _out_proj_mlp(
            x.reshape(b * s, d), o_mla.reshape(b * s, MLA_WIDTH), o_diff.reshape(b * s, DIFF_WIDTH),
            wo[:MLA_WIDTH], wo[MLA_WIDTH:], wup, w_down[layer].astype(BF16)).reshape(b, s, d)
    return x
```

```python
import functools
import math

import jax
import jax.numpy as jnp
from jax import lax
from jax.experimental import pallas as pl
from jax.experimental.pallas import tpu as pltpu

D_MODEL = 1024
DEPTH = 1
MLA_HEADS = 4
MLA_NOPE = 128
MLA_ROPE = 64
MLA_V = 128
Q_LORA = D_MODEL // 4
KV_LORA = D_MODEL // 8
MLA_QK = MLA_NOPE + MLA_ROPE
MLA_QK_PAD = 256
DIFF_HEADS = 4
DIFF_D = 64
DIFF_V = 2 * DIFF_D
DIFF_QK_WIDTH = DIFF_HEADS * 2 * DIFF_D
DIFF_WIDTH = DIFF_HEADS * DIFF_V
MLA_WIDTH = MLA_HEADS * MLA_V
D_FF = 4 * D_MODEL
N_BUCKETS = 32
MAX_DISTANCE = 128
ROPE_THETA = 10000.0
EPS = 1e-6
LOG2E = math.log2(math.e)

V7X_VMEM_LIMIT_BYTES = 56 * 1024 * 1024

PROJ_TOKENS = 1024
PROJ_SUB_TOKENS = 256
ATTN_TQ = 512
ATTN_TK = ATTN_TQ
MLA_GROUP = 4
DIFF_GROUP_Q = 2
MLP_TOKENS = 1024
MLP_FF_CHUNK = 1024
MLP_ROWS = 256

F32 = jnp.float32
BF16 = jnp.bfloat16


def _dot(a, b):
    return jnp.dot(a, b, preferred_element_type=F32)


def _dot_nt(a, b):
    return lax.dot_general(a, b, (((1,), (1,)), ((), ())), preferred_element_type=F32)


def _rotate_half(x):
    x1, x2 = jnp.split(x, 2, axis=-1)
    return jnp.concatenate([-x2, x1], axis=-1)


def _proj_kernel(x_ref, win_ref, wuq_ref, wukv_ref, cos_ref, sin_ref, gk_ref, gdk_ref,
                 qm_ref, km_ref, vm_ref, qd_ref, kd_ref, vd_ref):
    ys = []
    for i in range(x_ref.shape[1] // PROJ_SUB_TOKENS):
        x = x_ref[0, i * PROJ_SUB_TOKENS:(i + 1) * PROJ_SUB_TOKENS]
        h = x * lax.rsqrt(jnp.mean(x * x, axis=-1, keepdims=True) + EPS)
        ys.append(_dot_nt(win_ref[...], h.astype(BF16)))
    for i, y in enumerate(ys):
        _proj_heads(y, slice(i * PROJ_SUB_TOKENS, (i + 1) * PROJ_SUB_TOKENS),
                    wuq_ref, wukv_ref, cos_ref, sin_ref, gk_ref, gdk_ref,
                    qm_ref, km_ref, vm_ref, qd_ref, kd_ref, vd_ref)


def _proj_heads(y, tok, wuq_ref, wukv_ref, cos_ref, sin_ref, gk_ref, gdk_ref,
                qm_ref, km_ref, vm_ref, qd_ref, kd_ref, vd_ref):
    t = y.shape[1]
    cos = cos_ref[:, tok]
    sin = sin_ref[:, tok]

    o0 = 0
    c_q = y[o0:o0 + Q_LORA]
    o0 += Q_LORA
    c_kv = y[o0:o0 + KV_LORA]
    o0 += KV_LORA
    k_rope = y[o0:o0 + MLA_ROPE]
    o0 += MLA_ROPE
    k_rope_rot = y[o0:o0 + MLA_ROPE]
    o0 += MLA_ROPE
    dq = y[o0:o0 + DIFF_QK_WIDTH]
    o0 += DIFF_QK_WIDTH
    dk = y[o0:o0 + DIFF_QK_WIDTH]
    o0 += DIFF_QK_WIDTH
    dv = y[o0:o0 + DIFF_WIDTH]

    cq_n = c_q * lax.rsqrt(jnp.mean(c_q * c_q, axis=0, keepdims=True) + EPS)
    q_all = _dot(wuq_ref[...], cq_n.astype(BF16))
    q_scale = (MLA_QK ** -0.5) * LOG2E
    zeros_pad = jnp.zeros((MLA_QK_PAD - MLA_QK, t), F32)
    for hd in range(MLA_HEADS):
        nope = q_all[hd * MLA_NOPE:(hd + 1) * MLA_NOPE]
        r0 = MLA_HEADS * MLA_NOPE + hd * MLA_ROPE
        r1 = MLA_HEADS * (MLA_NOPE + MLA_ROPE) + hd * MLA_ROPE
        rope = q_all[r0:r0 + MLA_ROPE] * cos + q_all[r1:r1 + MLA_ROPE] * sin
        ss = jnp.sum(nope * nope, axis=0, keepdims=True) + jnp.sum(rope * rope, axis=0, keepdims=True)
        r = lax.rsqrt(ss * (1.0 / MLA_QK) + EPS) * q_scale
        qm_ref[0, hd, 0:MLA_NOPE, tok] = (nope * r).astype(BF16)
        qm_ref[0, hd, MLA_NOPE:MLA_QK, tok] = (rope * r).astype(BF16)
        qm_ref[0, hd, MLA_QK:MLA_QK_PAD, tok] = zeros_pad.astype(BF16)

    ckv_n = c_kv * lax.rsqrt(jnp.mean(c_kv * c_kv, axis=0, keepdims=True) + EPS)
    kv = _dot(wukv_ref[...], ckv_n.astype(BF16))
    k_r = k_rope * cos + k_rope_rot * sin
    k_r_ss = jnp.sum(k_r * k_r, axis=0, keepdims=True)
    gk = gk_ref[...]
    for hd in range(MLA_HEADS):
        k_nope = kv[hd * MLA_NOPE:(hd + 1) * MLA_NOPE]
        ss = jnp.sum(k_nope * k_nope, axis=0, keepdims=True) + k_r_ss
        r = lax.rsqrt(ss * (1.0 / MLA_QK) + EPS)
        k_t = jnp.concatenate([k_nope * r, k_r * r, zeros_pad], axis=0)
        km_ref[0, hd, tok, :] = (k_t.T * gk).astype(BF16)
        v0 = MLA_HEADS * MLA_NOPE + hd * MLA_V
        vm_ref[0, hd, :, tok] = kv[v0:v0 + MLA_V].astype(BF16)

    d_scale = (DIFF_D ** -0.5) * LOG2E
    zeros_half = jnp.zeros((DIFF_D, t), BF16)
    gdk = gdk_ref[...]
    for hd in range(DIFF_HEADS):
        k_parts = []
        for mp in range(2):
            r0 = hd * DIFF_V + mp * DIFF_D
            qv = dq[r0:r0 + DIFF_D]
            qn = qv * (lax.rsqrt(jnp.mean(qv * qv, axis=0, keepdims=True) + EPS) * d_scale)
            qd_ref[0, hd, mp, mp * DIFF_D:(mp + 1) * DIFF_D, tok] = qn.astype(BF16)
            qd_ref[0, hd, mp, (1 - mp) * DIFF_D:(2 - mp) * DIFF_D, tok] = zeros_half
            kv_ = dk[r0:r0 + DIFF_D]
            k_parts.append(kv_ * lax.rsqrt(jnp.mean(kv_ * kv_, axis=0, keepdims=True) + EPS))
        k_t = jnp.concatenate(k_parts, axis=0)
        kd_ref[0, hd, tok, :] = (k_t.T * gdk).astype(BF16)
        vd_ref[0, hd, :, tok] = dv[hd * DIFF_V:(hd + 1) * DIFF_V].astype(BF16)


def _projections(x, win_t, wuq_t, wukv_t, cos_t, sin_t, gk_row, gdk_row):
    b, s, d = x.shape
    t = PROJ_TOKENS
    n_in = win_t.shape[0]
    const = lambda shape: pl.BlockSpec(shape, lambda bi, si: (0,) * len(shape))
    out_shape = (
        jax.ShapeDtypeStruct((b, MLA_HEADS, MLA_QK_PAD, s), BF16),
        jax.ShapeDtypeStruct((b, MLA_HEADS, s, MLA_QK_PAD), BF16),
        jax.ShapeDtypeStruct((b, MLA_HEADS, MLA_V, s), BF16),
        jax.ShapeDtypeStruct((b, DIFF_HEADS, 2, DIFF_V, s), BF16),
        jax.ShapeDtypeStruct((b, DIFF_HEADS, s, DIFF_V), BF16),
        jax.ShapeDtypeStruct((b, DIFF_HEADS, DIFF_V, s), BF16),
    )
    out_specs = (
        pl.BlockSpec((1, MLA_HEADS, MLA_QK_PAD, t), lambda bi, si: (bi, 0, 0, si)),
        pl.BlockSpec((1, MLA_HEADS, t, MLA_QK_PAD), lambda bi, si: (bi, 0, si, 0)),
        pl.BlockSpec((1, MLA_HEADS, MLA_V, t), lambda bi, si: (bi, 0, 0, si)),
        pl.BlockSpec((1, DIFF_HEADS, 2, DIFF_V, t), lambda bi, si: (bi, 0, 0, 0, si)),
        pl.BlockSpec((1, DIFF_HEADS, t, DIFF_V), lambda bi, si: (bi, 0, si, 0)),
        pl.BlockSpec((1, DIFF_HEADS, DIFF_V, t), lambda bi, si: (bi, 0, 0, si)),
    )
    in_specs = [
        pl.BlockSpec((1, t, d), lambda bi, si: (bi, si, 0)),
        const((n_in, d)),
        const(wuq_t.shape),
        const(wukv_t.shape),
        pl.BlockSpec((MLA_ROPE, t), lambda bi, si: (0, si)),
        pl.BlockSpec((MLA_ROPE, t), lambda bi, si: (0, si)),
        const(gk_row.shape),
        const(gdk_row.shape),
    ]
    return pl.pallas_call(
        _proj_kernel,
        out_shape=out_shape,
        grid=(b, s // t),
        in_specs=in_specs,
        out_specs=out_specs,
        compiler_params=pltpu.CompilerParams(
            dimension_semantics=("arbitrary", "arbitrary"),
            vmem_limit_bytes=V7X_VMEM_LIMIT_BYTES),
        name="proj",
    )(x, win_t, wuq_t, wukv_t, cos_t, sin_t, gk_row, gdk_row)


def _attn_sweeps(k_ref, q_t, v_ref, score_bufs, prev_bufs, tk, bias=None, read_base=None):
    s_w, m_w, c_w = score_bufs
    s_r, m_r, c_r = prev_bufs
    n_keys, tq = s_w.shape
    n_chunks = n_keys // tk
    m_prev = m_r[...]
    m_new = None
    l = jnp.zeros((1, tq), F32)
    acc = jnp.zeros((v_ref.shape[2], tq), F32)
    if bias is not None:
        bias_ref, qi = bias
        n_bias = bias_ref.shape[0]
        far = (n_bias - 1) // 2
        n_near = n_bias - 2
        first_near = jnp.clip(qi - n_near // 2, 0, n_chunks - n_near)
        c_left = bias_ref[0, 0:1, :]
        c_right = bias_ref[n_bias - 1, 0:1, :]
    for j in range(n_chunks):
        if bias is None:
            rows_w = slice(j * tk, (j + 1) * tk)
            s = _dot(k_ref[0, 0, rows_w, :], q_t)
            m_c = jnp.max(s, axis=0, keepdims=True)
        else:
            c = first_near + j if j < n_near else lax.rem(first_near + j, n_chunks)
            rows_w = pl.ds(pl.multiple_of(c * tk, tk), tk)
            s = _dot(k_ref[0, 0, rows_w, :], q_t)
            if j < n_near:
                s = s + bias_ref[jnp.clip(c - qi, -far, far) + far]
                m_c = jnp.max(s, axis=0, keepdims=True)
                c_w[c] = jnp.zeros((1, tq), F32)
            else:
                off = jnp.where(c > qi, c_right, c_left)
                m_c = jnp.max(s, axis=0, keepdims=True) + off
                c_w[c] = off
        s_w[rows_w, :] = s
        m_new = m_c if m_new is None else jnp.maximum(m_new, m_c)

        if read_base is None:
            rows_r = slice(j * tk, (j + 1) * tk)
        else:
            rows_r = pl.ds(pl.multiple_of(read_base + j * tk, tk), tk)
        shift = m_prev if c_r is None else m_prev - c_r[j]
        p = jnp.exp2(s_r[rows_r, :] - shift)
        l = l + jnp.sum(p, axis=0, keepdims=True)
        acc = acc + _dot(v_ref[0, 0, :, rows_r], p.astype(BF16))
    m_w[...] = m_new
    return acc, l


def _mla_attn_kernel(q_ref, k_ref, vp_ref, vc_ref, o_ref, s_a, s_b, m_a, m_b, acc_d, l_d):
    tq = s_a.shape[1]
    bufs = ((s_a, m_a, None), (s_b, m_b, None))

    @pl.when(pl.program_id(0) == 0)
    def _():
        s_b[...] = jnp.zeros_like(s_b)
        m_b[...] = jnp.zeros_like(m_b)
        acc_d[...] = jnp.zeros_like(acc_d)
        l_d[...] = jnp.ones_like(l_d)

    def store(slot, acc, l):
        o_ref[0, slot * tq:(slot + 1) * tq, :] = (acc * (1.0 / l)).astype(o_ref.dtype).T

    zero = jnp.minimum(pl.program_id(0), 0)
    for i in range(MLA_GROUP):
        acc, l = _attn_sweeps(k_ref, q_ref[0, 0, :, i * tq:(i + 1) * tq], vp_ref if i == 0 else vc_ref,
                              bufs[i % 2], bufs[(i + 1) % 2], ATTN_TK,
                              read_base=None if i == 0 else zero)
        if i == 0:
            store(MLA_GROUP - 1, acc, l)
        else:
            store(i - 1, acc_d[i - 1], l_d[i - 1])
            acc_d[i - 1] = acc
            l_d[i - 1] = l


def _mla_attention(q_t, k, v_t):
    b, h, _, s = q_t.shape
    tq = ATTN_TQ
    n_qp = s // (MLA_GROUP * tq)
    n_pairs = b * h * n_qp

    def coords(p):
        return p // (h * n_qp), (p // n_qp) % h, p % n_qp

    def prev(g):
        return coords(jnp.maximum(g - 1, 0))

    def cur(g):
        return coords(jnp.minimum(g, n_pairs - 1))

    return pl.pallas_call(
        _mla_attn_kernel,
        out_shape=jax.ShapeDtypeStruct((b, s, h * MLA_V), BF16),
        grid=(n_pairs + 1,),
        in_specs=[
            pl.BlockSpec((1, 1, MLA_QK_PAD, MLA_GROUP * tq),
                         lambda g: (cur(g)[0], cur(g)[1], 0, cur(g)[2])),
            pl.BlockSpec((1, 1, s, MLA_QK_PAD), lambda g: (cur(g)[0], cur(g)[1], 0, 0)),
            pl.BlockSpec((1, 1, MLA_V, s), lambda g: (prev(g)[0], prev(g)[1], 0, 0)),
            pl.BlockSpec((1, 1, MLA_V, s), lambda g: (cur(g)[0], cur(g)[1], 0, 0)),
        ],
        out_specs=pl.BlockSpec((1, MLA_GROUP * tq, MLA_V),
                               lambda g: (prev(g)[0], prev(g)[2], prev(g)[1])),
        scratch_shapes=[pltpu.VMEM((s, tq), F32), pltpu.VMEM((s, tq), F32),
                        pltpu.VMEM((1, tq), F32), pltpu.VMEM((1, tq), F32),
                        pltpu.VMEM((MLA_GROUP - 1, MLA_V, tq), F32),
                        pltpu.VMEM((MLA_GROUP - 1, 1, tq), F32)],
        compiler_params=pltpu.CompilerParams(
            dimension_semantics=("arbitrary",),
            vmem_limit_bytes=V7X_VMEM_LIMIT_BYTES),
        name="mla_attn",
    )(q_t, k, v_t, v_t)


def _diff_attn_kernel(q_ref, k_ref, vp_ref, vc_ref, brow_ref,
                      lq1_ref, lk1_ref, lq2_ref, lk2_ref, g_ref, o_ref,
                      s_a, s_b, m_a, m_b, c_a, c_b, bias_ref, o_d, *, n_groups, n_qg, lam_init):
    g = pl.program_id(1)
    tq = s_a.shape[1]
    n_tiles = 2 * DIFF_GROUP_Q
    bufs = ((s_a, m_a, c_a), (s_b, m_b, c_b))

    @pl.when(g == 0)
    def _():
        s_b[...] = jnp.zeros_like(s_b)
        m_b[...] = jnp.zeros_like(m_b)
        c_b[...] = jnp.zeros_like(c_b)
        o_d[...] = jnp.zeros_like(o_d)
        n_bias, tk, _ = bias_ref.shape
        for j in range(n_bias):
            rows = jnp.broadcast_to(brow_ref[0, j], (tk, brow_ref.shape[-1]))
            bias_ref[j] = pltpu.roll(rows, 0, 1, stride=1, stride_axis=0)[:, :tq]

    qi0 = lax.rem(jnp.minimum(g, n_groups - 1), n_qg) * DIFF_GROUP_Q
    lam = (jnp.exp(jnp.sum(lq1_ref[...] * lk1_ref[...], axis=-1, keepdims=True))
           - jnp.exp(jnp.sum(lq2_ref[...] * lk2_ref[...], axis=-1, keepdims=True))
           + lam_init)

    def write(qt, o_map0, o_map1):
        o = o_map0 - lam * o_map1
        o = o * lax.rsqrt(jnp.mean(o * o, axis=0, keepdims=True) + EPS)
        o_ref[0, qt * tq:(qt + 1) * tq, :] = (o.T * (g_ref[...] * (1.0 - lam_init))).astype(o_ref.dtype)

    for i in range(n_tiles):
        qt, mp = divmod(i, 2)
        acc, l = _attn_sweeps(k_ref, q_ref[0, 0, mp, :, qt * tq:(qt + 1) * tq],
                              vp_ref if i == 0 else vc_ref, bufs[i % 2], bufs[(i + 1) % 2],
                              ATTN_TK, bias=(bias_ref, qi0 + qt))
        o = acc * (1.0 / l)
        if i == 0:
            write(DIFF_GROUP_Q - 1, o_d[n_tiles - 2], o)
            for q_prev in range(DIFF_GROUP_Q - 1):
                write(q_prev, o_d[2 * q_prev], o_d[2 * q_prev + 1])
        else:
            o_d[i - 1] = o


def _diff_attention(q_t, k, v_t, bias_rows, lq1, lk1, lq2, lk2, g_row, lam_init):
    b, h, _, _, s = q_t.shape
    tq = ATTN_TQ
    n_qg = s // (DIFF_GROUP_Q * tq)
    n_groups = b * n_qg
    n_bias, _, span = bias_rows.shape[1:]
    assert s // ATTN_TK >= n_bias - 2
    kern = functools.partial(_diff_attn_kernel, n_groups=n_groups, n_qg=n_qg, lam_init=lam_init)

    def prev(g):
        p = jnp.maximum(g - 1, 0)
        return p // n_qg, p % n_qg

    def cur(g):
        p = jnp.minimum(g, n_groups - 1)
        return p // n_qg, p % n_qg

    vec = pl.BlockSpec((1, DIFF_D), lambda hi, g: (0, 0))
    return pl.pallas_call(
        kern,
        out_shape=jax.ShapeDtypeStruct((b, s, h * DIFF_V), BF16),
        grid=(h, n_groups + 1),
        in_specs=[
            pl.BlockSpec((1, 1, 2, DIFF_V, DIFF_GROUP_Q * tq),
                         lambda hi, g: (cur(g)[0], hi, 0, 0, cur(g)[1])),
            pl.BlockSpec((1, 1, s, DIFF_V), lambda hi, g: (cur(g)[0], hi, 0, 0)),
            pl.BlockSpec((1, 1, DIFF_V, s), lambda hi, g: (prev(g)[0], hi, 0, 0)),
            pl.BlockSpec((1, 1, DIFF_V, s), lambda hi, g: (cur(g)[0], hi, 0, 0)),
            pl.BlockSpec((1, n_bias, 1, span), lambda hi, g: (hi, 0, 0, 0)),
            vec, vec, vec, vec,
            pl.BlockSpec((1, DIFF_V), lambda hi, g: (0, 0)),
        ],
        out_specs=pl.BlockSpec((1, DIFF_GROUP_Q * tq, DIFF_V),
                               lambda hi, g: (prev(g)[0], prev(g)[1], hi)),
        scratch_shapes=[pltpu.VMEM((s, tq), F32), pltpu.VMEM((s, tq), F32),
                        pltpu.VMEM((1, tq), F32), pltpu.VMEM((1, tq), F32),
                        pltpu.VMEM((s // ATTN_TK, 1, tq), F32), pltpu.VMEM((s // ATTN_TK, 1, tq), F32),
                        pltpu.VMEM((n_bias, ATTN_TK, tq), F32),
                        pltpu.VMEM((2 * DIFF_GROUP_Q - 1, DIFF_V, tq), F32)],
        compiler_params=pltpu.CompilerParams(
            dimension_semantics=("arbitrary", "arbitrary"),
            vmem_limit_bytes=V7X_VMEM_LIMIT_BYTES),
        name="diff_attn",
    )(q_t, k, v_t, v_t, bias_rows, lq1, lk1, lq2, lk2, g_row)


def _t5_bucket(rel):
    half = N_BUCKETS // 2
    ret = jnp.where(rel > 0, half, 0)
    n = jnp.abs(rel)
    max_exact = half // 2
    large = max_exact + (jnp.log(jnp.maximum(n, 1).astype(jnp.float32) / max_exact)
                         / math.log(MAX_DISTANCE / max_exact)
                         * (half - max_exact)).astype(jnp.int32)
    large = jnp.minimum(large, half - 1)
    return ret + jnp.where(n < max_exact, n, large)


def _bias_rows(rel_bias, t):
    far = pl.cdiv(t - 1 + MAX_DISTANCE, t)
    offs = jnp.arange(-far, far + 1, dtype=jnp.int32) * t
    span = 2 * t
    m = jnp.arange(span, dtype=jnp.int32)
    q_minus_k = jnp.where(m < t, m, m - span)
    rel = offs[:, None] - q_minus_k[None, :]
    table = rel_bias.astype(F32) * LOG2E
    bucket = _t5_bucket(rel)[None]
    rows = jnp.zeros((table.shape[1],) + rel.shape, F32)
    for bkt in range(N_BUCKETS):
        rows = jnp.where(bucket == bkt, table[bkt][:, None, None], rows)
    return rows[:, :, None, :]


def _mlp_kernel(x_ref, om_ref, od_ref, wo1_ref, wo2_ref, wup_ref, wdn_ref, o_ref, h_ref):
    for r in range(x_ref.shape[0] // MLP_ROWS):
        rows = slice(r * MLP_ROWS, (r + 1) * MLP_ROWS)
        x1 = (x_ref[rows, :] + _dot(om_ref[rows, :], wo1_ref[...])
              + _dot(od_ref[rows, :], wo2_ref[...]))
        o_ref[rows, :] = x1
        h_ref[rows, :] = (x1 * lax.rsqrt(jnp.mean(x1 * x1, axis=-1, keepdims=True) + EPS)
                          ).astype(BF16)

    for c in range(wup_ref.shape[1] // MLP_FF_CHUNK):
        cols = slice(c * MLP_FF_CHUNK, (c + 1) * MLP_FF_CHUNK)
        up = _dot(h_ref[...], wup_ref[:, cols])
        act = jnp.square(jnp.maximum(up, 0.0)).astype(BF16)
        o_ref[...] += _dot(act, wdn_ref[cols, :])


def _out_proj_mlp(x2d, o_mla, o_diff, wo1, wo2, wup, wdn):
    n, d = x2d.shape
    t = MLP_TOKENS
    resident = lambda a: pl.BlockSpec(a.shape, lambda i: (0, 0), pipeline_mode=pl.Buffered(1))
    return pl.pallas_call(
        _mlp_kernel,
        out_shape=jax.ShapeDtypeStruct((n, d), F32),
        grid=(n // t,),
        in_specs=[
            pl.BlockSpec((t, d), lambda i: (i, 0)),
            pl.BlockSpec((t, MLA_WIDTH), lambda i: (i, 0)),
            pl.BlockSpec((t, DIFF_WIDTH), lambda i: (i, 0)),
            resident(wo1), resident(wo2), resident(wup), resident(wdn),
        ],
        out_specs=pl.BlockSpec((t, d), lambda i: (i, 0)),
        scratch_shapes=[pltpu.VMEM((t, d), BF16)],
        compiler_params=pltpu.CompilerParams(
            dimension_semantics=("arbitrary",),
            vmem_limit_bytes=V7X_VMEM_LIMIT_BYTES),
        name="out_mlp",
    )(x2d, o_mla, o_diff, wo1, wo2, wup, wdn)


def _rope_tables_t(seq):
    inv = ROPE_THETA ** (-jnp.arange(0, MLA_ROPE, 2, dtype=jnp.float32) / MLA_ROPE)
    ang = jnp.arange(seq, dtype=jnp.float32)[:, None] * inv[None, :]
    ang = jnp.concatenate([ang, ang], axis=-1)
    return jnp.cos(ang).T, jnp.sin(ang).T


def kernel(x, attn_norm_w, w_in, q_a_norm_w, w_uq, kv_a_norm_w, w_ukv, mla_q_norm_w, mla_k_norm_w, diff_q_norm_w, diff_k_norm_w, lambda_q1, lambda_k1, lambda_q2, lambda_k2, diff_out_norm_w, w_out, mlp_norm_w, w_up, w_down, rel_bias):
    b, s, d = x.shape
    cos_t, sin_t = _rope_tables_t(s)
    bias_rows = _bias_rows(rel_bias, ATTN_TQ)
    for layer in range(DEPTH):
        lam_init = 0.8 - 0.6 * math.exp(-0.3 * layer)

        wi = w_in[layer].astype(F32) * attn_norm_w[layer].astype(F32)[:, None]
        c0, c1, c2, c3, c4 = (Q_LORA, Q_LORA + KV_LORA, Q_LORA + KV_LORA + MLA_ROPE,
                              Q_LORA + KV_LORA + MLA_ROPE + DIFF_QK_WIDTH,
                              Q_LORA + KV_LORA + MLA_ROPE + 2 * DIFF_QK_WIDTH)
        w_krope = wi[:, c1:c2]
        win_t = jnp.concatenate(
            [wi[:, :c1], w_krope, _rotate_half(w_krope), wi[:, c2:c3], wi[:, c3:c4], wi[:, c4:]],
            axis=1).T.astype(BF16)

        wq = (w_uq[layer].astype(F32) * q_a_norm_w[layer].astype(F32)[:, None]
              ).reshape(Q_LORA, MLA_HEADS, MLA_QK)
        wq_nope = wq[:, :, :MLA_NOPE].reshape(Q_LORA, -1)
        wq_rope = wq[:, :, MLA_NOPE:]
        wuq_t = jnp.concatenate(
            [wq_nope, wq_rope.reshape(Q_LORA, -1), _rotate_half(wq_rope).reshape(Q_LORA, -1)],
            axis=1).T.astype(BF16)

        wkv = (w_ukv[layer].astype(F32) * kv_a_norm_w[layer].astype(F32)[:, None]
               ).reshape(KV_LORA, MLA_HEADS, MLA_NOPE + MLA_V)
        wukv_t = jnp.concatenate(
            [wkv[:, :, :MLA_NOPE].reshape(KV_LORA, -1), wkv[:, :, MLA_NOPE:].reshape(KV_LORA, -1)],
            axis=1).T.astype(BF16)

        gk_row = jnp.concatenate(
            [mla_q_norm_w[layer].astype(F32) * mla_k_norm_w[layer].astype(F32),
             jnp.ones((MLA_QK_PAD - MLA_QK,), F32)])[None, :]
        gd = diff_q_norm_w[layer].astype(F32) * diff_k_norm_w[layer].astype(F32)
        gdk_row = jnp.concatenate([gd, gd])[None, :]

        qm_t, km, vm_t, qd_t, kd, vd_t = _projections(
            x, win_t, wuq_t, wukv_t, cos_t, sin_t, gk_row, gdk_row)

        o_mla = _mla_attention(qm_t, km, vm_t)
        o_diff = _diff_attention(
            qd_t, kd, vd_t, bias_rows,
            lambda_q1[layer].astype(F32)[None, :], lambda_k1[layer].astype(F32)[None, :],
            lambda_q2[layer].astype(F32)[None, :], lambda_k2[layer].astype(F32)[None, :],
            diff_out_norm_w[layer].astype(F32)[None, :], lam_init)

        wo = w_out[layer].astype(BF16)
        wup = (w_up[layer].astype(F32) * mlp_norm_w[layer].astype(F32)[:, None]).astype(BF16)
        x = _out_proj_mlp(
            x.reshape(b * s, d), o_mla.reshape(b * s, MLA_WIDTH), o_diff.reshape(b * s, DIFF_WIDTH),
            wo[:MLA_WIDTH], wo[MLA_WIDTH:], wup, w_down[layer].astype(BF16)).reshape(b, s, d)
    return x
```

```python
import functools
import math

import jax
import jax.numpy as jnp
from jax import lax
from jax.experimental import pallas as pl
from jax.experimental.pallas import tpu as pltpu

D_MODEL = 1024
DEPTH = 1
MLA_HEADS = 4
MLA_NOPE = 128
MLA_ROPE = 64
MLA_V = 128
Q_LORA = D_MODEL // 4
KV_LORA = D_MODEL // 8
MLA_QK = MLA_NOPE + MLA_ROPE
MLA_QK_PAD = 256
DIFF_HEADS = 4
DIFF_D = 64
DIFF_V = 2 * DIFF_D
DIFF_QK_WIDTH = DIFF_HEADS * 2 * DIFF_D
DIFF_WIDTH = DIFF_HEADS * DIFF_V
MLA_WIDTH = MLA_HEADS * MLA_V
D_FF = 4 * D_MODEL
N_BUCKETS = 32
MAX_DISTANCE = 128
ROPE_THETA = 10000.0
EPS = 1e-6
LOG2E = math.log2(math.e)

V7X_VMEM_LIMIT_BYTES = 56 * 1024 * 1024

PROJ_TOKENS = 1024
PROJ_SUB_TOKENS = 256
ATTN_TQ = 512
ATTN_TK = ATTN_TQ
MLA_GROUP = 8
DIFF_GROUP_Q = 4
MLP_TOKENS = 1024
MLP_FF_CHUNK = 1024
MLP_ROWS = 256

F32 = jnp.float32
BF16 = jnp.bfloat16


def _dot(a, b):
    return jnp.dot(a, b, preferred_element_type=F32)


def _dot_nt(a, b):
    return lax.dot_general(a, b, (((1,), (1,)), ((), ())), preferred_element_type=F32)


def _rotate_half(x):
    x1, x2 = jnp.split(x, 2, axis=-1)
    return jnp.concatenate([-x2, x1], axis=-1)


def _proj_kernel(x_ref, win_ref, wuq_ref, wukv_ref, cos_ref, sin_ref, gk_ref, gdk_ref,
                 qm_ref, km_ref, vm_ref, qd_ref, kd_ref, vd_ref):
    ys = []
    for i in range(x_ref.shape[1] // PROJ_SUB_TOKENS):
        x = x_ref[0, i * PROJ_SUB_TOKENS:(i + 1) * PROJ_SUB_TOKENS]
        h = x * lax.rsqrt(jnp.mean(x * x, axis=-1, keepdims=True) + EPS)
        ys.append(_dot_nt(win_ref[...], h.astype(BF16)))
    for i, y in enumerate(ys):
        _proj_heads(y, slice(i * PROJ_SUB_TOKENS, (i + 1) * PROJ_SUB_TOKENS),
                    wuq_ref, wukv_ref, cos_ref, sin_ref, gk_ref, gdk_ref,
                    qm_ref, km_ref, vm_ref, qd_ref, kd_ref, vd_ref)


def _proj_heads(y, tok, wuq_ref, wukv_ref, cos_ref, sin_ref, gk_ref, gdk_ref,
                qm_ref, km_ref, vm_ref, qd_ref, kd_ref, vd_ref):
    t = y.shape[1]
    cos = cos_ref[:, tok]
    sin = sin_ref[:, tok]

    o0 = 0
    c_q = y[o0:o0 + Q_LORA]
    o0 += Q_LORA
    c_kv = y[o0:o0 + KV_LORA]
    o0 += KV_LORA
    k_rope = y[o0:o0 + MLA_ROPE]
    o0 += MLA_ROPE
    k_rope_rot = y[o0:o0 + MLA_ROPE]
    o0 += MLA_ROPE
    dq = y[o0:o0 + DIFF_QK_WIDTH]
    o0 += DIFF_QK_WIDTH
    dk = y[o0:o0 + DIFF_QK_WIDTH]
    o0 += DIFF_QK_WIDTH
    dv = y[o0:o0 + DIFF_WIDTH]

    cq_n = c_q * lax.rsqrt(jnp.mean(c_q * c_q, axis=0, keepdims=True) + EPS)
    q_all = _dot(wuq_ref[...], cq_n.astype(BF16))
    q_scale = (MLA_QK ** -0.5) * LOG2E
    zeros_pad = jnp.zeros((MLA_QK_PAD - MLA_QK, t), F32)
    for hd in range(MLA_HEADS):
        nope = q_all[hd * MLA_NOPE:(hd + 1) * MLA_NOPE]
        r0 = MLA_HEADS * MLA_NOPE + hd * MLA_ROPE
        r1 = MLA_HEADS * (MLA_NOPE + MLA_ROPE) + hd * MLA_ROPE
        rope = q_all[r0:r0 + MLA_ROPE] * cos + q_all[r1:r1 + MLA_ROPE] * sin
        ss = jnp.sum(nope * nope, axis=0, keepdims=True) + jnp.sum(rope * rope, axis=0, keepdims=True)
        r = lax.rsqrt(ss * (1.0 / MLA_QK) + EPS) * q_scale
        qm_ref[0, hd, 0:MLA_NOPE, tok] = (nope * r).astype(BF16)
        qm_ref[0, hd, MLA_NOPE:MLA_QK, tok] = (rope * r).astype(BF16)
        qm_ref[0, hd, MLA_QK:MLA_QK_PAD, tok] = zeros_pad.astype(BF16)

    ckv_n = c_kv * lax.rsqrt(jnp.mean(c_kv * c_kv, axis=0, keepdims=True) + EPS)
    kv = _dot(wukv_ref[...], ckv_n.astype(BF16))
    k_r = k_rope * cos + k_rope_rot * sin
    k_r_ss = jnp.sum(k_r * k_r, axis=0, keepdims=True)
    gk = gk_ref[...]
    for hd in range(MLA_HEADS):
        k_nope = kv[hd * MLA_NOPE:(hd + 1) * MLA_NOPE]
        ss = jnp.sum(k_nope * k_nope, axis=0, keepdims=True) + k_r_ss
        r = lax.rsqrt(ss * (1.0 / MLA_QK) + EPS)
        k_t = jnp.concatenate([k_nope * r, k_r * r, zeros_pad], axis=0)
        km_ref[0, hd, tok, :] = (k_t.T * gk).astype(BF16)
        v0 = MLA_HEADS * MLA_NOPE + hd * MLA_V
        vm_ref[0, hd, :, tok] = kv[v0:v0 + MLA_V].astype(BF16)

    d_scale = (DIFF_D ** -0.5) * LOG2E
    zeros_half = jnp.zeros((DIFF_D, t), BF16)
    gdk = gdk_ref[...]
    for hd in range(DIFF_HEADS):
        k_parts = []
        for mp in range(2):
            r0 = hd * DIFF_V + mp * DIFF_D
            qv = dq[r0:r0 + DIFF_D]
            qn = qv * (lax.rsqrt(jnp.mean(qv * qv, axis=0, keepdims=True) + EPS) * d_scale)
            qd_ref[0, hd, mp, mp * DIFF_D:(mp + 1) * DIFF_D, tok] = qn.astype(BF16)
            qd_ref[0, hd, mp, (1 - mp) * DIFF_D:(2 - mp) * DIFF_D, tok] = zeros_half
            kv_ = dk[r0:r0 + DIFF_D]
            k_parts.append(kv_ * lax.rsqrt(jnp.mean(kv_ * kv_, axis=0, keepdims=True) + EPS))
        k_t = jnp.concatenate(k_parts, axis=0)
        kd_ref[0, hd, tok, :] = (k_t.T * gdk).astype(BF16)
        vd_ref[0, hd, :, tok] = dv[hd * DIFF_V:(hd + 1) * DIFF_V].astype(BF16)


def _projections(x, win_t, wuq_t, wukv_t, cos_t, sin_t, gk_row, gdk_row):
    b, s, d = x.shape
    t = PROJ_TOKENS
    n_in = win_t.shape[0]
    const = lambda shape: pl.BlockSpec(shape, lambda bi, si: (0,) * len(shape))
    out_shape = (
        jax.ShapeDtypeStruct((b, MLA_HEADS, MLA_QK_PAD, s), BF16),
        jax.ShapeDtypeStruct((b, MLA_HEADS, s, MLA_QK_PAD), BF16),
        jax.ShapeDtypeStruct((b, MLA_HEADS, MLA_V, s), BF16),
        jax.ShapeDtypeStruct((b, DIFF_HEADS, 2, DIFF_V, s), BF16),
        jax.ShapeDtypeStruct((b, DIFF_HEADS, s, DIFF_V), BF16),
        jax.ShapeDtypeStruct((b, DIFF_HEADS, DIFF_V, s), BF16),
    )
    out_specs = (
        pl.BlockSpec((1, MLA_HEADS, MLA_QK_PAD, t), lambda bi, si: (bi, 0, 0, si)),
        pl.BlockSpec((1, MLA_HEADS, t, MLA_QK_PAD), lambda bi, si: (bi, 0, si, 0)),
        pl.BlockSpec((1, MLA_HEADS, MLA_V, t), lambda bi, si: (bi, 0, 0, si)),
        pl.BlockSpec((1, DIFF_HEADS, 2, DIFF_V, t), lambda bi, si: (bi, 0, 0, 0, si)),
        pl.BlockSpec((1, DIFF_HEADS, t, DIFF_V), lambda bi, si: (bi, 0, si, 0)),
        pl.BlockSpec((1, DIFF_HEADS, DIFF_V, t), lambda bi, si: (bi, 0, 0, si)),
    )
    in_specs = [
        pl.BlockSpec((1, t, d), lambda bi, si: (bi, si, 0)),
        const((n_in, d)),
        const(wuq_t.shape),
        const(wukv_t.shape),
        pl.BlockSpec((MLA_ROPE, t), lambda bi, si: (0, si)),
        pl.BlockSpec((MLA_ROPE, t), lambda bi, si: (0, si)),
        const(gk_row.shape),
        const(gdk_row.shape),
    ]
    return pl.pallas_call(
        _proj_kernel,
        out_shape=out_shape,
        grid=(b, s // t),
        in_specs=in_specs,
        out_specs=out_specs,
        compiler_params=pltpu.CompilerParams(
            dimension_semantics=("arbitrary", "arbitrary"),
            vmem_limit_bytes=V7X_VMEM_LIMIT_BYTES),
        name="proj",
    )(x, win_t, wuq_t, wukv_t, cos_t, sin_t, gk_row, gdk_row)


def _attn_sweeps(k_ref, q_t, v_ref, score_bufs, prev_bufs, tk, bias=None, read_base=None):
    s_w, m_w, c_w = score_bufs
    s_r, m_r, c_r = prev_bufs
    n_keys, tq = s_w.shape
    n_chunks = n_keys // tk
    m_prev = m_r[...]
    m_new = None
    l = jnp.zeros((1, tq), F32)
    acc = jnp.zeros((v_ref.shape[2], tq), F32)
    if bias is not None:
        bias_ref, qi = bias
        n_bias = bias_ref.shape[0]
        far = (n_bias - 1) // 2
        n_near = n_bias - 2
        first_near = jnp.clip(qi - n_near // 2, 0, n_chunks - n_near)
        c_left = bias_ref[0, 0:1, :]
        c_right = bias_ref[n_bias - 1, 0:1, :]
    for j in range(n_chunks):
        if bias is None:
            rows_w = slice(j * tk, (j + 1) * tk)
            s = _dot(k_ref[0, 0, rows_w, :], q_t)
            m_c = jnp.max(s, axis=0, keepdims=True)
        else:
            c = first_near + j if j < n_near else lax.rem(first_near + j, n_chunks)
            rows_w = pl.ds(pl.multiple_of(c * tk, tk), tk)
            s = _dot(k_ref[0, 0, rows_w, :], q_t)
            if j < n_near:
                s = s + bias_ref[jnp.clip(c - qi, -far, far) + far]
                m_c = jnp.max(s, axis=0, keepdims=True)
                c_w[c] = jnp.zeros((1, tq), F32)
            else:
                off = jnp.where(c > qi, c_right, c_left)
                m_c = jnp.max(s, axis=0, keepdims=True) + off
                c_w[c] = off
        s_w[rows_w, :] = s
        m_new = m_c if m_new is None else jnp.maximum(m_new, m_c)

        if read_base is None:
            rows_r = slice(j * tk, (j + 1) * tk)
        else:
            rows_r = pl.ds(pl.multiple_of(read_base + j * tk, tk), tk)
        shift = m_prev if c_r is None else m_prev - c_r[j]
        p = jnp.exp2(s_r[rows_r, :] - shift)
        l = l + jnp.sum(p, axis=0, keepdims=True)
        acc = acc + _dot(v_ref[0, 0, :, rows_r], p.astype(BF16))
    m_w[...] = m_new
    return acc, l


def _mla_attn_kernel(q_ref, k_ref, vp_ref, vc_ref, o_ref, s_a, s_b, m_a, m_b, acc_d, l_d):
    tq = s_a.shape[1]
    bufs = ((s_a, m_a, None), (s_b, m_b, None))

    @pl.when(pl.program_id(0) == 0)
    def _():
        s_b[...] = jnp.zeros_like(s_b)
        m_b[...] = jnp.zeros_like(m_b)
        acc_d[...] = jnp.zeros_like(acc_d)
        l_d[...] = jnp.ones_like(l_d)

    def store(slot, acc, l):
        o_ref[0, slot * tq:(slot + 1) * tq, :] = (acc * (1.0 / l)).astype(o_ref.dtype).T

    zero = jnp.minimum(pl.program_id(0), 0)
    for i in range(MLA_GROUP):
        acc, l = _attn_sweeps(k_ref, q_ref[0, 0, :, i * tq:(i + 1) * tq], vp_ref if i == 0 else vc_ref,
                              bufs[i % 2], bufs[(i + 1) % 2], ATTN_TK,
                              read_base=None if i == 0 else zero)
        if i == 0:
            store(MLA_GROUP - 1, acc, l)
        else:
            store(i - 1, acc_d[i - 1], l_d[i - 1])
            acc_d[i - 1] = acc
            l_d[i - 1] = l


def _mla_attention(q_t, k, v_t):
    b, h, _, s = q_t.shape
    tq = ATTN_TQ
    n_qp = s // (MLA_GROUP * tq)
    n_pairs = b * h * n_qp

    def coords(p):
        return p // (h * n_qp), (p // n_qp) % h, p % n_qp

    def prev(g):
        return coords(jnp.maximum(g - 1, 0))

    def cur(g):
        return coords(jnp.minimum(g, n_pairs - 1))

    return pl.pallas_call(
        _mla_attn_kernel,
        out_shape=jax.ShapeDtypeStruct((b, s, h * MLA_V), BF16),
        grid=(n_pairs + 1,),
        in_specs=[
            pl.BlockSpec((1, 1, MLA_QK_PAD, MLA_GROUP * tq),
                         lambda g: (cur(g)[0], cur(g)[1], 0, cur(g)[2])),
            pl.BlockSpec((1, 1, s, MLA_QK_PAD), lambda g: (cur(g)[0], cur(g)[1], 0, 0)),
            pl.BlockSpec((1, 1, MLA_V, s), lambda g: (prev(g)[0], prev(g)[1], 0, 0)),
            pl.BlockSpec((1, 1, MLA_V, s), lambda g: (cur(g)[0], cur(g)[1], 0, 0)),
        ],
        out_specs=pl.BlockSpec((1, MLA_GROUP * tq, MLA_V),
                               lambda g: (prev(g)[0], prev(g)[2], prev(g)[1])),
        scratch_shapes=[pltpu.VMEM((s, tq), F32), pltpu.VMEM((s, tq), F32),
                        pltpu.VMEM((1, tq), F32), pltpu.VMEM((1, tq), F32),
                        pltpu.VMEM((MLA_GROUP - 1, MLA_V, tq), F32),
                        pltpu.VMEM((MLA_GROUP - 1, 1, tq), F32)],
        compiler_params=pltpu.CompilerParams(
            dimension_semantics=("arbitrary",),
            vmem_limit_bytes=V7X_VMEM_LIMIT_BYTES),
        name="mla_attn",
    )(q_t, k, v_t, v_t)


def _diff_attn_kernel(q_ref, k_ref, vp_ref, vc_ref, brow_ref,
                      lq1_ref, lk1_ref, lq2_ref, lk2_ref, g_ref, o_ref,
                      s_a, s_b, m_a, m_b, c_a, c_b, bias_ref, o_d, *, n_groups, n_qg, lam_init):
    g = pl.program_id(1)
    tq = s_a.shape[1]
    n_tiles = 2 * DIFF_GROUP_Q
    bufs = ((s_a, m_a, c_a), (s_b, m_b, c_b))

    @pl.when(g == 0)
    def _():
        s_b[...] = jnp.zeros_like(s_b)
        m_b[...] = jnp.zeros_like(m_b)
        c_b[...] = jnp.zeros_like(c_b)
        o_d[...] = jnp.zeros_like(o_d)
        n_bias, tk, _ = bias_ref.shape
        for j in range(n_bias):
            rows = jnp.broadcast_to(brow_ref[0, j], (tk, brow_ref.shape[-1]))
            bias_ref[j] = pltpu.roll(rows, 0, 1, stride=1, stride_axis=0)[:, :tq]

    qi0 = lax.rem(jnp.minimum(g, n_groups - 1), n_qg) * DIFF_GROUP_Q
    lam = (jnp.exp(jnp.sum(lq1_ref[...] * lk1_ref[...], axis=-1, keepdims=True))
           - jnp.exp(jnp.sum(lq2_ref[...] * lk2_ref[...], axis=-1, keepdims=True))
           + lam_init)

    def write(qt, o_map0, o_map1):
        o = o_map0 - lam * o_map1
        o = o * lax.rsqrt(jnp.mean(o * o, axis=0, keepdims=True) + EPS)
        o_ref[0, qt * tq:(qt + 1) * tq, :] = (o.T * (g_ref[...] * (1.0 - lam_init))).astype(o_ref.dtype)

    for i in range(n_tiles):
        qt, mp = divmod(i, 2)
        acc, l = _attn_sweeps(k_ref, q_ref[0, 0, mp, :, qt * tq:(qt + 1) * tq],
                              vp_ref if i == 0 else vc_ref, bufs[i % 2], bufs[(i + 1) % 2],
                              ATTN_TK, bias=(bias_ref, qi0 + qt))
        o = acc * (1.0 / l)
        if i == 0:
            write(DIFF_GROUP_Q - 1, o_d[n_tiles - 2], o)
            for q_prev in range(DIFF_GROUP_Q - 1):
                write(q_prev, o_d[2 * q_prev], o_d[2 * q_prev + 1])
        else:
            o_d[i - 1] = o


def _diff_attention(q_t, k, v_t, bias_rows, lq1, lk1, lq2, lk2, g_row, lam_init):
    b, h, _, _, s = q_t.shape
    tq = ATTN_TQ
    n_qg = s // (DIFF_GROUP_Q * tq)
    n_groups = b * n_qg
    n_bias, _, span = bias_rows.shape[1:]
    assert s // ATTN_TK >= n_bias - 2
    kern = functools.partial(_diff_attn_kernel, n_groups=n_groups, n_qg=n_qg, lam_init=lam_init)

    def prev(g):
        p = jnp.maximum(g - 1, 0)
        return p // n_qg, p % n_qg

    def cur(g):
        p = jnp.minimum(g, n_groups - 1)
        return p // n_qg, p % n_qg

    vec = pl.BlockSpec((1, DIFF_D), lambda hi, g: (0, 0))
    return pl.pallas_call(
        kern,
        out_shape=jax.ShapeDtypeStruct((b, s, h * DIFF_V), BF16),
        grid=(h, n_groups + 1),
        in_specs=[
            pl.BlockSpec((1, 1, 2, DIFF_V, DIFF_GROUP_Q * tq),
                         lambda hi, g: (cur(g)[0], hi, 0, 0, cur(g)[1])),
            pl.BlockSpec((1, 1, s, DIFF_V), lambda hi, g: (cur(g)[0], hi, 0, 0)),
            pl.BlockSpec((1, 1, DIFF_V, s), lambda hi, g: (prev(g)[0], hi, 0, 0)),
            pl.BlockSpec((1, 1, DIFF_V, s), lambda hi, g: (cur(g)[0], hi, 0, 0)),
            pl.BlockSpec((1, n_bias, 1, span), lambda hi, g: (hi, 0, 0, 0)),
            vec, vec, vec, vec,
            pl.BlockSpec((1, DIFF_V), lambda hi, g: (0, 0)),
        ],
        out_specs=pl.BlockSpec((1, DIFF_GROUP_Q * tq, DIFF_V),
                               lambda hi, g: (prev(g)[0], prev(g)[1], hi)),
        scratch_shapes=[pltpu.VMEM((s, tq), F32), pltpu.VMEM((s, tq), F32),
                        pltpu.VMEM((1, tq), F32), pltpu.VMEM((1, tq), F32),
                        pltpu.VMEM((s // ATTN_TK, 1, tq), F32), pltpu.VMEM((s // ATTN_TK, 1, tq), F32),
                        pltpu.VMEM((n_bias, ATTN_TK, tq), F32),
                        pltpu.VMEM((2 * DIFF_GROUP_Q - 1, DIFF_V, tq), F32)],
        compiler_params=pltpu.CompilerParams(
            dimension_semantics=("arbitrary", "arbitrary"),
            vmem_limit_bytes=V7X_VMEM_LIMIT_BYTES),
        name="diff_attn",
    )(q_t, k, v_t, v_t, bias_rows, lq1, lk1, lq2, lk2, g_row)


def _t5_bucket(rel):
    half = N_BUCKETS // 2
    ret = jnp.where(rel > 0, half, 0)
    n = jnp.abs(rel)
    max_exact = half // 2
    large = max_exact + (jnp.log(jnp.maximum(n, 1).astype(jnp.float32) / max_exact)
                         / math.log(MAX_DISTANCE / max_exact)
                         * (half - max_exact)).astype(jnp.int32)
    large = jnp.minimum(large, half - 1)
    return ret + jnp.where(n < max_exact, n, large)


def _bias_rows(rel_bias, t):
    far = pl.cdiv(t - 1 + MAX_DISTANCE, t)
    offs = jnp.arange(-far, far + 1, dtype=jnp.int32) * t
    span = 2 * t
    m = jnp.arange(span, dtype=jnp.int32)
    q_minus_k = jnp.where(m < t, m, m - span)
    rel = offs[:, None] - q_minus_k[None, :]
    table = rel_bias.astype(F32) * LOG2E
    bucket = _t5_bucket(rel)[None]
    rows = jnp.zeros((table.shape[1],) + rel.shape, F32)
    for bkt in range(N_BUCKETS):
        rows = jnp.where(bucket == bkt, table[bkt][:, None, None], rows)
    return rows[:, :, None, :]


def _mlp_kernel(x_ref, om_ref, od_ref, wo1_ref, wo2_ref, wup_ref, wdn_ref, o_ref, h_ref):
    for r in range(x_ref.shape[0] // MLP_ROWS):
        rows = slice(r * MLP_ROWS, (r + 1) * MLP_ROWS)
        x1 = (x_ref[rows, :] + _dot(om_ref[rows, :], wo1_ref[...])
              + _dot(od_ref[rows, :], wo2_ref[...]))
        o_ref[rows, :] = x1
        h_ref[rows, :] = (x1 * lax.rsqrt(jnp.mean(x1 * x1, axis=-1, keepdims=True) + EPS)
                          ).astype(BF16)

    for c in range(wup_ref.shape[1] // MLP_FF_CHUNK):
        cols = slice(c * MLP_FF_CHUNK, (c + 1) * MLP_FF_CHUNK)
        up = _dot(h_ref[...], wup_ref[:, cols])
        act = jnp.square(jnp.maximum(up, 0.0)).astype(BF16)
        o_ref[...] += _dot(act, wdn_ref[cols, :])


def _out_proj_mlp(x2d, o_mla, o_diff, wo1, wo2, wup, wdn):
    n, d = x2d.shape
    t = MLP_TOKENS
    resident = lambda a: pl.BlockSpec(a.shape, lambda i: (0, 0), pipeline_mode=pl.Buffered(1))
    return pl.pallas_call(
        _mlp_kernel,
        out_shape=jax.ShapeDtypeStruct((n, d), F32),
        grid=(n // t,),
        in_specs=[
            pl.BlockSpec((t, d), lambda i: (i, 0)),
            pl.BlockSpec((t, MLA_WIDTH), lambda i: (i, 0)),
            pl.BlockSpec((t, DIFF_WIDTH), lambda i: (i, 0)),
            resident(wo1), resident(wo2), resident(wup), resident(wdn),
        ],
        out_specs=pl.BlockSpec((t, d), lambda i: (i, 0)),
        scratch_shapes=[pltpu.VMEM((t, d), BF16)],
        compiler_params=pltpu.CompilerParams(
            dimension_semantics=("arbitrary",),
            vmem_limit_bytes=V7X_VMEM_LIMIT_BYTES),
        name="out_mlp",
    )(x2d, o_mla, o_diff, wo1, wo2, wup, wdn)


def _rope_tables_t(seq):
    inv = ROPE_THETA ** (-jnp.arange(0, MLA_ROPE, 2, dtype=jnp.float32) / MLA_ROPE)
    ang = jnp.arange(seq, dtype=jnp.float32)[:, None] * inv[None, :]
    ang = jnp.concatenate([ang, ang], axis=-1)
    return jnp.cos(ang).T, jnp.sin(ang).T


def kernel(x, attn_norm_w, w_in, q_a_norm_w, w_uq, kv_a_norm_w, w_ukv, mla_q_norm_w, mla_k_norm_w, diff_q_norm_w, diff_k_norm_w, lambda_q1, lambda_k1, lambda_q2, lambda_k2, diff_out_norm_w, w_out, mlp_norm_w, w_up, w_down, rel_bias):
    b, s, d = x.shape
    cos_t, sin_t = _rope_tables_t(s)
    bias_rows = _bias_rows(rel_bias, ATTN_TQ)
    for layer in range(DEPTH):
        lam_init = 0.8 - 0.6 * math.exp(-0.3 * layer)

        wi = w_in[layer].astype(F32) * attn_norm_w[layer].astype(F32)[:, None]
        c0, c1, c2, c3, c4 = (Q_LORA, Q_LORA + KV_LORA, Q_LORA + KV_LORA + MLA_ROPE,
                              Q_LORA + KV_LORA + MLA_ROPE + DIFF_QK_WIDTH,
                              Q_LORA + KV_LORA + MLA_ROPE + 2 * DIFF_QK_WIDTH)
        w_krope = wi[:, c1:c2]
        win_t = jnp.concatenate(
            [wi[:, :c1], w_krope, _rotate_half(w_krope), wi[:, c2:c3], wi[:, c3:c4], wi[:, c4:]],
            axis=1).T.astype(BF16)

        wq = (w_uq[layer].astype(F32) * q_a_norm_w[layer].astype(F32)[:, None]
              ).reshape(Q_LORA, MLA_HEADS, MLA_QK)
        wq_nope = wq[:, :, :MLA_NOPE].reshape(Q_LORA, -1)
        wq_rope = wq[:, :, MLA_NOPE:]
        wuq_t = jnp.concatenate(
            [wq_nope, wq_rope.reshape(Q_LORA, -1), _rotate_half(wq_rope).reshape(Q_LORA, -1)],
            axis=1).T.astype(BF16)

        wkv = (w_ukv[layer].astype(F32) * kv_a_norm_w[layer].astype(F32)[:, None]
               ).reshape(KV_LORA, MLA_HEADS, MLA_NOPE + MLA_V)
        wukv_t = jnp.concatenate(
            [wkv[:, :, :MLA_NOPE].reshape(KV_LORA, -1), wkv[:, :, MLA_NOPE:].reshape(KV_LORA, -1)],
            axis=1).T.astype(BF16)

        gk_row = jnp.concatenate(
            [mla_q_norm_w[layer].astype(F32) * mla_k_norm_w[layer].astype(F32),
             jnp.ones((MLA_QK_PAD - MLA_QK,), F32)])[None, :]
        gd = diff_q_norm_w[layer].astype(F32) * diff_k_norm_w[layer].astype(F32)
        gdk_row = jnp.concatenate([gd, gd])[None, :]

        qm_t, km, vm_t, qd_t, kd, vd_t = _projections(
            x, win_t, wuq_t, wukv_t, cos_t, sin_t, gk_row, gdk_row)

        o_mla = _mla_attention(qm_t, km, vm_t)
        o_diff = _diff_attention(
            qd_t, kd, vd_t, bias_rows,
            lambda_q1[layer].astype(F32)[None, :], lambda_k1[layer].astype(F32)[None, :],
            lambda_q2[layer].astype(F32)[None, :], lambda_k2[layer].astype(F32)[None, :],
            diff_out_norm_w[layer].astype(F32)[None, :], lam_init)

        wo = w_out[layer].astype(BF16)
        wup = (w_up[layer].astype(F32) * mlp_norm_w[layer].astype(F32)[:, None]).astype(BF16)
        x = _out_proj_mlp(
            x.reshape(b * s, d), o_mla.reshape(b * s, MLA_WIDTH), o_diff.reshape(b * s, DIFF_WIDTH),
            wo[:MLA_WIDTH], wo[MLA_WIDTH:], wup, w_down[layer].astype(BF16)).reshape(b, s, d)
    return x
```

```python
import functools
import math

import jax
import jax.numpy as jnp
from jax import lax
from jax.experimental import pallas as pl
from jax.experimental.pallas import tpu as pltpu

D_MODEL = 1024
DEPTH = 1
MLA_HEADS = 4
MLA_NOPE = 128
MLA_ROPE = 64
MLA_V = 128
Q_LORA = D_MODEL // 4
KV_LORA = D_MODEL // 8
MLA_QK = MLA_NOPE + MLA_ROPE
MLA_QK_PAD = 256
DIFF_HEADS = 4
DIFF_D = 64
DIFF_V = 2 * DIFF_D
DIFF_QK_WIDTH = DIFF_HEADS * 2 * DIFF_D
DIFF_WIDTH = DIFF_HEADS * DIFF_V
MLA_WIDTH = MLA_HEADS * MLA_V
D_FF = 4 * D_MODEL
N_BUCKETS = 32
MAX_DISTANCE = 128
ROPE_THETA = 10000.0
EPS = 1e-6
LOG2E = math.log2(math.e)

V7X_VMEM_LIMIT_BYTES = 56 * 1024 * 1024

PROJ_TOKENS = 1024
PROJ_SUB_TOKENS = 256
ATTN_TQ = 512
ATTN_TK = ATTN_TQ
MLA_GROUP = 8
DIFF_GROUP_Q = 4
MLP_TOKENS = 1024
MLP_FF_CHUNK = 1024
MLP_ROWS = 256

F32 = jnp.float32
BF16 = jnp.bfloat16


def _dot(a, b):
    return jnp.dot(a, b, preferred_element_type=F32)


def _dot_nt(a, b):
    return lax.dot_general(a, b, (((1,), (1,)), ((), ())), preferred_element_type=F32)


def _rotate_half(x):
    x1, x2 = jnp.split(x, 2, axis=-1)
    return jnp.concatenate([-x2, x1], axis=-1)


def _proj_kernel(x_ref, win_ref, wuq_ref, wukv_ref, cos_ref, sin_ref, gk_ref, gdk_ref,
                 qm_ref, km_ref, vm_ref, qd_ref, kd_ref, vd_ref):
    ys = []
    for i in range(x_ref.shape[1] // PROJ_SUB_TOKENS):
        x = x_ref[0, i * PROJ_SUB_TOKENS:(i + 1) * PROJ_SUB_TOKENS]
        h = x * lax.rsqrt(jnp.mean(x * x, axis=-1, keepdims=True) + EPS)
        ys.append(_dot_nt(win_ref[...], h.astype(BF16)))
    for i, y in enumerate(ys):
        _proj_heads(y, slice(i * PROJ_SUB_TOKENS, (i + 1) * PROJ_SUB_TOKENS),
                    wuq_ref, wukv_ref, cos_ref, sin_ref, gk_ref, gdk_ref,
                    qm_ref, km_ref, vm_ref, qd_ref, kd_ref, vd_ref)


def _proj_heads(y, tok, wuq_ref, wukv_ref, cos_ref, sin_ref, gk_ref, gdk_ref,
                qm_ref, km_ref, vm_ref, qd_ref, kd_ref, vd_ref):
    t = y.shape[1]
    cos = cos_ref[:, tok]
    sin = sin_ref[:, tok]

    o0 = 0
    c_q = y[o0:o0 + Q_LORA]
    o0 += Q_LORA
    c_kv = y[o0:o0 + KV_LORA]
    o0 += KV_LORA
    k_rope = y[o0:o0 + MLA_ROPE]
    o0 += MLA_ROPE
    k_rope_rot = y[o0:o0 + MLA_ROPE]
    o0 += MLA_ROPE
    dq = y[o0:o0 + DIFF_QK_WIDTH]
    o0 += DIFF_QK_WIDTH
    dk = y[o0:o0 + DIFF_QK_WIDTH]
    o0 += DIFF_QK_WIDTH
    dv = y[o0:o0 + DIFF_WIDTH]

    cq_n = c_q * lax.rsqrt(jnp.mean(c_q * c_q, axis=0, keepdims=True) + EPS)
    q_all = _dot(wuq_ref[...], cq_n.astype(BF16))
    q_scale = (MLA_QK ** -0.5) * LOG2E
    zeros_pad = jnp.zeros((MLA_QK_PAD - MLA_QK, t), F32)
    for hd in range(MLA_HEADS):
        nope = q_all[hd * MLA_NOPE:(hd + 1) * MLA_NOPE]
        r0 = MLA_HEADS * MLA_NOPE + hd * MLA_ROPE
        r1 = MLA_HEADS * (MLA_NOPE + MLA_ROPE) + hd * MLA_ROPE
        rope = q_all[r0:r0 + MLA_ROPE] * cos + q_all[r1:r1 + MLA_ROPE] * sin
        ss = jnp.sum(nope * nope, axis=0, keepdims=True) + jnp.sum(rope * rope, axis=0, keepdims=True)
        r = lax.rsqrt(ss * (1.0 / MLA_QK) + EPS) * q_scale
        qm_ref[0, hd, 0:MLA_NOPE, tok] = (nope * r).astype(BF16)
        qm_ref[0, hd, MLA_NOPE:MLA_QK, tok] = (rope * r).astype(BF16)
        qm_ref[0, hd, MLA_QK:MLA_QK_PAD, tok] = zeros_pad.astype(BF16)

    ckv_n = c_kv * lax.rsqrt(jnp.mean(c_kv * c_kv, axis=0, keepdims=True) + EPS)
    kv = _dot(wukv_ref[...], ckv_n.astype(BF16))
    k_r = k_rope * cos + k_rope_rot * sin
    k_r_ss = jnp.sum(k_r * k_r, axis=0, keepdims=True)
    gk = gk_ref[...]
    for hd in range(MLA_HEADS):
        k_nope = kv[hd * MLA_NOPE:(hd + 1) * MLA_NOPE]
        ss = jnp.sum(k_nope * k_nope, axis=0, keepdims=True) + k_r_ss
        r = lax.rsqrt(ss * (1.0 / MLA_QK) + EPS)
        k_t = jnp.concatenate([k_nope * r, k_r * r, zeros_pad], axis=0)
        km_ref[0, hd, tok, :] = (k_t.T * gk).astype(BF16)
        v0 = MLA_HEADS * MLA_NOPE + hd * MLA_V
        vm_ref[0, hd, :, tok] = kv[v0:v0 + MLA_V].astype(BF16)

    d_scale = (DIFF_D ** -0.5) * LOG2E
    zeros_half = jnp.zeros((DIFF_D, t), BF16)
    gdk = gdk_ref[...]
    for hd in range(DIFF_HEADS):
        k_parts = []
        for mp in range(2):
            r0 = hd * DIFF_V + mp * DIFF_D
            qv = dq[r0:r0 + DIFF_D]
            qn = qv * (lax.rsqrt(jnp.mean(qv * qv, axis=0, keepdims=True) + EPS) * d_scale)
            qd_ref[0, hd, mp, mp * DIFF_D:(mp + 1) * DIFF_D, tok] = qn.astype(BF16)
            qd_ref[0, hd, mp, (1 - mp) * DIFF_D:(2 - mp) * DIFF_D, tok] = zeros_half
            kv_ = dk[r0:r0 + DIFF_D]
            k_parts.append(kv_ * lax.rsqrt(jnp.mean(kv_ * kv_, axis=0, keepdims=True) + EPS))
        k_t = jnp.concatenate(k_parts, axis=0)
        kd_ref[0, hd, tok, :] = (k_t.T * gdk).astype(BF16)
        vd_ref[0, hd, :, tok] = dv[hd * DIFF_V:(hd + 1) * DIFF_V].astype(BF16)


def _projections(x, win_t, wuq_t, wukv_t, cos_t, sin_t, gk_row, gdk_row):
    b, s, d = x.shape
    t = PROJ_TOKENS
    n_in = win_t.shape[0]
    const = lambda shape: pl.BlockSpec(shape, lambda bi, si: (0,) * len(shape))
    out_shape = (
        jax.ShapeDtypeStruct((b, MLA_HEADS, MLA_QK_PAD, s), BF16),
        jax.ShapeDtypeStruct((b, MLA_HEADS, s, MLA_QK_PAD), BF16),
        jax.ShapeDtypeStruct((b, MLA_HEADS, MLA_V, s), BF16),
        jax.ShapeDtypeStruct((b, DIFF_HEADS, 2, DIFF_V, s), BF16),
        jax.ShapeDtypeStruct((b, DIFF_HEADS, s, DIFF_V), BF16),
        jax.ShapeDtypeStruct((b, DIFF_HEADS, DIFF_V, s), BF16),
    )
    out_specs = (
        pl.BlockSpec((1, MLA_HEADS, MLA_QK_PAD, t), lambda bi, si: (bi, 0, 0, si)),
        pl.BlockSpec((1, MLA_HEADS, t, MLA_QK_PAD), lambda bi, si: (bi, 0, si, 0)),
        pl.BlockSpec((1, MLA_HEADS, MLA_V, t), lambda bi, si: (bi, 0, 0, si)),
        pl.BlockSpec((1, DIFF_HEADS, 2, DIFF_V, t), lambda bi, si: (bi, 0, 0, 0, si)),
        pl.BlockSpec((1, DIFF_HEADS, t, DIFF_V), lambda bi, si: (bi, 0, si, 0)),
        pl.BlockSpec((1, DIFF_HEADS, DIFF_V, t), lambda bi, si: (bi, 0, 0, si)),
    )
    in_specs = [
        pl.BlockSpec((1, t, d), lambda bi, si: (bi, si, 0)),
        const((n_in, d)),
        const(wuq_t.shape),
        const(wukv_t.shape),
        pl.BlockSpec((MLA_ROPE, t), lambda bi, si: (0, si)),
        pl.BlockSpec((MLA_ROPE, t), lambda bi, si: (0, si)),
        const(gk_row.shape),
        const(gdk_row.shape),
    ]
    return pl.pallas_call(
        _proj_kernel,
        out_shape=out_shape,
        grid=(b, s // t),
        in_specs=in_specs,
        out_specs=out_specs,
        compiler_params=pltpu.CompilerParams(
            dimension_semantics=("arbitrary", "arbitrary"),
            vmem_limit_bytes=V7X_VMEM_LIMIT_BYTES),
        name="proj",
    )(x, win_t, wuq_t, wukv_t, cos_t, sin_t, gk_row, gdk_row)


def _attn_sweeps(k_ref, q_t, v_ref, score_bufs, prev_bufs, tk, bias=None, read_base=None):
    s_w, m_w, c_w = score_bufs
    s_r, m_r, c_r = prev_bufs
    n_keys, tq = s_w.shape
    n_chunks = n_keys // tk
    m_prev = m_r[...]
    m_new = None
    l = jnp.zeros((1, tq), F32)
    acc = jnp.zeros((v_ref.shape[2], tq), F32)
    if bias is not None:
        bias_ref, qi = bias
        n_bias = bias_ref.shape[0]
        far = (n_bias - 1) // 2
        n_near = n_bias - 2
        first_near = jnp.clip(qi - n_near // 2, 0, n_chunks - n_near)
        c_left = bias_ref[0, 0:1, :]
        c_right = bias_ref[n_bias - 1, 0:1, :]
    for j in range(n_chunks):
        if bias is None:
            rows_w = slice(j * tk, (j + 1) * tk)
            s = _dot(k_ref[0, 0, rows_w, :], q_t)
            m_c = jnp.max(s, axis=0, keepdims=True)
        else:
            c = first_near + j if j < n_near else lax.rem(first_near + j, n_chunks)
            rows_w = pl.ds(pl.multiple_of(c * tk, tk), tk)
            s = _dot(k_ref[0, 0, rows_w, :], q_t)
            if j < n_near:
                s = s + bias_ref[jnp.clip(c - qi, -far, far) + far]
                m_c = jnp.max(s, axis=0, keepdims=True)
                c_w[c] = jnp.zeros((1, tq), F32)
            else:
                off = jnp.where(c > qi, c_right, c_left)
                m_c = jnp.max(s, axis=0, keepdims=True) + off
                c_w[c] = off
        s_w[rows_w, :] = s
        m_new = m_c if m_new is None else jnp.maximum(m_new, m_c)

        if read_base is None:
            rows_r = slice(j * tk, (j + 1) * tk)
        else:
            rows_r = pl.ds(pl.multiple_of(read_base + j * tk, tk), tk)
        shift = m_prev if c_r is None else m_prev - c_r[j]
        p = jnp.exp2(s_r[rows_r, :] - shift)
        l = l + jnp.sum(p, axis=0, keepdims=True)
        acc = acc + _dot(v_ref[0, 0, :, rows_r], p.astype(BF16))
    m_w[...] = m_new
    return acc, l


def _mla_attn_kernel(q_ref, k_ref, vp_ref, vc_ref, o_ref, s_a, s_b, m_a, m_b, acc_d, l_d):
    tq = s_a.shape[1]
    bufs = ((s_a, m_a, None), (s_b, m_b, None))

    @pl.when(pl.program_id(0) == 0)
    def _():
        s_b[...] = jnp.zeros_like(s_b)
        m_b[...] = jnp.zeros_like(m_b)
        acc_d[...] = jnp.zeros_like(acc_d)
        l_d[...] = jnp.ones_like(l_d)

    def store(slot, acc, l):
        o_ref[0, slot * tq:(slot + 1) * tq, :] = (acc * (1.0 / l)).astype(o_ref.dtype).T

    zero = jnp.minimum(pl.program_id(0), 0)
    for i in range(MLA_GROUP):
        acc, l = _attn_sweeps(k_ref, q_ref[0, 0, :, i * tq:(i + 1) * tq], vp_ref if i == 0 else vc_ref,
                              bufs[i % 2], bufs[(i + 1) % 2], ATTN_TK,
                              read_base=None if i == 0 else zero)
        if i == 0:
            store(MLA_GROUP - 1, acc, l)
        else:
            store(i - 1, acc_d[i - 1], l_d[i - 1])
            acc_d[i - 1] = acc
            l_d[i - 1] = l


def _mla_attention(q_t, k, v_t):
    b, h, _, s = q_t.shape
    tq = ATTN_TQ
    n_qp = s // (MLA_GROUP * tq)
    n_pairs = b * h * n_qp

    def coords(p):
        return p // (h * n_qp), (p // n_qp) % h, p % n_qp

    def prev(g):
        return coords(jnp.maximum(g - 1, 0))

    def cur(g):
        return coords(jnp.minimum(g, n_pairs - 1))

    return pl.pallas_call(
        _mla_attn_kernel,
        out_shape=jax.ShapeDtypeStruct((b, s, h * MLA_V), BF16),
        grid=(n_pairs + 1,),
        in_specs=[
            pl.BlockSpec((1, 1, MLA_QK_PAD, MLA_GROUP * tq),
                         lambda g: (cur(g)[0], cur(g)[1], 0, cur(g)[2])),
            pl.BlockSpec((1, 1, s, MLA_QK_PAD), lambda g: (cur(g)[0], cur(g)[1], 0, 0)),
            pl.BlockSpec((1, 1, MLA_V, s), lambda g: (prev(g)[0], prev(g)[1], 0, 0)),
            pl.BlockSpec((1, 1, MLA_V, s), lambda g: (cur(g)[0], cur(g)[1], 0, 0)),
        ],
        out_specs=pl.BlockSpec((1, MLA_GROUP * tq, MLA_V),
                               lambda g: (prev(g)[0], prev(g)[2], prev(g)[1])),
        scratch_shapes=[pltpu.VMEM((s, tq), F32), pltpu.VMEM((s, tq), F32),
                        pltpu.VMEM((1, tq), F32), pltpu.VMEM((1, tq), F32),
                        pltpu.VMEM((MLA_GROUP - 1, MLA_V, tq), F32),
                        pltpu.VMEM((MLA_GROUP - 1, 1, tq), F32)],
        compiler_params=pltpu.CompilerParams(
            dimension_semantics=("arbitrary",),
            vmem_limit_bytes=V7X_VMEM_LIMIT_BYTES),
        name="mla_attn",
    )(q_t, k, v_t, v_t)


def _diff_attn_kernel(q_ref, k_ref, vp_ref, vc_ref, brow_ref,
                      lq1_ref, lk1_ref, lq2_ref, lk2_ref, g_ref, o_ref,
                      s_a, s_b, m_a, m_b, c_a, c_b, bias_ref, o_d, *, n_groups, n_qg, lam_init):
    g = pl.program_id(1)
    tq = s_a.shape[1]
    n_tiles = 2 * DIFF_GROUP_Q
    bufs = ((s_a, m_a, c_a), (s_b, m_b, c_b))

    @pl.when(g == 0)
    def _():
        s_b[...] = jnp.zeros_like(s_b)
        m_b[...] = jnp.zeros_like(m_b)
        c_b[...] = jnp.zeros_like(c_b)
        o_d[...] = jnp.zeros_like(o_d)
        n_bias, tk, _ = bias_ref.shape
        for j in range(n_bias):
            rows = jnp.broadcast_to(brow_ref[0, j], (tk, brow_ref.shape[-1]))
            bias_ref[j] = pltpu.roll(rows, 0, 1, stride=1, stride_axis=0)[:, :tq]

    qi0 = lax.rem(jnp.minimum(g, n_groups - 1), n_qg) * DIFF_GROUP_Q
    lam = (jnp.exp(jnp.sum(lq1_ref[...] * lk1_ref[...], axis=-1, keepdims=True))
           - jnp.exp(jnp.sum(lq2_ref[...] * lk2_ref[...], axis=-1, keepdims=True))
           + lam_init)

    def write(qt, o_map0, o_map1):
        o = o_map0 - lam * o_map1
        o = o * lax.rsqrt(jnp.mean(o * o, axis=0, keepdims=True) + EPS)
        o_ref[0, qt * tq:(qt + 1) * tq, :] = (o.T * (g_ref[...] * (1.0 - lam_init))).astype(o_ref.dtype)

    for i in range(n_tiles):
        qt, mp = divmod(i, 2)
        acc, l = _attn_sweeps(k_ref, q_ref[0, 0, mp, :, qt * tq:(qt + 1) * tq],
                              vp_ref if i == 0 else vc_ref, bufs[i % 2], bufs[(i + 1) % 2],
                              ATTN_TK, bias=(bias_ref, qi0 + qt),
                              read_base=None if i == 0 else jnp.minimum(g, 0))
        o = acc * (1.0 / l)
        if i == 0:
            write(DIFF_GROUP_Q - 1, o_d[n_tiles - 2], o)
            for q_prev in range(DIFF_GROUP_Q - 1):
                write(q_prev, o_d[2 * q_prev], o_d[2 * q_prev + 1])
        else:
            o_d[i - 1] = o


def _diff_attention(q_t, k, v_t, bias_rows, lq1, lk1, lq2, lk2, g_row, lam_init):
    b, h, _, _, s = q_t.shape
    tq = ATTN_TQ
    n_qg = s // (DIFF_GROUP_Q * tq)
    n_groups = b * n_qg
    n_bias, _, span = bias_rows.shape[1:]
    assert s // ATTN_TK >= n_bias - 2
    kern = functools.partial(_diff_attn_kernel, n_groups=n_groups, n_qg=n_qg, lam_init=lam_init)

    def prev(g):
        p = jnp.maximum(g - 1, 0)
        return p // n_qg, p % n_qg

    def cur(g):
        p = jnp.minimum(g, n_groups - 1)
        return p // n_qg, p % n_qg

    vec = pl.BlockSpec((1, DIFF_D), lambda hi, g: (0, 0))
    return pl.pallas_call(
        kern,
        out_shape=jax.ShapeDtypeStruct((b, s, h * DIFF_V), BF16),
        grid=(h, n_groups + 1),
        in_specs=[
            pl.BlockSpec((1, 1, 2, DIFF_V, DIFF_GROUP_Q * tq),
                         lambda hi, g: (cur(g)[0], hi, 0, 0, cur(g)[1])),
            pl.BlockSpec((1, 1, s, DIFF_V), lambda hi, g: (cur(g)[0], hi, 0, 0)),
            pl.BlockSpec((1, 1, DIFF_V, s), lambda hi, g: (prev(g)[0], hi, 0, 0)),
            pl.BlockSpec((1, 1, DIFF_V, s), lambda hi, g: (cur(g)[0], hi, 0, 0)),
            pl.BlockSpec((1, n_bias, 1, span), lambda hi, g: (hi, 0, 0, 0)),
            vec, vec, vec, vec,
            pl.BlockSpec((1, DIFF_V), lambda hi, g: (0, 0)),
        ],
        out_specs=pl.BlockSpec((1, DIFF_GROUP_Q * tq, DIFF_V),
                               lambda hi, g: (prev(g)[0], prev(g)[1], hi)),
        scratch_shapes=[pltpu.VMEM((s, tq), F32), pltpu.VMEM((s, tq), F32),
                        pltpu.VMEM((1, tq), F32), pltpu.VMEM((1, tq), F32),
                        pltpu.VMEM((s // ATTN_TK, 1, tq), F32), pltpu.VMEM((s // ATTN_TK, 1, tq), F32),
                        pltpu.VMEM((n_bias, ATTN_TK, tq), F32),
                        pltpu.VMEM((2 * DIFF_GROUP_Q - 1, DIFF_V, tq), F32)],
        compiler_params=pltpu.CompilerParams(
            dimension_semantics=("arbitrary", "arbitrary"),
            vmem_limit_bytes=V7X_VMEM_LIMIT_BYTES),
        name="diff_attn",
    )(q_t, k, v_t, v_t, bias_rows, lq1, lk1, lq2, lk2, g_row)


def _t5_bucket(rel):
    half = N_BUCKETS // 2
    ret = jnp.where(rel > 0, half, 0)
    n = jnp.abs(rel)
    max_exact = half // 2
    large = max_exact + (jnp.log(jnp.maximum(n, 1).astype(jnp.float32) / max_exact)
                         / math.log(MAX_DISTANCE / max_exact)
                         * (half - max_exact)).astype(jnp.int32)
    large = jnp.minimum(large, half - 1)
    return ret + jnp.where(n < max_exact, n, large)


def _bias_rows(rel_bias, t):
    far = pl.cdiv(t - 1 + MAX_DISTANCE, t)
    offs = jnp.arange(-far, far + 1, dtype=jnp.int32) * t
    span = 2 * t
    m = jnp.arange(span, dtype=jnp.int32)
    q_minus_k = jnp.where(m < t, m, m - span)
    rel = offs[:, None] - q_minus_k[None, :]
    table = rel_bias.astype(F32) * LOG2E
    bucket = _t5_bucket(rel)[None]
    rows = jnp.zeros((table.shape[1],) + rel.shape, F32)
    for bkt in range(N_BUCKETS):
        rows = jnp.where(bucket == bkt, table[bkt][:, None, None], rows)
    return rows[:, :, None, :]


def _mlp_kernel(x_ref, om_ref, od_ref, wo1_ref, wo2_ref, wup_ref, wdn_ref, o_ref, h_ref):
    for r in range(x_ref.shape[0] // MLP_ROWS):
        rows = slice(r * MLP_ROWS, (r + 1) * MLP_ROWS)
        x1 = (x_ref[rows, :] + _dot(om_ref[rows, :], wo1_ref[...])
              + _dot(od_ref[rows, :], wo2_ref[...]))
        o_ref[rows, :] = x1
        h_ref[rows, :] = (x1 * lax.rsqrt(jnp.mean(x1 * x1, axis=-1, keepdims=True) + EPS)
                          ).astype(BF16)

    for c in range(wup_ref.shape[1] // MLP_FF_CHUNK):
        cols = slice(c * MLP_FF_CHUNK, (c + 1) * MLP_FF_CHUNK)
        up = _dot(h_ref[...], wup_ref[:, cols])
        act = jnp.square(jnp.maximum(up, 0.0)).astype(BF16)
        o_ref[...] += _dot(act, wdn_ref[cols, :])


def _out_proj_mlp(x2d, o_mla, o_diff, wo1, wo2, wup, wdn):
    n, d = x2d.shape
    t = MLP_TOKENS
    resident = lambda a: pl.BlockSpec(a.shape, lambda i: (0, 0), pipeline_mode=pl.Buffered(1))
    return pl.pallas_call(
        _mlp_kernel,
        out_shape=jax.ShapeDtypeStruct((n, d), F32),
        grid=(n // t,),
        in_specs=[
            pl.BlockSpec((t, d), lambda i: (i, 0)),
            pl.BlockSpec((t, MLA_WIDTH), lambda i: (i, 0)),
            pl.BlockSpec((t, DIFF_WIDTH), lambda i: (i, 0)),
            resident(wo1), resident(wo2), resident(wup), resident(wdn),
        ],
        out_specs=pl.BlockSpec((t, d), lambda i: (i, 0)),
        scratch_shapes=[pltpu.VMEM((t, d), BF16)],
        compiler_params=pltpu.CompilerParams(
            dimension_semantics=("arbitrary",),
            vmem_limit_bytes=V7X_VMEM_LIMIT_BYTES),
        name="out_mlp",
    )(x2d, o_mla, o_diff, wo1, wo2, wup, wdn)


def _rope_tables_t(seq):
    inv = ROPE_THETA ** (-jnp.arange(0, MLA_ROPE, 2, dtype=jnp.float32) / MLA_ROPE)
    ang = jnp.arange(seq, dtype=jnp.float32)[:, None] * inv[None, :]
    ang = jnp.concatenate([ang, ang], axis=-1)
    return jnp.cos(ang).T, jnp.sin(ang).T


def kernel(x, attn_norm_w, w_in, q_a_norm_w, w_uq, kv_a_norm_w, w_ukv, mla_q_norm_w, mla_k_norm_w, diff_q_norm_w, diff_k_norm_w, lambda_q1, lambda_k1, lambda_q2, lambda_k2, diff_out_norm_w, w_out, mlp_norm_w, w_up, w_down, rel_bias):
    b, s, d = x.shape
    cos_t, sin_t = _rope_tables_t(s)
    bias_rows = _bias_rows(rel_bias, ATTN_TQ)
    for layer in range(DEPTH):
        lam_init = 0.8 - 0.6 * math.exp(-0.3 * layer)

        wi = w_in[layer].astype(F32) * attn_norm_w[layer].astype(F32)[:, None]
        c0, c1, c2, c3, c4 = (Q_LORA, Q_LORA + KV_LORA, Q_LORA + KV_LORA + MLA_ROPE,
                              Q_LORA + KV_LORA + MLA_ROPE + DIFF_QK_WIDTH,
                              Q_LORA + KV_LORA + MLA_ROPE + 2 * DIFF_QK_WIDTH)
        w_krope = wi[:, c1:c2]
        win_t = jnp.concatenate(
            [wi[:, :c1], w_krope, _rotate_half(w_krope), wi[:, c2:c3], wi[:, c3:c4], wi[:, c4:]],
            axis=1).T.astype(BF16)

        wq = (w_uq[layer].astype(F32) * q_a_norm_w[layer].astype(F32)[:, None]
              ).reshape(Q_LORA, MLA_HEADS, MLA_QK)
        wq_nope = wq[:, :, :MLA_NOPE].reshape(Q_LORA, -1)
        wq_rope = wq[:, :, MLA_NOPE:]
        wuq_t = jnp.concatenate(
            [wq_nope, wq_rope.reshape(Q_LORA, -1), _rotate_half(wq_rope).reshape(Q_LORA, -1)],
            axis=1).T.astype(BF16)

        wkv = (w_ukv[layer].astype(F32) * kv_a_norm_w[layer].astype(F32)[:, None]
               ).reshape(KV_LORA, MLA_HEADS, MLA_NOPE + MLA_V)
        wukv_t = jnp.concatenate(
            [wkv[:, :, :MLA_NOPE].reshape(KV_LORA, -1), wkv[:, :, MLA_NOPE:].reshape(KV_LORA, -1)],
            axis=1).T.astype(BF16)

        gk_row = jnp.concatenate(
            [mla_q_norm_w[layer].astype(F32) * mla_k_norm_w[layer].astype(F32),
             jnp.ones((MLA_QK_PAD - MLA_QK,), F32)])[None, :]
        gd = diff_q_norm_w[layer].astype(F32) * diff_k_norm_w[layer].astype(F32)
        gdk_row = jnp.concatenate([gd, gd])[None, :]

        qm_t, km, vm_t, qd_t, kd, vd_t = _projections(
            x, win_t, wuq_t, wukv_t, cos_t, sin_t, gk_row, gdk_row)

        o_mla = _mla_attention(qm_t, km, vm_t)
        o_diff = _diff_attention(
            qd_t, kd, vd_t, bias_rows,
            lambda_q1[layer].astype(F32)[None, :], lambda_k1[layer].astype(F32)[None, :],
            lambda_q2[layer].astype(F32)[None, :], lambda_k2[layer].astype(F32)[None, :],
            diff_out_norm_w[layer].astype(F32)[None, :], lam_init)

        wo = w_out[layer].astype(BF16)
        wup = (w_up[layer].astype(F32) * mlp_norm_w[layer].astype(F32)[:, None]).astype(BF16)
        x = _out_proj_mlp(
            x.reshape(b * s, d), o_mla.reshape(b * s, MLA_WIDTH), o_diff.reshape(b * s, DIFF_WIDTH),
            wo[:MLA_WIDTH], wo[MLA_WIDTH:], wup, w_down[layer].astype(BF16)).reshape(b, s, d)
    return x
```
